```python
import math
import jax, jax.numpy as jnp
from jax import lax
import numpy as np

D_MODEL = 2048
BATCH = 4
SEQ = 2048
DEPTH = 4
DEC_BATCH = 2
DEC_SEQ = 16384
PAST_LEN = 128

HY_CH = D_MODEL // 2
POOL_CH = D_MODEL - HY_CH
POOL_WINDOWS = (2, 4, 8, 16)
N_POOL_GROUPS = len(POOL_WINDOWS)
POOL_GROUP = POOL_CH // N_POOL_GROUPS
HY_ORDER = 2
D_IN = (HY_ORDER + 1) * HY_CH + POOL_CH
SHORT_K = 3
POS_BANDS = 16
POS_EMB = 2 * POS_BANDS + 1
FILT_HIDDEN = 64
N_DIR = 2
DECAY_TARGET = 1e-2
FAST_DECAY_PCT = 0.3
SLOW_DECAY_PCT = 1.5
N_MEM = 256
XA_HEADS = 4
XA_HEAD_DIM = D_MODEL // XA_HEADS
D_FF = 4 * D_MODEL
EPS = 1e-6

kernel_name = "hyena_pool_hybrid_encoder"


def _rmsnorm(x, g):
    xf = x.astype(jnp.float32)
    y = xf * lax.rsqrt(jnp.mean(xf * xf, axis=-1, keepdims=True) + EPS) * g.astype(jnp.float32)
    return y.astype(x.dtype)


def _pos_features(L):
    n = jnp.arange(L, dtype=jnp.float32)
    t = n / float(max(L - 1, 1))
    w = (2.0 * math.pi / L) * n
    f = jnp.linspace(1e-4, POS_BANDS - 1, POS_BANDS, dtype=jnp.float32)
    ang = w[:, None] * f[None, :]
    z = jnp.concatenate([t[:, None], jnp.cos(ang), -jnp.sin(ang)], axis=-1)
    return t, z


def _hyena_filter_spectrum(L, w1, b1, w2, b2, freq, w3, decay):
    f32 = jnp.float32
    t, z = _pos_features(L)
    fr = freq.astype(f32)
    h = jnp.sin(fr * (z @ w1.astype(f32) + b1.astype(f32)))
    h = jnp.sin(fr * (h @ w2.astype(f32) + b2.astype(f32)))
    h = (h @ w3.astype(f32)).reshape(L, HY_ORDER, N_DIR, HY_CH)
    h = h * jnp.exp(-t[:, None, None, None] * jnp.abs(decay.astype(f32))[None])
    h = h / jnp.sum(jnp.abs(h), axis=(0, 2), keepdims=True)
    hf, hb = h[:, :, 0], h[:, :, 1]
    k = jnp.concatenate([hf, jnp.zeros((1, HY_ORDER, HY_CH), f32), hb[:0:-1]], axis=0)
    return jnp.fft.rfft(k, axis=0)


def _long_conv(v, kf, bias):
    L = v.shape[1]
    V = jnp.fft.rfft(v, n=2 * L, axis=1)
    y = jnp.fft.irfft(V * kf[None], n=2 * L, axis=1)[:, :L]
    return y + v * bias


def _short_conv(u, w, b):
    L = u.shape[1]
    pad = SHORT_K // 2
    up = jnp.pad(u, ((0, 0), (pad, SHORT_K - 1 - pad), (0, 0)))
    y = up[:, 0:L] * w[0]
    for j in range(1, SHORT_K):
        y = y + up[:, j:j + L] * w[j]
    return y + b


def _hyena(u, conv_w, conv_b, kf, bias):
    uc = _short_conv(u.astype(jnp.float32), conv_w.astype(jnp.float32), conv_b.astype(jnp.float32))
    x1, x2, v = jnp.split(uc, HY_ORDER + 1, axis=-1)
    bias = bias.astype(jnp.float32)
    z = x1 * _long_conv(v, kf[:, 0], bias[0])
    z = x2 * _long_conv(z, kf[:, 1], bias[1])
    return z


def _multiscale_pool(p, pool_w, pool_scale):
    B, L, _ = p.shape
    pf = p.astype(jnp.float32)
    S = jnp.concatenate([jnp.zeros((B, 1, POOL_CH), jnp.float32), jnp.cumsum(pf, axis=1)], axis=1)
    pos = jnp.arange(L)
    diffs = []
    for g, win in enumerate(POOL_WINDOWS):
        sl = slice(g * POOL_GROUP, (g + 1) * POOL_GROUP)
        lo = jnp.clip(pos - win // 2, 0, L - 1)
        hi = jnp.clip(pos + win // 2 - 1, 0, L - 1)
        cnt = (hi - lo + 1).astype(jnp.float32)[None, :, None]
        Sg = S[:, :, sl]
        mean = (jnp.take(Sg, hi + 1, axis=1) - jnp.take(Sg, lo, axis=1)) / cnt
        diffs.append(mean - pf[:, :, sl])
    d = jnp.stack(diffs, axis=2)
    y = jnp.einsum('blgc,gcd->blgd', d, pool_w.astype(jnp.float32)).reshape(B, L, POOL_CH)
    return y * pool_scale.astype(jnp.float32)


def _cross_attn(h, m, w_q, w_kv, w_o):
    B, L, _ = h.shape
    M = m.shape[1]
    q = (h @ w_q).reshape(B, L, XA_HEADS, XA_HEAD_DIM)
    kv = (m @ w_kv).reshape(B, M, 2, XA_HEADS, XA_HEAD_DIM)
    k, v = kv[:, :, 0], kv[:, :, 1]
    s = jnp.einsum('blhd,bmhd->bhlm', q, k).astype(jnp.float32) * (XA_HEAD_DIM ** -0.5)
    a = jax.nn.softmax(s, axis=-1).astype(v.dtype)
    o = jnp.einsum('bhlm,bmhd->blhd', a, v).reshape(B, L, D_MODEL)
    return o @ w_o


def _layer(x, mem, g_mix, w_in, conv_w, conv_b, filt_w1, filt_b1, filt_w2, filt_b2, filt_freq, filt_w3,
           hy_decay, hy_bias, pool_w, pool_scale, g_hy, g_pool, w_out, g_xa, g_mem, w_q, w_kv, w_o,
           g_mlp, w_up, w_down):
    L = x.shape[1]
    h = _rmsnorm(x, g_mix)
    u = h @ w_in
    kf = _hyena_filter_spectrum(L, filt_w1, filt_b1, filt_w2, filt_b2, filt_freq, filt_w3, hy_decay)
    y_hy = _hyena(u[..., :(HY_ORDER + 1) * HY_CH], conv_w, conv_b, kf, hy_bias)
    y_pool = _multiscale_pool(u[..., (HY_ORDER + 1) * HY_CH:], pool_w, pool_scale)
    y = jnp.concatenate([_rmsnorm(y_hy, g_hy), _rmsnorm(y_pool, g_pool)], axis=-1).astype(x.dtype)
    x = x + y @ w_out
    x = x + _cross_attn(_rmsnorm(x, g_xa), _rmsnorm(mem, g_mem), w_q, w_kv, w_o)
    hm = _rmsnorm(x, g_mlp) @ w_up
    x = x + jnp.square(jax.nn.relu(hm)) @ w_down
    return x


def _trunk(x, mem, layer_weights, g_final):
    for i in range(DEPTH):
        x = _layer(x, mem, *[w[i] for w in layer_weights])
    return _rmsnorm(x, g_final)


def setup_inputs(seed: int = 0) -> dict:
    key = jax.random.key(seed)
    ks = iter(jax.random.split(key, 40))
    f32 = jnp.float32

    def nrm(shape, scale):
        return jax.random.normal(next(ks), shape, f32) * scale

    def gain(shape):
        return 1.0 + 0.01 * jax.random.normal(next(ks), shape, f32)

    min_decay = math.log(DECAY_TARGET) / SLOW_DECAY_PCT
    max_decay = math.log(DECAY_TARGET) / FAST_DECAY_PCT
    base_decay = jnp.abs(jnp.linspace(min_decay, max_decay, HY_CH, dtype=f32))
    out_scale = (2.0 * D_MODEL) ** -0.5
    return {
        "x_prompt": nrm((BATCH, SEQ, D_MODEL), 1.0),
        "x_sample": nrm((DEC_BATCH, DEC_SEQ, D_MODEL), 1.0),
        "mem_prompt": nrm((BATCH, N_MEM, D_MODEL), 1.0),
        "mem_sample": nrm((DEC_BATCH, N_MEM, D_MODEL), 1.0),
        "g_mix": gain((DEPTH, D_MODEL)),
        "w_in": nrm((DEPTH, D_MODEL, D_IN), D_MODEL ** -0.5),
        "conv_w": nrm((DEPTH, SHORT_K, (HY_ORDER + 1) * HY_CH), SHORT_K ** -0.5),
        "conv_b": nrm((DEPTH, (HY_ORDER + 1) * HY_CH), 0.02),
        "filt_w1": nrm((DEPTH, POS_EMB, FILT_HIDDEN), POS_EMB ** -0.5),
        "filt_b1": nrm((DEPTH, FILT_HIDDEN), 0.1),
        "filt_w2": nrm((DEPTH, FILT_HIDDEN, FILT_HIDDEN), FILT_HIDDEN ** -0.5),
        "filt_b2": nrm((DEPTH, FILT_HIDDEN), 0.1),
        "filt_freq": gain((DEPTH, FILT_HIDDEN)),
        "filt_w3": nrm((DEPTH, FILT_HIDDEN, HY_ORDER * N_DIR * HY_CH), FILT_HIDDEN ** -0.5),
        "hy_decay": base_decay * (1.0 + 0.05 * jax.random.normal(next(ks), (DEPTH, HY_ORDER, N_DIR, HY_CH), f32)),
        "hy_bias": nrm((DEPTH, HY_ORDER, HY_CH), 1.0),
        "pool_w": nrm((DEPTH, N_POOL_GROUPS, POOL_GROUP, POOL_GROUP), POOL_GROUP ** -0.5),
        "pool_scale": 1.0 + 0.1 * jax.random.normal(next(ks), (DEPTH, POOL_CH), f32),
        "g_hy": gain((DEPTH, HY_CH)),
        "g_pool": gain((DEPTH, POOL_CH)),
        "w_out": nrm((DEPTH, D_MODEL, D_MODEL), out_scale),
        "g_xa": gain((DEPTH, D_MODEL)),
        "g_mem": gain((DEPTH, D_MODEL)),
        "w_q": nrm((DEPTH, D_MODEL, D_MODEL), D_MODEL ** -0.5),
        "w_kv": nrm((DEPTH, D_MODEL, 2 * D_MODEL), D_MODEL ** -0.5),
        "w_o": nrm((DEPTH, D_MODEL, D_MODEL), out_scale),
        "g_mlp": gain((DEPTH, D_MODEL)),
        "w_up": nrm((DEPTH, D_MODEL, D_FF), D_MODEL ** -0.5),
        "w_down": nrm((DEPTH, D_FF, D_MODEL), (2.0 * D_FF) ** -0.5),
        "g_final": gain((D_MODEL,)),
    }


def reference(x_prompt, x_sample, mem_prompt, mem_sample, g_mix, w_in, conv_w, conv_b, filt_w1, filt_b1,
              filt_w2, filt_b2, filt_freq, filt_w3, hy_decay, hy_bias, pool_w, pool_scale, g_hy, g_pool,
              w_out, g_xa, g_mem, w_q, w_kv, w_o, g_mlp, w_up, w_down, g_final):
    layer_weights = (g_mix, w_in, conv_w, conv_b, filt_w1, filt_b1, filt_w2, filt_b2, filt_freq, filt_w3,
                     hy_decay, hy_bias, pool_w, pool_scale, g_hy, g_pool, w_out, g_xa, g_mem, w_q, w_kv, w_o,
                     g_mlp, w_up, w_down)
    y_prompt = _trunk(x_prompt, mem_prompt, layer_weights, g_final)
    y_sample = _trunk(x_sample, mem_sample, layer_weights, g_final)
    return (y_prompt, y_sample)
```

```python
import functools
import math

import numpy as np
import jax
import jax.numpy as jnp
from jax import lax
from jax.experimental import pallas as pl
from jax.experimental.pallas import tpu as pltpu

F32 = jnp.float32
BF16 = jnp.bfloat16

D_MODEL = 2048
HY_CH = 1024
POOL_CH = 1024
D_IN = 3 * HY_CH + POOL_CH
POOL_WINDOWS = (2, 4, 8, 16)
POOL_GROUP = POOL_CH // len(POOL_WINDOWS)
POOL_HALO = max(POOL_WINDOWS) // 2
POS_BANDS = 16
FILT_HIDDEN = 64
N_MEM = 256
XA_HEADS = 4
XA_HEAD_DIM = D_MODEL // XA_HEADS
D_FF = 4 * D_MODEL
EPS = 1e-6

LANES = 128
VMEM_LIMIT_BYTES = 56 * 1024 * 1024
HI = lax.Precision.HIGHEST


def _cparams(*sem):
    return pltpu.CompilerParams(dimension_semantics=sem, vmem_limit_bytes=VMEM_LIMIT_BYTES)


def _rms(x, g):
    return x * lax.rsqrt(jnp.mean(x * x, axis=-1, keepdims=True) + EPS) * g


def _const_spec(shape):
    nd = len(shape)
    return pl.BlockSpec(shape, lambda *_: (0,) * nd)


def _norm_matmul_body(x_ref, g_ref, w_ref, o_ref, h_ref):
    @pl.when(pl.program_id(1) == 0)
    def _():
        h_ref[...] = _rms(x_ref[...], g_ref[...]).astype(BF16)

    o_ref[...] = jnp.dot(h_ref[...], w_ref[...], preferred_element_type=F32).astype(o_ref.dtype)


def _norm_matmul(x, g, w, out_dtype, tm, tn, name):
    t, d = x.shape
    n = w.shape[1]
    return pl.pallas_call(
        _norm_matmul_body,
        grid=(t // tm, n // tn),
        in_specs=[pl.BlockSpec((tm, d), lambda i, j: (i, 0)),
                  pl.BlockSpec((1, d), lambda i, j: (0, 0)),
                  pl.BlockSpec((d, tn), lambda i, j: (0, j))],
        out_specs=pl.BlockSpec((tm, tn), lambda i, j: (i, j)),
        out_shape=jax.ShapeDtypeStruct((t, n), out_dtype),
        scratch_shapes=[pltpu.VMEM((tm, d), BF16)],
        compiler_params=_cparams("parallel", "arbitrary"),
        name=name,
    )(x, g.reshape(1, d), w)


def _mlp_body(x_ref, g_ref, wu_ref, wd_ref, gf_ref, o_ref, h_ref, *, final_norm):
    f = pl.program_id(1)

    @pl.when(f == 0)
    def _():
        x = x_ref[...]
        h_ref[...] = _rms(x, g_ref[...]).astype(BF16)
        o_ref[...] = x

    hm = jnp.dot(h_ref[...], wu_ref[...], preferred_element_type=F32)
    a = jnp.square(jnp.maximum(hm, 0.0)).astype(BF16)
    o_ref[...] += jnp.dot(a, wd_ref[...], preferred_element_type=F32)

    if final_norm:
        @pl.when(f == pl.num_programs(1) - 1)
        def _():
            o_ref[...] = _rms(o_ref[...], gf_ref[...])


def _mlp(x, g, wu, wd, g_final, final_norm, tm=512, tf=1024):
    t, d = x.shape
    ff = wu.shape[1]
    return pl.pallas_call(
        functools.partial(_mlp_body, final_norm=final_norm),
        grid=(t // tm, ff // tf),
        in_specs=[pl.BlockSpec((tm, d), lambda i, f: (i, 0)),
                  pl.BlockSpec((1, d), lambda i, f: (0, 0)),
                  pl.BlockSpec((d, tf), lambda i, f: (0, f)),
                  pl.BlockSpec((tf, d), lambda i, f: (f, 0)),
                  pl.BlockSpec((1, d), lambda i, f: (0, 0))],
        out_specs=pl.BlockSpec((tm, d), lambda i, f: (i, 0)),
        out_shape=jax.ShapeDtypeStruct((t, d), F32),
        scratch_shapes=[pltpu.VMEM((tm, d), BF16)],
        compiler_params=_cparams("parallel", "arbitrary"),
        name="mlp",
    )(x, g.reshape(1, d), wu, wd, g_final.reshape(1, d))


def _attn_body(x_ref, k_ref, v_ref, g_ref, wq_ref, wo_ref, o_ref):
    x = x_ref[0]
    h = _rms(x, g_ref[...]).astype(BF16)
    q = jnp.dot(h, wq_ref[...], preferred_element_type=F32)
    scale = XA_HEAD_DIM ** -0.5
    heads = []
    for hd in range(XA_HEADS):
        sl = slice(hd * XA_HEAD_DIM, (hd + 1) * XA_HEAD_DIM)
        s = lax.dot_general(q[:, sl].astype(BF16), k_ref[0, :, sl], (((1,), (1,)), ((), ())),
                            preferred_element_type=F32) * scale
        p = jnp.exp(s - jnp.max(s, axis=-1, keepdims=True))
        a = (p / jnp.sum(p, axis=-1, keepdims=True)).astype(BF16)
        heads.append(jnp.dot(a, v_ref[0, :, sl], preferred_element_type=F32).astype(BF16))
    o = jnp.concatenate(heads, axis=-1)
    o_ref[0] = x + jnp.dot(o, wo_ref[...], preferred_element_type=F32)


def _attn(x, kv, g, wq, wo, tq=512):
    b, l, d = x.shape
    return pl.pallas_call(
        _attn_body,
        grid=(b, l // tq),
        in_specs=[pl.BlockSpec((1, tq, d), lambda i, j: (i, j, 0)),
                  pl.BlockSpec((1, N_MEM, d), lambda i, j: (i, 0, 0)),
                  pl.BlockSpec((1, N_MEM, d), lambda i, j: (i, 0, 1)),
                  pl.BlockSpec((1, d), lambda i, j: (0, 0)),
                  pl.BlockSpec((d, d), lambda i, j: (0, 0)),
                  pl.BlockSpec((d, d), lambda i, j: (0, 0))],
        out_specs=pl.BlockSpec((1, tq, d), lambda i, j: (i, j, 0)),
        out_shape=jax.ShapeDtypeStruct((b, l, d), F32),
        compiler_params=_cparams("parallel", "arbitrary"),
        name="attn",
    )(x, kv, kv, g.reshape(1, d), wq, wo)


def _outproj_body(x_ref, yh_ref, yp_ref, gh_ref, wt_ref, wb_ref, o_ref):
    yh = _rms(yh_ref[...], gh_ref[...]).astype(BF16)
    acc = jnp.dot(yh, wt_ref[...], preferred_element_type=F32)
    acc += jnp.dot(yp_ref[...], wb_ref[...], preferred_element_type=F32)
    o_ref[...] = x_ref[...] + acc


def _outproj(x, yh, yp, g_hy, w_out, tm=512):
    t, d = x.shape
    return pl.pallas_call(
        _outproj_body,
        grid=(t // tm,),
        in_specs=[pl.BlockSpec((tm, d), lambda i: (i, 0)),
                  pl.BlockSpec((tm, HY_CH), lambda i: (i, 0)),
                  pl.BlockSpec((tm, POOL_CH), lambda i: (i, 0)),
                  pl.BlockSpec((1, HY_CH), lambda i: (0, 0)),
                  pl.BlockSpec((HY_CH, d), lambda i: (0, 0)),
                  pl.BlockSpec((POOL_CH, d), lambda i: (1, 0))],
        out_specs=pl.BlockSpec((tm, d), lambda i: (i, 0)),
        out_shape=jax.ShapeDtypeStruct((t, d), F32),
        compiler_params=_cparams("parallel"),
        name="outproj",
    )(x, yh, yp, g_hy.reshape(1, HY_CH), w_out, w_out)


def _shift_rows(x, down):
    r = x.shape[-2]
    rows = lax.broadcasted_iota(jnp.int32, x.shape, x.ndim - 2)
    if down:
        return jnp.where(rows == 0, 0.0, pltpu.roll(x, 1, axis=x.ndim - 2))
    return jnp.where(rows == r - 1, 0.0, pltpu.roll(x, r - 1, axis=x.ndim - 2))


def _wrap_prev(x, is_first):
    return jnp.where(is_first, _shift_rows(x, True), x)


def _wrap_next(x, is_last):
    return jnp.where(is_last, _shift_rows(x, False), x)


def _short_conv_planes(main_ref, prev_ref, next_ref, w_ref, b_ref, is_first, is_last):
    tj = main_ref.shape[1]
    planes = [_wrap_prev(prev_ref[:, 0], is_first)]
    planes += [main_ref[:, jj] for jj in range(tj)]
    planes += [_wrap_next(next_ref[:, 0], is_last)]
    w0, w1, w2, b = w_ref[0:1, :], w_ref[1:2, :], w_ref[2:3, :], b_ref[...]
    return [planes[jj] * w0 + planes[jj + 1] * w1 + planes[jj + 2] * w2 + b for jj in range(tj)]


def _s1_fwd_body(*refs, conv):
    if conv:
        main_ref, prev_ref, next_ref, cw_ref, cb_ref, w_ref, o_ref = refs
        j = pl.program_id(1)
        planes = _short_conv_planes(main_ref, prev_ref, next_ref, cw_ref, cb_ref,
                                    j == 0, j == pl.num_programs(1) - 1)
    else:
        main_ref, w_ref, o_ref = refs
        planes = [main_ref[:, jj] for jj in range(main_ref.shape[1])]
    n1 = o_ref.shape[2]
    for jj, v in enumerate(planes):
        nb, r, ct = v.shape
        res = jnp.dot(w_ref[jj], v.reshape(nb * r, ct).astype(BF16), preferred_element_type=F32)
        o_ref[0, :, :, jj, :] = res.reshape(2, n1, ct)


def _s1_fwd(src, goff, tab, conv_w, conv_b, cfg, name):
    n2, r, n1, tj, ct = cfg["N2"], cfg["R"], cfg["N1"], cfg["tj"], cfg["ct"]
    b = src.shape[0]
    p = b // 2
    conv = conv_w is not None
    main = pl.BlockSpec((2, tj, r, ct), lambda p_, j, c: (p_, j, 0, goff + c))
    in_specs = [main]
    args = [src]
    if conv:
        in_specs += [pl.BlockSpec((2, 1, r, ct), lambda p_, j, c: (p_, (j * tj + n2 - 1) % n2, 0, goff + c)),
                     pl.BlockSpec((2, 1, r, ct), lambda p_, j, c: (p_, ((j + 1) * tj) % n2, 0, goff + c)),
                     pl.BlockSpec((3, ct), lambda p_, j, c: (0, c)),
                     pl.BlockSpec((1, ct), lambda p_, j, c: (0, c))]
        args += [src, src, conv_w, conv_b]
    in_specs.append(pl.BlockSpec((tj, 2 * n1, tab.shape[2]), lambda p_, j, c: (j, 0, 0)))
    args.append(tab)
    return pl.pallas_call(
        functools.partial(_s1_fwd_body, conv=conv),
        grid=(p, n2 // tj, HY_CH // ct),
        in_specs=in_specs,
        out_specs=pl.BlockSpec((1, 2, n1, tj, ct), lambda p_, j, c: (p_, 0, 0, j, c)),
        out_shape=jax.ShapeDtypeStruct((p, 2, n1, n2, HY_CH), F32),
        compiler_params=_cparams("parallel", "arbitrary", "arbitrary"),
        name=name,
    )(*args)


def _s2_body(a_ref, kf_ref, f_ref, g_ref, o_ref):
    tk, n2, ct = a_ref.shape[2], a_ref.shape[3], a_ref.shape[4]
    for kk in range(tk):
        x = a_ref[0, :, kk].reshape(2 * n2, ct).astype(BF16)
        s = jnp.dot(f_ref[...], x, preferred_element_type=F32)
        sr, si = s[:n2], s[n2:]
        kr, ki = kf_ref[0, 0, kk], kf_ref[0, 1, kk]
        y = jnp.concatenate([sr * kr - si * ki, sr * ki + si * kr], axis=0).astype(BF16)
        o_ref[0, :, kk] = jnp.dot(g_ref[...], y, preferred_element_type=F32).reshape(2, n2, ct)


def _s2(a, kf, order, f2, g2, cfg):
    n2, n1, tk, ct = cfg["N2"], cfg["N1"], cfg["tk"], cfg["ct"]
    p = a.shape[0]
    blk = (1, 2, tk, n2, ct)
    return pl.pallas_call(
        _s2_body,
        grid=(p, n1 // tk, HY_CH // ct),
        in_specs=[pl.BlockSpec(blk, lambda p_, k, c: (p_, 0, k, 0, c)),
                  pl.BlockSpec(blk, lambda p_, k, c: (order, 0, k, 0, c)),
                  _const_spec((2 * n2, 2 * n2)), _const_spec((2 * n2, 2 * n2))],
        out_specs=pl.BlockSpec(blk, lambda p_, k, c: (p_, 0, k, 0, c)),
        out_shape=jax.ShapeDtypeStruct(a.shape, F32),
        compiler_params=_cparams("parallel", "arbitrary", "arbitrary"),
        name="conv_s2",
    )(a, kf, f2, g2)


def _s1_inv_body(*refs, vconv):
    if vconv:
        (a_ref, w_ref, vm_ref, vp_ref, vn_ref, vcw_ref, vcb_ref,
         gm_ref, gp_ref, gn_ref, gcw_ref, gcb_ref, bias_ref, o_ref) = refs
    else:
        a_ref, w_ref, vm_ref, gm_ref, gp_ref, gn_ref, gcw_ref, gcb_ref, bias_ref, o_ref = refs
    j = pl.program_id(1)
    first, last = j == 0, j == pl.num_programs(1) - 1
    tj = gm_ref.shape[1]
    gates = _short_conv_planes(gm_ref, gp_ref, gn_ref, gcw_ref, gcb_ref, first, last)
    if vconv:
        vs = _short_conv_planes(vm_ref, vp_ref, vn_ref, vcw_ref, vcb_ref, first, last)
    else:
        vs = [vm_ref[:, jj] for jj in range(tj)]
    n1, ct = a_ref.shape[2], a_ref.shape[4]
    bias = bias_ref[...]
    for jj in range(tj):
        x = a_ref[0, :, :, jj, :].reshape(2 * n1, ct).astype(BF16)
        y = jnp.dot(w_ref[jj], x, preferred_element_type=F32).reshape(vs[jj].shape)
        o_ref[:, jj] = gates[jj] * (y + vs[jj] * bias)


def _s1_inv(a, tab, vsrc, v_goff, v_cw, v_cb, u, g_goff, g_cw, g_cb, bias, cfg, name):
    n2, r, n1, tj, ct = cfg["N2"], cfg["R"], cfg["N1"], cfg["tj"], cfg["ct"]
    p = a.shape[0]
    vconv = v_cw is not None

    def plane_specs(goff):
        return [pl.BlockSpec((2, tj, r, ct), lambda p_, j, c: (p_, j, 0, goff + c)),
                pl.BlockSpec((2, 1, r, ct), lambda p_, j, c: (p_, (j * tj + n2 - 1) % n2, 0, goff + c)),
                pl.BlockSpec((2, 1, r, ct), lambda p_, j, c: (p_, ((j + 1) * tj) % n2, 0, goff + c))]

    wspecs = [pl.BlockSpec((3, ct), lambda p_, j, c: (0, c)), pl.BlockSpec((1, ct), lambda p_, j, c: (0, c))]
    in_specs = [pl.BlockSpec((1, 2, n1, tj, ct), lambda p_, j, c: (p_, 0, 0, j, c)),
                pl.BlockSpec((tj, 2 * r, 2 * n1), lambda p_, j, c: (j, 0, 0))]
    args = [a, tab]
    if vconv:
        in_specs += plane_specs(v_goff) + wspecs
        args += [vsrc, vsrc, vsrc, v_cw, v_cb]
    else:
        in_specs += plane_specs(v_goff)[:1]
        args += [vsrc]
    in_specs += plane_specs(g_goff) + wspecs + [pl.BlockSpec((1, ct), lambda p_, j, c: (0, c))]
    args += [u, u, u, g_cw, g_cb, bias]
    return pl.pallas_call(
        functools.partial(_s1_inv_body, vconv=vconv),
        grid=(p, n2 // tj, HY_CH // ct),
        in_specs=in_specs,
        out_specs=pl.BlockSpec((2, tj, r, ct), lambda p_, j, c: (p_, j, 0, c)),
        out_shape=jax.ShapeDtypeStruct((2 * p, n2, r, HY_CH), F32),
        compiler_params=_cparams("parallel", "arbitrary", "arbitrary"),
        name=name,
    )(*args)


def _filt_gen_body(fv_ref, w1_ref, b1_ref, w2_ref, b2_ref, fr_ref, w3_ref, dec_ref, k_ref, nrm_ref,
                   *, seq_len, n2_total):
    tj, n1 = k_ref.shape[1], k_ref.shape[2]
    r = n1 // 2
    j0 = pl.program_id(0) * tj

    @pl.when(pl.program_id(0) == 0)
    def _():
        nrm_ref[...] = jnp.zeros_like(nrm_ref)

    lane = lax.broadcasted_iota(jnp.int32, (n1, LANES), 1)
    row = lax.broadcasted_iota(jnp.int32, (n1, LANES), 0)
    for jj in range(tj):
        n2 = j0 + jj
        pos_f = row * n2_total + n2
        pos_b = (n1 - row) * n2_total - n2
        at_l = jnp.logical_and(row == r, n2 == 0)
        pos = jnp.where(row < r, pos_f, jnp.where(at_l, 0, pos_b)).astype(F32)
        t = pos / float(max(seq_len - 1, 1))
        ang = ((2.0 * math.pi / seq_len) * pos) * fv_ref[...]
        z = jnp.where(lane == 0, t, jnp.where(lane <= POS_BANDS, jnp.cos(ang),
                                              jnp.where(lane <= 2 * POS_BANDS, -jnp.sin(ang), 0.0)))
        h = jnp.sin(fr_ref[...] * (jnp.dot(z, w1_ref[...], precision=HI, preferred_element_type=F32) + b1_ref[...]))
        h = jnp.sin(fr_ref[...] * (jnp.dot(h, w2_ref[...], precision=HI, preferred_element_type=F32) + b2_ref[...]))
        tcol = t[:, 0:1]
        rowb = lax.broadcasted_iota(jnp.int32, (r, HY_CH), 0)
        drop = jnp.logical_and(rowb == 0, n2 == 0)
        for d in range(2):
            hd, td = h[d * r:(d + 1) * r], tcol[d * r:(d + 1) * r]
            for o in range(2):
                val = jnp.dot(hd, w3_ref[o, d], precision=HI, preferred_element_type=F32)
                val = val * jnp.exp(-td * jnp.abs(dec_ref[o, d]))
                nrm_ref[o] += jnp.sum(jnp.abs(val), axis=0, keepdims=True)
                if d == 1:
                    val = jnp.where(drop, 0.0, val)
                k_ref[o, jj, d * r:(d + 1) * r, :] = val


def _filt_gen(fvec, w1p, b1, w2, b2, freq, w3, dec, cfg, seq_len):
    n2, n1 = cfg["N2"], cfg["N1"]
    tj = cfg["tj_gen"]
    return pl.pallas_call(
        functools.partial(_filt_gen_body, seq_len=seq_len, n2_total=n2),
        grid=(n2 // tj,),
        in_specs=[_const_spec((1, LANES)), _const_spec((LANES, FILT_HIDDEN)), _const_spec((1, FILT_HIDDEN)),
                  _const_spec((FILT_HIDDEN, FILT_HIDDEN)), _const_spec((1, FILT_HIDDEN)),
                  _const_spec((1, FILT_HIDDEN)), _const_spec((2, 2, FILT_HIDDEN, HY_CH)),
                  _const_spec((2, 2, 1, HY_CH))],
        out_specs=[pl.BlockSpec((2, tj, n1, HY_CH), lambda j: (0, j, 0, 0)),
                   _const_spec((2, 1, HY_CH))],
        out_shape=[jax.ShapeDtypeStruct((2, n2, n1, HY_CH), F32),
                   jax.ShapeDtypeStruct((2, 1, HY_CH), F32)],
        compiler_params=_cparams("arbitrary"),
        name="filt_gen",
    )(fvec, w1p, b1, w2, b2, freq, w3, dec)


def _filt_s1_body(k_ref, w_ref, o_ref):
    n1 = o_ref.shape[2]
    for jj in range(k_ref.shape[1]):
        x = k_ref[0, jj]
        res = jnp.dot(w_ref[jj], x.astype(BF16), preferred_element_type=F32)
        o_ref[0, :, :, jj, :] = res.reshape(2, n1, x.shape[-1])


def _filt_s1(ksig, tab, cfg):
    n2, n1, tj, ct = cfg["N2"], cfg["N1"], cfg["tj"], cfg["ct"]
    return pl.pallas_call(
        _filt_s1_body,
        grid=(2, n2 // tj, HY_CH // ct),
        in_specs=[pl.BlockSpec((1, tj, n1, ct), lambda o, j, c: (o, j, 0, c)),
                  pl.BlockSpec((tj, 2 * n1, n1), lambda o, j, c: (j, 0, 0))],
        out_specs=pl.BlockSpec((1, 2, n1, tj, ct), lambda o, j, c: (o, 0, 0, j, c)),
        out_shape=jax.ShapeDtypeStruct((2, 2, n1, n2, HY_CH), F32),
        compiler_params=_cparams("parallel", "arbitrary", "arbitrary"),
        name="filt_s1",
    )(ksig, tab)


def _filt_s2_body(a_ref, nrm_ref, f_ref, o_ref):
    tk, n2, ct = a_ref.shape[2], a_ref.shape[3], a_ref.shape[4]
    inv = 1.0 / nrm_ref[0]
    for kk in range(tk):
        x = a_ref[0, :, kk].reshape(2 * n2, ct).astype(BF16)
        s = jnp.dot(f_ref[...], x, preferred_element_type=F32) * inv
        o_ref[0, :, kk] = s.reshape(2, n2, ct)


def _filt_s2(a, nrm, f2, cfg):
    n2, n1, tk, ct = cfg["N2"], cfg["N1"], cfg["tk"], cfg["ct"]
    blk = (1, 2, tk, n2, ct)
    return pl.pallas_call(
        _filt_s2_body,
        grid=(2, n1 // tk, HY_CH // ct),
        in_specs=[pl.BlockSpec(blk, lambda o, k, c: (o, 0, k, 0, c)),
                  pl.BlockSpec((1, 1, ct), lambda o, k, c: (o, 0, c)),
                  _const_spec((2 * n2, 2 * n2))],
        out_specs=pl.BlockSpec(blk, lambda o, k, c: (o, 0, k, 0, c)),
        out_shape=jax.ShapeDtypeStruct(a.shape, F32),
        compiler_params=_cparams("parallel", "arbitrary", "arbitrary"),
        name="filt_s2",
    )(a, nrm, f2)


def _pool_body(m_ref, p_ref, n_ref, pw_ref, ps_ref, gp_ref, o_ref, *, seq_len, n2_total):
    tj, r = m_ref.shape[1], m_ref.shape[2]
    j = pl.program_id(1)
    first, last = j == 0, j == pl.num_programs(1) - 1
    h = POOL_HALO
    j0 = j * tj
    plane = lax.broadcasted_iota(jnp.int32, (tj, r, POOL_GROUP), 0)
    row = lax.broadcasted_iota(jnp.int32, (tj, r, POOL_GROUP), 1)
    t = row * n2_total + j0 + plane
    outs = []
    for g, win in enumerate(POOL_WINDOWS):
        sl = slice(g * POOL_GROUP, (g + 1) * POOL_GROUP)
        ext = jnp.concatenate([_wrap_prev(p_ref[0, :, :, sl], first), m_ref[0, :, :, sl],
                               _wrap_next(n_ref[0, :, :, sl], last)], axis=0)
        acc = ext[h - win // 2:h - win // 2 + tj]
        for o in range(1 - win // 2, win // 2):
            acc = acc + ext[h + o:h + o + tj]
        lo = jnp.maximum(t - win // 2, 0)
        hi = jnp.minimum(t + win // 2 - 1, seq_len - 1)
        cnt = (hi - lo + 1).astype(F32)
        dgrp = acc / cnt - ext[h:h + tj]
        outs.append(jnp.dot(dgrp.reshape(tj * r, POOL_GROUP).astype(BF16), pw_ref[g],
                            preferred_element_type=F32))
    y = jnp.concatenate(outs, axis=-1) * ps_ref[...]
    o_ref[0] = _rms(y, gp_ref[...]).astype(BF16).reshape(tj, r, POOL_CH)


def _pool(u, pool_w, pool_scale, g_pool, cfg, seq_len):
    n2, r, tj = cfg["N2"], cfg["R"], cfg["tj_pool"]
    b = u.shape[0]
    goff = 3 * HY_CH // POOL_CH
    h = POOL_HALO
    nh = n2 // h
    return pl.pallas_call(
        functools.partial(_pool_body, seq_len=seq_len, n2_total=n2),
        grid=(b, n2 // tj),
        in_specs=[pl.BlockSpec((1, tj, r, POOL_CH), lambda i, j: (i, j, 0, goff)),
                  pl.BlockSpec((1, h, r, POOL_CH), lambda i, j: (i, (j * (tj // h) + nh - 1) % nh, 0, goff)),
                  pl.BlockSpec((1, h, r, POOL_CH), lambda i, j: (i, ((j + 1) * (tj // h)) % nh, 0, goff)),
                  _const_spec((len(POOL_WINDOWS), POOL_GROUP, POOL_GROUP)),
                  _const_spec((1, POOL_CH)), _const_spec((1, POOL_CH))],
        out_specs=pl.BlockSpec((1, tj, r, POOL_CH), lambda i, j: (i, j, 0, 0)),
        out_shape=jax.ShapeDtypeStruct((b, n2, r, POOL_CH), BF16),
        compiler_params=_cparams("parallel", "arbitrary"),
        name="pool",
    )(u, u, u, pool_w, pool_scale, g_pool)


def _fft_cfg(seq_len):
    n2 = 128 if seq_len >= 8192 else 64
    r = seq_len // n2
    assert r % 8 == 0 and r * n2 == seq_len
    return dict(N2=n2, R=r, N1=2 * r, tj=8, tk=8, ct=256, tj_gen=2, tj_pool=8)


def _dft_tables(cfg):
    n2, r, n1 = cfg["N2"], cfg["R"], cfg["N1"]
    n = n1 * n2
    k1 = jnp.arange(n1, dtype=jnp.int32)[None, :, None]
    t = (jnp.arange(n1, dtype=jnp.int32)[None, None, :] * n2 + jnp.arange(n2, dtype=jnp.int32)[:, None, None])
    ang = (-2.0 * math.pi / n) * ((k1 * t) % n).astype(F32)
    mr, mi = jnp.cos(ang), jnp.sin(ang)
    mrd, mid = mr[:, :, :r], mi[:, :, :r]
    wd = jnp.concatenate([jnp.concatenate([mrd, -mid], axis=2), jnp.concatenate([mid, mrd], axis=2)], axis=1)
    winv = jnp.swapaxes(wd, 1, 2) * (1.0 / n)
    wf = jnp.concatenate([mr, mi], axis=1)
    kk = np.arange(n2)
    a2 = -2.0 * np.pi * ((kk[:, None] * kk[None, :]) % n2) / n2
    fr, fi = np.cos(a2), np.sin(a2)
    f2 = np.block([[fr, -fi], [fi, fr]])
    g2 = np.block([[fr, fi], [-fi, fr]])
    return dict(wd=wd.astype(BF16), winv=winv.astype(BF16), wf=wf.astype(BF16),
                f2=jnp.asarray(f2, BF16), g2=jnp.asarray(g2, BF16))


def _to_plane_order(x, cfg):
    b, l, d = x.shape
    return jnp.swapaxes(x.reshape(b, cfg["R"], cfg["N2"], d), 1, 2).reshape(b * l, d)


def _from_plane_order(x, b, cfg):
    d = x.shape[-1]
    return jnp.swapaxes(x.reshape(b, cfg["N2"], cfg["R"], d), 1, 2).reshape(b, cfg["N2"] * cfg["R"], d)


def _trunk(x, mem, lw, g_final, depth):
    b, seq_len, d = x.shape
    cfg = _fft_cfg(seq_len)
    tabs = _dft_tables(cfg)
    n2, r, ct = cfg["N2"], cfg["R"], cfg["ct"]
    xs = _to_plane_order(x, cfg)
    memf = mem.reshape(b * N_MEM, d)
    fvec = np.zeros((1, LANES), np.float32)
    bands = np.linspace(1e-4, POS_BANDS - 1, POS_BANDS, dtype=np.float32)
    fvec[0, 1:1 + POS_BANDS] = bands
    fvec[0, 1 + POS_BANDS:1 + 2 * POS_BANDS] = bands
    fvec = jnp.asarray(fvec)
    cpb = HY_CH // ct
    for i in range(depth):
        w = {k: v[i] for k, v in lw.items()}
        w1p = jnp.zeros((LANES, FILT_HIDDEN), F32).at[:2 * POS_BANDS + 1].set(w["filt_w1"])
        ksig, nrm = _filt_gen(fvec, w1p, w["filt_b1"].reshape(1, -1), w["filt_w2"], w["filt_b2"].reshape(1, -1),
                              w["filt_freq"].reshape(1, -1),
                              jnp.transpose(w["filt_w3"].reshape(FILT_HIDDEN, 2, 2, HY_CH), (1, 2, 0, 3)),
                              w["hy_decay"].reshape(2, 2, 1, HY_CH), cfg, seq_len)
        kf = _filt_s2(_filt_s1(ksig, tabs["wf"], cfg), nrm, tabs["f2"], cfg)
        u = _norm_matmul(xs, w["g_mix"], w["w_in"], F32, 512, 1024, "in_proj").reshape(b, n2, r, D_IN)
        cw, cbias = w["conv_w"], w["conv_b"].reshape(1, -1)
        cws = [cw[:, k * HY_CH:(k + 1) * HY_CH] for k in range(3)]
        cbs = [cbias[:, k * HY_CH:(k + 1) * HY_CH] for k in range(3)]
        hb = w["hy_bias"]
        a = _s1_fwd(u, 2 * cpb, tabs["wd"], cws[2], cbs[2], cfg, "conv1_s1")
        a = _s2(a, kf, 0, tabs["f2"], tabs["g2"], cfg)
        z1 = _s1_inv(a, tabs["winv"], u, 2 * cpb, cws[2], cbs[2], u, 0, cws[0], cbs[0], hb[0:1], cfg, "conv1_inv")
        a = _s1_fwd(z1, 0, tabs["wd"], None, None, cfg, "conv2_s1")
        a = _s2(a, kf, 1, tabs["f2"], tabs["g2"], cfg)
        yh = _s1_inv(a, tabs["winv"], z1, 0, None, None, u, cpb, cws[1], cbs[1], hb[1:2], cfg, "conv2_inv")
        yp = _pool(u, w["pool_w"], w["pool_scale"].reshape(1, -1), w["g_pool"].reshape(1, -1), cfg, seq_len)
        xs = _outproj(xs, yh.reshape(b * seq_len, HY_CH), yp.reshape(b * seq_len, POOL_CH), w["g_hy"], w["w_out"])
        kv = _norm_matmul(memf, w["g_mem"], w["w_kv"], BF16, 512, 1024, "kv_proj").reshape(b, N_MEM, 2 * d)
        xs = _attn(xs.reshape(b, seq_len, d), kv, w["g_xa"], w["w_q"], w["w_o"]).reshape(b * seq_len, d)
        xs = _mlp(xs, w["g_mlp"], w["w_up"], w["w_down"], g_final, i == depth - 1)
    return _from_plane_order(xs, b, cfg)


_MATMUL_WEIGHTS = ("w_in", "pool_w", "w_out", "w_q", "w_kv", "w_o", "w_up", "w_down")


def kernel(x_prompt, x_sample, mem_prompt, mem_sample, g_mix, w_in, conv_w, conv_b, filt_w1, filt_b1, filt_w2, filt_b2, filt_freq, filt_w3, hy_decay, hy_bias, pool_w, pool_scale, g_hy, g_pool, w_out, g_xa, g_mem, w_q, w_kv, w_o, g_mlp, w_up, w_down, g_final):
    lw = dict(g_mix=g_mix, w_in=w_in, conv_w=conv_w, conv_b=conv_b, filt_w1=filt_w1, filt_b1=filt_b1,
              filt_w2=filt_w2, filt_b2=filt_b2, filt_freq=filt_freq, filt_w3=filt_w3, hy_decay=hy_decay,
              hy_bias=hy_bias, pool_w=pool_w, pool_scale=pool_scale, g_hy=g_hy, g_pool=g_pool, w_out=w_out,
              g_xa=g_xa, g_mem=g_mem, w_q=w_q, w_kv=w_kv, w_o=w_o, g_mlp=g_mlp, w_up=w_up, w_down=w_down)
    lw = {k: (v.astype(BF16) if k in _MATMUL_WEIGHTS else v) for k, v in lw.items()}
    depth = g_mix.shape[0]
    y_prompt = _trunk(x_prompt, mem_prompt, lw, g_final, depth)
    y_sample = _trunk(x_sample, mem_sample, lw, g_final, depth)
    return (y_prompt, y_sample)
```

```python
import functools
import math

import numpy as np
import jax
import jax.numpy as jnp
from jax import lax
from jax.experimental import pallas as pl
from jax.experimental.pallas import tpu as pltpu

F32 = jnp.float32
BF16 = jnp.bfloat16
U32 = jnp.uint32

D_MODEL = 2048
HY_CH = 1024
POOL_CH = 1024
D_IN = 3 * HY_CH + POOL_CH
POOL_WINDOWS = (2, 4, 8, 16)
POOL_GROUP = POOL_CH // len(POOL_WINDOWS)
POOL_HALO = max(POOL_WINDOWS) // 2
POS_BANDS = 16
FILT_HIDDEN = 64
N_MEM = 256
XA_HEADS = 4
XA_HEAD_DIM = D_MODEL // XA_HEADS
D_FF = 4 * D_MODEL
EPS = 1e-6

LANES = 128
SUBLANES = 8
VMEM_LIMIT_BYTES = 58 * 1024 * 1024
HI = lax.Precision.HIGHEST


def _cparams(*sem):
    return pltpu.CompilerParams(dimension_semantics=sem, vmem_limit_bytes=VMEM_LIMIT_BYTES)


def _rms(x, g):
    return x * lax.rsqrt(jnp.mean(x * x, axis=-1, keepdims=True) + EPS) * g


def _const_spec(shape):
    nd = len(shape)
    return pl.BlockSpec(shape, lambda *_: (0,) * nd)


def _norm_matmul_body(x_ref, g_ref, w_ref, o_ref, h_ref):
    @pl.when(pl.program_id(1) == 0)
    def _():
        h_ref[...] = _rms(x_ref[...], g_ref[...]).astype(BF16)

    o_ref[...] = jnp.dot(h_ref[...], w_ref[...], preferred_element_type=F32).astype(o_ref.dtype)


def _norm_matmul(x, g, w, out_dtype, tm, tn, name):
    t, d = x.shape
    n = w.shape[1]
    return pl.pallas_call(
        _norm_matmul_body,
        grid=(t // tm, n // tn),
        in_specs=[pl.BlockSpec((tm, d), lambda i, j: (i, 0)),
                  pl.BlockSpec((1, d), lambda i, j: (0, 0)),
                  pl.BlockSpec((d, tn), lambda i, j: (0, j))],
        out_specs=pl.BlockSpec((tm, tn), lambda i, j: (i, j)),
        out_shape=jax.ShapeDtypeStruct((t, n), out_dtype),
        scratch_shapes=[pltpu.VMEM((tm, d), BF16)],
        compiler_params=_cparams("parallel", "arbitrary"),
        name=name,
    )(x, g.reshape(1, d), w)


def _mlp_body(x_ref, g_ref, wu_ref, wd_ref, gf_ref, o_ref, h_ref, *, final_norm):
    f = pl.program_id(1)

    @pl.when(f == 0)
    def _():
        x = x_ref[...]
        h_ref[...] = _rms(x, g_ref[...]).astype(BF16)
        o_ref[...] = x

    hm = jnp.dot(h_ref[...], wu_ref[...], preferred_element_type=F32)
    a = jnp.square(jnp.maximum(hm, 0.0)).astype(BF16)
    o_ref[...] += jnp.dot(a, wd_ref[...], preferred_element_type=F32)

    if final_norm:
        @pl.when(f == pl.num_programs(1) - 1)
        def _():
            o_ref[...] = _rms(o_ref[...], gf_ref[...])


def _mlp(x, g, wu, wd, g_final, final_norm, tm=512, tf=1024):
    t, d = x.shape
    ff = wu.shape[1]
    return pl.pallas_call(
        functools.partial(_mlp_body, final_norm=final_norm),
        grid=(t // tm, ff // tf),
        in_specs=[pl.BlockSpec((tm, d), lambda i, f: (i, 0)),
                  pl.BlockSpec((1, d), lambda i, f: (0, 0)),
                  pl.BlockSpec((d, tf), lambda i, f: (0, f)),
                  pl.BlockSpec((tf, d), lambda i, f: (f, 0)),
                  pl.BlockSpec((1, d), lambda i, f: (0, 0))],
        out_specs=pl.BlockSpec((tm, d), lambda i, f: (i, 0)),
        out_shape=jax.ShapeDtypeStruct((t, d), F32),
        scratch_shapes=[pltpu.VMEM((tm, d), BF16)],
        compiler_params=_cparams("parallel", "arbitrary"),
        name="mlp",
    )(x, g.reshape(1, d), wu, wd, g_final.reshape(1, d))


def _attn_body(x_ref, k_ref, v_ref, g_ref, wq_ref, wo_ref, o_ref):
    x = x_ref[0]
    h = _rms(x, g_ref[...]).astype(BF16)
    q = jnp.dot(h, wq_ref[...], preferred_element_type=F32)
    scale = XA_HEAD_DIM ** -0.5
    heads = []
    for hd in range(XA_HEADS):
        sl = slice(hd * XA_HEAD_DIM, (hd + 1) * XA_HEAD_DIM)
        s = lax.dot_general(q[:, sl].astype(BF16), k_ref[0, :, sl], (((1,), (1,)), ((), ())),
                            preferred_element_type=F32) * scale
        p = jnp.exp(s - jnp.max(s, axis=-1, keepdims=True))
        a = (p / jnp.sum(p, axis=-1, keepdims=True)).astype(BF16)
        heads.append(jnp.dot(a, v_ref[0, :, sl], preferred_element_type=F32).astype(BF16))
    o = jnp.concatenate(heads, axis=-1)
    o_ref[0] = x + jnp.dot(o, wo_ref[...], preferred_element_type=F32)


def _attn(x, kv, g, wq, wo, tq=512):
    b, l, d = x.shape
    return pl.pallas_call(
        _attn_body,
        grid=(b, l // tq),
        in_specs=[pl.BlockSpec((1, tq, d), lambda i, j: (i, j, 0)),
                  pl.BlockSpec((1, N_MEM, d), lambda i, j: (i, 0, 0)),
                  pl.BlockSpec((1, N_MEM, d), lambda i, j: (i, 0, 1)),
                  pl.BlockSpec((1, d), lambda i, j: (0, 0)),
                  pl.BlockSpec((d, d), lambda i, j: (0, 0)),
                  pl.BlockSpec((d, d), lambda i, j: (0, 0))],
        out_specs=pl.BlockSpec((1, tq, d), lambda i, j: (i, j, 0)),
        out_shape=jax.ShapeDtypeStruct((b, l, d), F32),
        compiler_params=_cparams("parallel", "arbitrary"),
        name="attn",
    )(x, kv, kv, g.reshape(1, d), wq, wo)


def _outproj_body(x_ref, yh_ref, yp_ref, gh_ref, wt_ref, wb_ref, o_ref):
    yh = _rms(yh_ref[...], gh_ref[...]).astype(BF16)
    acc = jnp.dot(yh, wt_ref[...], preferred_element_type=F32)
    acc += jnp.dot(yp_ref[...], wb_ref[...], preferred_element_type=F32)
    o_ref[...] = x_ref[...] + acc


def _outproj(x, yh, yp, g_hy, w_out, tm=512):
    t, d = x.shape
    return pl.pallas_call(
        _outproj_body,
        grid=(t // tm,),
        in_specs=[pl.BlockSpec((tm, d), lambda i: (i, 0)),
                  pl.BlockSpec((tm, HY_CH), lambda i: (i, 0)),
                  pl.BlockSpec((tm, POOL_CH), lambda i: (i, 0)),
                  pl.BlockSpec((1, HY_CH), lambda i: (0, 0)),
                  pl.BlockSpec((HY_CH, d), lambda i: (0, 0)),
                  pl.BlockSpec((POOL_CH, d), lambda i: (1, 0))],
        out_specs=pl.BlockSpec((tm, d), lambda i: (i, 0)),
        out_shape=jax.ShapeDtypeStruct((t, d), F32),
        compiler_params=_cparams("parallel"),
        name="outproj",
    )(x, yh, yp, g_hy.reshape(1, HY_CH), w_out, w_out)


def _shift_rows(x, down):
    r = x.shape[-2]
    rows = lax.broadcasted_iota(jnp.int32, x.shape, x.ndim - 2)
    if down:
        return jnp.where(rows == 0, 0.0, pltpu.roll(x, 1, axis=x.ndim - 2))
    return jnp.where(rows == r - 1, 0.0, pltpu.roll(x, r - 1, axis=x.ndim - 2))


def _wrap_prev(x, is_first):
    return jnp.where(is_first, _shift_rows(x, True), x)


def _wrap_next(x, is_last):
    return jnp.where(is_last, _shift_rows(x, False), x)


def _short_conv_planes(main_ref, prev_ref, next_ref, w_ref, b_ref, is_first, is_last):
    tj = main_ref.shape[1]
    planes = [_wrap_prev(prev_ref[:, 0], is_first)]
    planes += [main_ref[:, jj] for jj in range(tj)]
    planes += [_wrap_next(next_ref[:, 0], is_last)]
    w0, w1, w2, b = w_ref[0:1, :], w_ref[1:2, :], w_ref[2:3, :], b_ref[...]
    return [planes[jj] * w0 + planes[jj + 1] * w1 + planes[jj + 2] * w2 + b for jj in range(tj)]


def _pack(re, im):
    rb = lax.bitcast_convert_type(re.astype(BF16).astype(F32), U32)
    ib = lax.bitcast_convert_type(im.astype(BF16).astype(F32), U32)
    return rb | (ib >> 16)


def _unpack(p):
    return (lax.bitcast_convert_type(p & jnp.uint32(0xFFFF0000), F32),
            lax.bitcast_convert_type(p << 16, F32))


def _store_cols(scr, n2i, n1, pitch, val):
    for l in range(scr.shape[0]):
        scr[l, pl.ds(n2i, n1, stride=pitch), :] = val[:, l * LANES:(l + 1) * LANES]


def _load_cols(scr, n2i, n1, pitch):
    return jnp.concatenate([scr[l, pl.ds(n2i, n1, stride=pitch), :] for l in range(scr.shape[0])], axis=1)


def _store_rows(scr, base, n2, val):
    for l in range(scr.shape[0]):
        scr[l, pl.ds(base, n2), :] = val[:, l * LANES:(l + 1) * LANES]


def _load_rows(scr, base, n2):
    return jnp.concatenate([scr[l, pl.ds(base, n2), :] for l in range(scr.shape[0])], axis=1)


def _stage1_to_scratch(x_bf16, w_ref, tw_ref, jj, scr, n2i, n1, pitch):
    res = jnp.dot(w_ref[...], x_bf16, preferred_element_type=F32)
    ar, ai = res[:n1], res[n1:]
    twr, twi = tw_ref[0, 0, :, jj:jj + 1], tw_ref[0, 1, :, jj:jj + 1]
    _store_cols(scr, n2i, n1, pitch, _pack(ar * twr - ai * twi, ar * twi + ai * twr))


def _stage2_fwd(scr, base, n2, f_ref):
    xr, xi = _unpack(_load_rows(scr, base, n2))
    x = jnp.concatenate([xr, xi], axis=0).astype(BF16)
    s = jnp.dot(f_ref[...], x, preferred_element_type=F32)
    return s[:n2], s[n2:]


def _lconv_body(*refs, conv, n2, n1, pitch, tj, tk, s1, s2):
    if conv:
        (vm_ref, vp_ref, vn_ref, vcw_ref, vcb_ref, gm_ref, gp_ref, gn_ref, gcw_ref, gcb_ref,
         kf_ref, w_ref, wi_ref, f_ref, g_ref, tw_ref, o_ref, scr) = refs
    else:
        (vm_ref, gm_ref, gp_ref, gn_ref, gcw_ref, gcb_ref,
         kf_ref, w_ref, wi_ref, f_ref, g_ref, tw_ref, o_ref, scr) = refs
    s = pl.program_id(2)

    @pl.when(s < s1)
    def _():
        if conv:
            planes = _short_conv_planes(vm_ref, vp_ref, vn_ref, vcw_ref, vcb_ref, s == 0, s == s1 - 1)
        else:
            planes = [vm_ref[:, jj] for jj in range(tj)]
        for jj, v in enumerate(planes):
            nb, r, ct = v.shape
            _stage1_to_scratch(v.reshape(nb * r, ct).astype(BF16), w_ref, tw_ref, jj, scr,
                               s * tj + jj, n1, pitch)

    @pl.when(jnp.logical_and(s >= s1, s < s1 + s2))
    def _():
        for kk in range(tk):
            base = pl.multiple_of(((s - s1) * tk + kk) * pitch, SUBLANES)
            sr, si = _stage2_fwd(scr, base, n2, f_ref)
            kr, ki = kf_ref[0, 0, kk], kf_ref[0, 1, kk]
            y = jnp.concatenate([sr * kr - si * ki, sr * ki + si * kr], axis=0).astype(BF16)
            a = jnp.dot(g_ref[...], y, preferred_element_type=F32)
            _store_rows(scr, base, n2, _pack(a[:n2], a[n2:]))

    @pl.when(s >= s1 + s2)
    def _():
        jp = s - s1 - s2
        gates = _short_conv_planes(gm_ref, gp_ref, gn_ref, gcw_ref, gcb_ref, jp == 0, jp == s1 - 1)
        for jj in range(tj):
            ar, ai = _unpack(_load_cols(scr, jp * tj + jj, n1, pitch))
            twr, twi = tw_ref[0, 0, :, jj:jj + 1], tw_ref[0, 1, :, jj:jj + 1]
            x = jnp.concatenate([ar * twr + ai * twi, ai * twr - ar * twi], axis=0).astype(BF16)
            y = jnp.dot(wi_ref[...], x, preferred_element_type=F32)
            o_ref[:, jj] = gates[jj] * y.reshape(gates[jj].shape)


def _lconv(vsrc, v_goff, v_cw, v_cb, u, g_goff, g_cw, g_cb, kf, order, tabs, cfg, name):
    n2, r, n1, tj, tk, ct, pitch = (cfg[k] for k in ("N2", "R", "N1", "tj", "tk", "ct", "pitch"))
    p = vsrc.shape[0] // 2
    s1, s2 = n2 // tj, n1 // tk
    conv = v_cw is not None

    def jv(s):
        return jnp.minimum(s, s1 - 1)

    def jg(s):
        return jnp.maximum(s - s1 - s2, 0)

    def jt(s):
        return jnp.where(s < s1, s, jg(s))

    def plane_specs(goff, jf):
        return [pl.BlockSpec((2, tj, r, ct), lambda p_, c, s: (p_, jf(s), 0, goff + c)),
                pl.BlockSpec((2, 1, r, ct), lambda p_, c, s: (p_, (jf(s) * tj + n2 - 1) % n2, 0, goff + c)),
                pl.BlockSpec((2, 1, r, ct), lambda p_, c, s: (p_, ((jf(s) + 1) * tj) % n2, 0, goff + c))]

    wspecs = [pl.BlockSpec((3, ct), lambda p_, c, s: (0, c)), pl.BlockSpec((1, ct), lambda p_, c, s: (0, c))]
    if conv:
        in_specs = plane_specs(v_goff, jv) + wspecs
        args = [vsrc, vsrc, vsrc, v_cw, v_cb]
    else:
        in_specs = plane_specs(v_goff, jv)[:1]
        args = [vsrc]
    in_specs += plane_specs(g_goff, jg) + wspecs
    args += [u, u, u, g_cw, g_cb]
    in_specs += [pl.BlockSpec((1, 2, tk, n2, ct), lambda p_, c, s: (order, 0, jnp.clip(s - s1, 0, s2 - 1), 0, c)),
                 _const_spec((2 * n1, 2 * r)), _const_spec((2 * r, 2 * n1)),
                 _const_spec((2 * n2, 2 * n2)), _const_spec((2 * n2, 2 * n2)),
                 pl.BlockSpec((1, 2, n1, tj), lambda p_, c, s: (jt(s), 0, 0, 0))]
    args += [kf, tabs["w1"], tabs["w1inv"], tabs["f2"], tabs["g2"], tabs["tw"]]
    return pl.pallas_call(
        functools.partial(_lconv_body, conv=conv, n2=n2, n1=n1, pitch=pitch, tj=tj, tk=tk, s1=s1, s2=s2),
        grid=(p, HY_CH // ct, 2 * s1 + s2),
        in_specs=in_specs,
        out_specs=pl.BlockSpec((2, tj, r, ct), lambda p_, c, s: (p_, jg(s), 0, c)),
        out_shape=jax.ShapeDtypeStruct((2 * p, n2, r, HY_CH), F32),
        scratch_shapes=[pltpu.VMEM((ct // LANES, n1 * pitch, LANES), U32)],
        compiler_params=_cparams("parallel", "arbitrary", "arbitrary"),
        name=name,
    )(*args)


def _filt_pos_half(shape, n2, n2_total, d):
    row = lax.broadcasted_iota(jnp.int32, shape, 0)
    if d == 0:
        return (row * n2_total + n2).astype(F32)
    pos_b = (shape[0] - row) * n2_total - n2
    return jnp.where(jnp.logical_and(row == 0, n2 == 0), 0, pos_b).astype(F32)


def _filt_pos(shape, n2, n2_total):
    half = (shape[0] // 2, shape[1])
    return jnp.concatenate([_filt_pos_half(half, n2, n2_total, d) for d in range(2)], axis=0)


def _filt_hidden_body(fv_ref, w1_ref, b1_ref, w2_ref, b2_ref, fr_ref, h_ref, *, seq_len, n2_total):
    tj, n1 = h_ref.shape[0], h_ref.shape[1]
    lane = lax.broadcasted_iota(jnp.int32, (n1, LANES), 1)
    for jj in range(tj):
        pos = _filt_pos((n1, LANES), pl.program_id(0) * tj + jj, n2_total)
        t = pos / float(max(seq_len - 1, 1))
        ang = ((2.0 * math.pi / seq_len) * pos) * fv_ref[...]
        z = jnp.where(lane == 0, t, jnp.where(lane <= POS_BANDS, jnp.cos(ang),
                                              jnp.where(lane <= 2 * POS_BANDS, -jnp.sin(ang), 0.0)))
        h = jnp.sin(fr_ref[...] * (jnp.dot(z, w1_ref[...], precision=HI, preferred_element_type=F32) + b1_ref[...]))
        h = jnp.sin(fr_ref[...] * (jnp.dot(h, w2_ref[...], precision=HI, preferred_element_type=F32) + b2_ref[...]))
        h_ref[jj] = h


def _filt_hidden(fvec, w1p, b1, w2, b2, freq, cfg, seq_len):
    n2, n1, tj = cfg["N2"], cfg["N1"], cfg["tj"]
    return pl.pallas_call(
        functools.partial(_filt_hidden_body, seq_len=seq_len, n2_total=n2),
        grid=(n2 // tj,),
        in_specs=[_const_spec((1, LANES)), _const_spec((LANES, FILT_HIDDEN)), _const_spec((1, FILT_HIDDEN)),
                  _const_spec((FILT_HIDDEN, FILT_HIDDEN)), _const_spec((1, FILT_HIDDEN)),
                  _const_spec((1, FILT_HIDDEN))],
        out_specs=pl.BlockSpec((tj, n1, FILT_HIDDEN), lambda j: (j, 0, 0)),
        out_shape=jax.ShapeDtypeStruct((n2, n1, FILT_HIDDEN), F32),
        compiler_params=_cparams("parallel"),
        name="filt_hidden",
    )(fvec, w1p, b1, w2, b2, freq)


def _filt_spec_body(h_ref, w3_ref, dec_ref, bias_ref, wf_ref, f_ref, tw_ref, o_ref, scr, nrm_ref,
                    *, seq_len, n2, n1, pitch, tj, tk, s1):
    s = pl.program_id(2)
    r = n1 // 2
    ct = o_ref.shape[-1]

    @pl.when(s == 0)
    def _():
        nrm_ref[...] = jnp.zeros_like(nrm_ref)

    @pl.when(s < s1)
    def _():
        row = lax.broadcasted_iota(jnp.int32, (r, ct), 0)
        for jj in range(tj):
            n2i = s * tj + jj
            halves = []
            for d in range(2):
                val = jnp.dot(h_ref[jj, d * r:(d + 1) * r, :], w3_ref[0, d], precision=HI,
                              preferred_element_type=F32)
                t = _filt_pos_half((r, ct), n2i, n2, d) / float(max(seq_len - 1, 1))
                val = val * jnp.exp(-t * jnp.abs(dec_ref[0, d:d + 1, :]))
                nrm_ref[...] += jnp.sum(jnp.abs(val), axis=0, keepdims=True)
                if d == 1:
                    val = jnp.where(jnp.logical_and(row == 0, n2i == 0), 0.0, val)
                halves.append(val)
            _stage1_to_scratch(jnp.concatenate(halves, axis=0).astype(BF16), wf_ref, tw_ref, jj, scr,
                               n2i, n1, pitch)

    @pl.when(s >= s1)
    def _():
        inv = 1.0 / nrm_ref[...]
        bias = bias_ref[0]
        for kk in range(tk):
            base = pl.multiple_of(((s - s1) * tk + kk) * pitch, SUBLANES)
            sr, si = _stage2_fwd(scr, base, n2, f_ref)
            o_ref[0, 0, kk] = sr * inv + bias
            o_ref[0, 1, kk] = si * inv


def _filt_spec(hid, w3, dec, bias, tabs, cfg, seq_len):
    n2, n1, tj, tk, ct, pitch = (cfg[k] for k in ("N2", "N1", "tj", "tk", "ct", "pitch"))
    s1, s2 = n2 // tj, n1 // tk
    return pl.pallas_call(
        functools.partial(_filt_spec_body, seq_len=seq_len, n2=n2, n1=n1, pitch=pitch, tj=tj, tk=tk, s1=s1),
        grid=(2, HY_CH // ct, s1 + s2),
        in_specs=[pl.BlockSpec((tj, n1, FILT_HIDDEN), lambda o, c, s: (jnp.minimum(s, s1 - 1), 0, 0)),
                  pl.BlockSpec((1, 2, FILT_HIDDEN, ct), lambda o, c, s: (o, 0, 0, c)),
                  pl.BlockSpec((1, 2, ct), lambda o, c, s: (o, 0, c)),
                  pl.BlockSpec((1, 1, ct), lambda o, c, s: (o, 0, c)),
                  _const_spec((2 * n1, n1)), _const_spec((2 * n2, 2 * n2)),
                  pl.BlockSpec((1, 2, n1, tj), lambda o, c, s: (jnp.minimum(s, s1 - 1), 0, 0, 0))],
        out_specs=pl.BlockSpec((1, 2, tk, n2, ct), lambda o, c, s: (o, 0, jnp.maximum(s - s1, 0), 0, c)),
        out_shape=jax.ShapeDtypeStruct((2, 2, n1, n2, HY_CH), F32),
        scratch_shapes=[pltpu.VMEM((ct // LANES, n1 * pitch, LANES), U32), pltpu.VMEM((1, ct), F32)],
        compiler_params=_cparams("parallel", "arbitrary", "arbitrary"),
        name="filt_spec",
    )(hid, w3, dec, bias, tabs["wf"], tabs["f2"], tabs["tw"])


def _pool_body(m_ref, p_ref, n_ref, pw_ref, ps_ref, gp_ref, o_ref, *, seq_len, n2_total):
    tj, r = m_ref.shape[1], m_ref.shape[2]
    j = pl.program_id(1)
    first, last = j == 0, j == pl.num_programs(1) - 1
    h = POOL_HALO
    j0 = j * tj
    plane = lax.broadcasted_iota(jnp.int32, (tj, r, POOL_GROUP), 0)
    row = lax.broadcasted_iota(jnp.int32, (tj, r, POOL_GROUP), 1)
    t = row * n2_total + j0 + plane
    outs = []
    for g, win in enumerate(POOL_WINDOWS):
        sl = slice(g * POOL_GROUP, (g + 1) * POOL_GROUP)
        ext = jnp.concatenate([_wrap_prev(p_ref[0, :, :, sl], first), m_ref[0, :, :, sl],
                               _wrap_next(n_ref[0, :, :, sl], last)], axis=0)
        acc = ext[h - win // 2:h - win // 2 + tj]
        for o in range(1 - win // 2, win // 2):
            acc = acc + ext[h + o:h + o + tj]
        lo = jnp.maximum(t - win // 2, 0)
        hi = jnp.minimum(t + win // 2 - 1, seq_len - 1)
        cnt = (hi - lo + 1).astype(F32)
        dgrp = acc / cnt - ext[h:h + tj]
        outs.append(jnp.dot(dgrp.reshape(tj * r, POOL_GROUP).astype(BF16), pw_ref[g],
                            preferred_element_type=F32))
    y = jnp.concatenate(outs, axis=-1) * ps_ref[...]
    o_ref[0] = _rms(y, gp_ref[...]).astype(BF16).reshape(tj, r, POOL_CH)


def _pool(u, pool_w, pool_scale, g_pool, cfg, seq_len):
    n2, r, tj = cfg["N2"], cfg["R"], cfg["tj_pool"]
    b = u.shape[0]
    goff = 3 * HY_CH // POOL_CH
    h = POOL_HALO
    nh = n2 // h
    return pl.pallas_call(
        functools.partial(_pool_body, seq_len=seq_len, n2_total=n2),
        grid=(b, n2 // tj),
        in_specs=[pl.BlockSpec((1, tj, r, POOL_CH), lambda i, j: (i, j, 0, goff)),
                  pl.BlockSpec((1, h, r, POOL_CH), lambda i, j: (i, (j * (tj // h) + nh - 1) % nh, 0, goff)),
                  pl.BlockSpec((1, h, r, POOL_CH), lambda i, j: (i, ((j + 1) * (tj // h)) % nh, 0, goff)),
                  _const_spec((len(POOL_WINDOWS), POOL_GROUP, POOL_GROUP)),
                  _const_spec((1, POOL_CH)), _const_spec((1, POOL_CH))],
        out_specs=pl.BlockSpec((1, tj, r, POOL_CH), lambda i, j: (i, j, 0, 0)),
        out_shape=jax.ShapeDtypeStruct((b, n2, r, POOL_CH), BF16),
        compiler_params=_cparams("parallel", "arbitrary"),
        name="pool",
    )(u, u, u, pool_w, pool_scale, g_pool)


def _fft_cfg(seq_len):
    n2 = 128 if seq_len >= 8192 else 64
    r = seq_len // n2
    assert r % SUBLANES == 0 and r * n2 == seq_len
    return dict(N2=n2, R=r, N1=2 * r, tj=8, tk=8, ct=256, pitch=n2 + SUBLANES, tj_pool=8)


def _dft_tables(cfg):
    n2, r, n1, tj = cfg["N2"], cfg["R"], cfg["N1"], cfg["tj"]
    n = n1 * n2

    def cis(idx, period):
        a = -2.0 * np.pi * (idx % period) / period
        return np.cos(a), np.sin(a)

    k1 = np.arange(n1)
    mr, mi = cis(k1[:, None] * k1[None, :], n1)
    mrd, mid = mr[:, :r], mi[:, :r]
    w1 = np.block([[mrd, -mid], [mid, mrd]])
    wf = np.concatenate([mr, mi], axis=0)
    k2 = np.arange(n2)
    fr, fi = cis(k2[:, None] * k2[None, :], n2)
    f2 = np.block([[fr, -fi], [fi, fr]])
    g2 = np.block([[fr, fi], [-fi, fr]])
    twr, twi = cis(k1[:, None] * k2[None, :], n)
    tw = np.stack([twr, twi]).reshape(2, n1, n2 // tj, tj).transpose(2, 0, 1, 3)
    return dict(w1=jnp.asarray(w1, BF16), w1inv=jnp.asarray(w1.T / n, BF16), wf=jnp.asarray(wf, BF16),
                f2=jnp.asarray(f2, BF16), g2=jnp.asarray(g2, BF16), tw=jnp.asarray(tw, F32))


def _to_plane_order(x, cfg):
    b, l, d = x.shape
    return jnp.swapaxes(x.reshape(b, cfg["R"], cfg["N2"], d), 1, 2).reshape(b * l, d)


def _from_plane_order(x, b, cfg):
    d = x.shape[-1]
    return jnp.swapaxes(x.reshape(b, cfg["N2"], cfg["R"], d), 1, 2).reshape(b, cfg["N2"] * cfg["R"], d)


def _trunk(x, mem, lw, g_final, depth):
    b, seq_len, d = x.shape
    cfg = _fft_cfg(seq_len)
    tabs = _dft_tables(cfg)
    n2, r, ct = cfg["N2"], cfg["R"], cfg["ct"]
    xs = _to_plane_order(x, cfg)
    memf = mem.reshape(b * N_MEM, d)
    fvec = np.zeros((1, LANES), np.float32)
    bands = np.linspace(1e-4, POS_BANDS - 1, POS_BANDS, dtype=np.float32)
    fvec[0, 1:1 + POS_BANDS] = bands
    fvec[0, 1 + POS_BANDS:1 + 2 * POS_BANDS] = bands
    fvec = jnp.asarray(fvec)
    cpb = HY_CH // ct
    for i in range(depth):
        w = {k: v[i] for k, v in lw.items()}
        w1p = jnp.zeros((LANES, FILT_HIDDEN), F32).at[:2 * POS_BANDS + 1].set(w["filt_w1"])
        hid = _filt_hidden(fvec, w1p, w["filt_b1"].reshape(1, -1), w["filt_w2"], w["filt_b2"].reshape(1, -1),
                           w["filt_freq"].reshape(1, -1), cfg, seq_len)
        kf = _filt_spec(hid, jnp.transpose(w["filt_w3"].reshape(FILT_HIDDEN, 2, 2, HY_CH), (1, 2, 0, 3)),
                        w["hy_decay"], w["hy_bias"].reshape(2, 1, HY_CH), tabs, cfg, seq_len)
        u = _norm_matmul(xs, w["g_mix"], w["w_in"], F32, 1024, 1024, "in_proj").reshape(b, n2, r, D_IN)
        cw, cbias = w["conv_w"], w["conv_b"].reshape(1, -1)
        cws = [cw[:, k * HY_CH:(k + 1) * HY_CH] for k in range(3)]
        cbs = [cbias[:, k * HY_CH:(k + 1) * HY_CH] for k in range(3)]
        z1 = _lconv(u, 2 * cpb, cws[2], cbs[2], u, 0, cws[0], cbs[0], kf, 0, tabs, cfg, "lconv1")
        yh = _lconv(z1, 0, None, None, u, cpb, cws[1], cbs[1], kf, 1, tabs, cfg, "lconv2")
        yp = _pool(u, w["pool_w"], w["pool_scale"].reshape(1, -1), w["g_pool"].reshape(1, -1), cfg, seq_len)
        xs = _outproj(xs, yh.reshape(b * seq_len, HY_CH), yp.reshape(b * seq_len, POOL_CH), w["g_hy"], w["w_out"])
        kv = _norm_matmul(memf, w["g_mem"], w["w_kv"], BF16, 512, 1024, "kv_proj").reshape(b, N_MEM, 2 * d)
        xs = _attn(xs.reshape(b, seq_len, d), kv, w["g_xa"], w["w_q"], w["w_o"]).reshape(b * seq_len, d)
        xs = _mlp(xs, w["g_mlp"], w["w_up"], w["w_down"], g_final, i == depth - 1)
    return _from_plane_order(xs, b, cfg)


_MATMUL_WEIGHTS = ("w_in", "pool_w", "w_out", "w_q", "w_kv", "w_o", "w_up", "w_down")


def kernel(x_prompt, x_sample, mem_prompt, mem_sample, g_mix, w_in, conv_w, conv_b, filt_w1, filt_b1, filt_w2, filt_b2, filt_freq, filt_w3, hy_decay, hy_bias, pool_w, pool_scale, g_hy, g_pool, w_out, g_xa, g_mem, w_q, w_kv, w_o, g_mlp, w_up, w_down, g_final):
    lw = dict(g_mix=g_mix, w_in=w_in, conv_w=conv_w, conv_b=conv_b, filt_w1=filt_w1, filt_b1=filt_b1,
              filt_w2=filt_w2, filt_b2=filt_b2, filt_freq=filt_freq, filt_w3=filt_w3, hy_decay=hy_decay,
              hy_bias=hy_bias, pool_w=pool_w, pool_scale=pool_scale, g_hy=g_hy, g_pool=g_pool, w_out=w_out,
              g_xa=g_xa, g_mem=g_mem, w_q=w_q, w_kv=w_kv, w_o=w_o, g_mlp=g_mlp, w_up=w_up, w_down=w_down)
    lw = {k: (v.astype(BF16) if k in _MATMUL_WEIGHTS else v) for k, v in lw.items()}
    depth = g_mix.shape[0]
    y_prompt = _trunk(x_prompt, mem_prompt, lw, g_final, depth)
    y_sample = _trunk(x_sample, mem_sample, lw, g_final, depth)
    return (y_prompt, y_sample)
```

```python
import functools
import math

import numpy as np
import jax
import jax.numpy as jnp
from jax import lax
from jax.experimental import pallas as pl
from jax.experimental.pallas import tpu as pltpu

F32 = jnp.float32
BF16 = jnp.bfloat16
U32 = jnp.uint32

D_MODEL = 2048
HY_CH = 1024
POOL_CH = 1024
D_IN = 3 * HY_CH + POOL_CH
POOL_WINDOWS = (2, 4, 8, 16)
POOL_GROUP = POOL_CH // len(POOL_WINDOWS)
POOL_HALO = max(POOL_WINDOWS) // 2
POS_BANDS = 16
FILT_HIDDEN = 64
N_MEM = 256
XA_HEADS = 4
XA_HEAD_DIM = D_MODEL // XA_HEADS
D_FF = 4 * D_MODEL
EPS = 1e-6

LANES = 128
SUBLANES = 8
VMEM_LIMIT_BYTES = 58 * 1024 * 1024
HI = lax.Precision.HIGHEST


def _cparams(*sem):
    return pltpu.CompilerParams(dimension_semantics=sem, vmem_limit_bytes=VMEM_LIMIT_BYTES)


def _rms(x, g):
    return x * lax.rsqrt(jnp.mean(x * x, axis=-1, keepdims=True) + EPS) * g


def _const_spec(shape):
    nd = len(shape)
    return pl.BlockSpec(shape, lambda *_: (0,) * nd)


def _norm_matmul_body(x_ref, g_ref, w_ref, o_ref, h_ref):
    @pl.when(pl.program_id(1) == 0)
    def _():
        h_ref[...] = _rms(x_ref[...], g_ref[...]).astype(BF16)

    o_ref[...] = jnp.dot(h_ref[...], w_ref[...], preferred_element_type=F32).astype(o_ref.dtype)


def _norm_matmul(x, g, w, layer, out_dtype, tm, tn, name):
    t, d = x.shape
    n = w.shape[2]
    return pl.pallas_call(
        _norm_matmul_body,
        grid=(t // tm, n // tn),
        in_specs=[pl.BlockSpec((tm, d), lambda i, j: (i, 0)),
                  pl.BlockSpec((1, d), lambda i, j: (0, 0)),
                  pl.BlockSpec((None, d, tn), lambda i, j: (layer, 0, j))],
        out_specs=pl.BlockSpec((tm, tn), lambda i, j: (i, j)),
        out_shape=jax.ShapeDtypeStruct((t, n), out_dtype),
        scratch_shapes=[pltpu.VMEM((tm, d), BF16)],
        compiler_params=_cparams("parallel", "arbitrary"),
        name=name,
    )(x, g.reshape(1, d), w)


def _mlp_body(x_ref, g_ref, wu_ref, wd_ref, gf_ref, o_ref, h_ref, *, final_norm):
    f = pl.program_id(1)

    @pl.when(f == 0)
    def _():
        x = x_ref[...]
        h_ref[...] = _rms(x, g_ref[...]).astype(BF16)
        o_ref[...] = x

    hm = jnp.dot(h_ref[...], wu_ref[...], preferred_element_type=F32)
    a = jnp.square(jnp.maximum(hm, 0.0)).astype(BF16)
    o_ref[...] += jnp.dot(a, wd_ref[...], preferred_element_type=F32)

    if final_norm:
        @pl.when(f == pl.num_programs(1) - 1)
        def _():
            o_ref[...] = _rms(o_ref[...], gf_ref[...])


def _mlp(x, g, wu, wd, layer, g_final, final_norm, tm=512, tf=1024):
    t, d = x.shape
    ff = wu.shape[2]
    return pl.pallas_call(
        functools.partial(_mlp_body, final_norm=final_norm),
        grid=(t // tm, ff // tf),
        in_specs=[pl.BlockSpec((tm, d), lambda i, f: (i, 0)),
                  pl.BlockSpec((1, d), lambda i, f: (0, 0)),
                  pl.BlockSpec((None, d, tf), lambda i, f: (layer, 0, f)),
                  pl.BlockSpec((None, tf, d), lambda i, f: (layer, f, 0)),
                  pl.BlockSpec((1, d), lambda i, f: (0, 0))],
        out_specs=pl.BlockSpec((tm, d), lambda i, f: (i, 0)),
        out_shape=jax.ShapeDtypeStruct((t, d), F32),
        scratch_shapes=[pltpu.VMEM((tm, d), BF16)],
        compiler_params=_cparams("parallel", "arbitrary"),
        name="mlp",
    )(x, g.reshape(1, d), wu, wd, g_final.reshape(1, d))


def _attn_body(x_ref, k_ref, v_ref, g_ref, wq_ref, wo_ref, o_ref):
    x = x_ref[0]
    h = _rms(x, g_ref[...]).astype(BF16)
    q = jnp.dot(h, wq_ref[...], preferred_element_type=F32)
    scale = XA_HEAD_DIM ** -0.5
    heads = []
    for hd in range(XA_HEADS):
        sl = slice(hd * XA_HEAD_DIM, (hd + 1) * XA_HEAD_DIM)
        s = lax.dot_general(q[:, sl].astype(BF16), k_ref[0, :, sl], (((1,), (1,)), ((), ())),
                            preferred_element_type=F32) * scale
        p = jnp.exp(s - jnp.max(s, axis=-1, keepdims=True))
        a = (p / jnp.sum(p, axis=-1, keepdims=True)).astype(BF16)
        heads.append(jnp.dot(a, v_ref[0, :, sl], preferred_element_type=F32).astype(BF16))
    o = jnp.concatenate(heads, axis=-1)
    o_ref[0] = x + jnp.dot(o, wo_ref[...], preferred_element_type=F32)


def _attn(x, kv, g, wq, wo, layer, tq=512):
    b, l, d = x.shape
    return pl.pallas_call(
        _attn_body,
        grid=(b, l // tq),
        in_specs=[pl.BlockSpec((1, tq, d), lambda i, j: (i, j, 0)),
                  pl.BlockSpec((1, N_MEM, d), lambda i, j: (i, 0, 0)),
                  pl.BlockSpec((1, N_MEM, d), lambda i, j: (i, 0, 1)),
                  pl.BlockSpec((1, d), lambda i, j: (0, 0)),
                  pl.BlockSpec((None, d, d), lambda i, j: (layer, 0, 0)),
                  pl.BlockSpec((None, d, d), lambda i, j: (layer, 0, 0))],
        out_specs=pl.BlockSpec((1, tq, d), lambda i, j: (i, j, 0)),
        out_shape=jax.ShapeDtypeStruct((b, l, d), F32),
        compiler_params=_cparams("parallel", "arbitrary"),
        name="attn",
    )(x, kv, kv, g.reshape(1, d), wq, wo)


def _outproj_body(x_ref, yh_ref, yp_ref, gh_ref, wt_ref, wb_ref, o_ref):
    yh = _rms(yh_ref[...], gh_ref[...]).astype(BF16)
    acc = jnp.dot(yh, wt_ref[...], preferred_element_type=F32)
    acc += jnp.dot(yp_ref[...], wb_ref[...], preferred_element_type=F32)
    o_ref[...] = x_ref[...] + acc


def _outproj(x, yh, yp, g_hy, w_out, layer, tm=512):
    t, d = x.shape
    return pl.pallas_call(
        _outproj_body,
        grid=(t // tm,),
        in_specs=[pl.BlockSpec((tm, d), lambda i: (i, 0)),
                  pl.BlockSpec((tm, HY_CH), lambda i: (i, 0)),
                  pl.BlockSpec((tm, POOL_CH), lambda i: (i, 0)),
                  pl.BlockSpec((1, HY_CH), lambda i: (0, 0)),
                  pl.BlockSpec((None, HY_CH, d), lambda i: (layer, 0, 0)),
                  pl.BlockSpec((None, POOL_CH, d), lambda i: (layer, 1, 0))],
        out_specs=pl.BlockSpec((tm, d), lambda i: (i, 0)),
        out_shape=jax.ShapeDtypeStruct((t, d), F32),
        compiler_params=_cparams("parallel"),
        name="outproj",
    )(x, yh, yp, g_hy.reshape(1, HY_CH), w_out, w_out)


def _shift_rows(x, down):
    r = x.shape[-2]
    rows = lax.broadcasted_iota(jnp.int32, x.shape, x.ndim - 2)
    if down:
        return jnp.where(rows == 0, 0.0, pltpu.roll(x, 1, axis=x.ndim - 2))
    return jnp.where(rows == r - 1, 0.0, pltpu.roll(x, r - 1, axis=x.ndim - 2))


def _wrap_prev(x, is_first):
    return jnp.where(is_first, _shift_rows(x, True), x)


def _wrap_next(x, is_last):
    return jnp.where(is_last, _shift_rows(x, False), x)


def _short_conv_plane(jj, main_ref, prev_ref, next_ref, w_ref, b_ref, is_first, is_last):
    tj = main_ref.shape[1]
    lo = main_ref[:, jj - 1] if jj > 0 else _wrap_prev(prev_ref[:, 0], is_first)
    hi = main_ref[:, jj + 1] if jj < tj - 1 else _wrap_next(next_ref[:, 0], is_last)
    return lo * w_ref[0:1, :] + main_ref[:, jj] * w_ref[1:2, :] + hi * w_ref[2:3, :] + b_ref[...]


def _pack(re, im):
    rb = lax.bitcast_convert_type(re.astype(BF16).astype(F32), U32)
    ib = lax.bitcast_convert_type(im.astype(BF16).astype(F32), U32)
    return rb | (ib >> 16)


def _unpack(p):
    return (lax.bitcast_convert_type(p & jnp.uint32(0xFFFF0000), F32),
            lax.bitcast_convert_type(p << 16, F32))


def _store_cols(scr, n2i, n1, pitch, val):
    for l in range(scr.shape[0]):
        scr[l, pl.ds(n2i, n1, stride=pitch), :] = val[:, l * LANES:(l + 1) * LANES]


def _load_cols(scr, n2i, n1, pitch):
    return jnp.concatenate([scr[l, pl.ds(n2i, n1, stride=pitch), :] for l in range(scr.shape[0])], axis=1)


def _store_rows(scr, base, n2, val):
    for l in range(scr.shape[0]):
        scr[l, pl.ds(base, n2), :] = val[:, l * LANES:(l + 1) * LANES]


def _load_rows(scr, base, n2):
    return jnp.concatenate([scr[l, pl.ds(base, n2), :] for l in range(scr.shape[0])], axis=1)


def _stage1_to_scratch(x_bf16, w_ref, tw_ref, jj, scr, n2i, n1, pitch):
    res = jnp.dot(w_ref[...], x_bf16, preferred_element_type=F32)
    ar, ai = res[:n1], res[n1:]
    twr, twi = tw_ref[0, 0, :, jj:jj + 1], tw_ref[0, 1, :, jj:jj + 1]
    _store_cols(scr, n2i, n1, pitch, _pack(ar * twr - ai * twi, ar * twi + ai * twr))


def _stage2_fwd(scr, base, n2, f_ref):
    xr, xi = _unpack(_load_rows(scr, base, n2))
    x = jnp.concatenate([xr, xi], axis=0).astype(BF16)
    s = jnp.dot(f_ref[...], x, preferred_element_type=F32)
    return s[:n2], s[n2:]


def _lconv_body(*refs, conv, n2, n1, pitch, tj, tk, s1, s2):
    if conv:
        (vm_ref, vp_ref, vn_ref, vcw_ref, vcb_ref, gm_ref, gp_ref, gn_ref, gcw_ref, gcb_ref,
         kf_ref, w_ref, wi_ref, f_ref, g_ref, tw_ref, o_ref, scr) = refs
    else:
        (vm_ref, gm_ref, gp_ref, gn_ref, gcw_ref, gcb_ref,
         kf_ref, w_ref, wi_ref, f_ref, g_ref, tw_ref, o_ref, scr) = refs
    s = pl.program_id(2)

    @pl.when(s < s1)
    def _():
        for jj in range(tj):
            if conv:
                v = _short_conv_plane(jj, vm_ref, vp_ref, vn_ref, vcw_ref, vcb_ref, s == 0, s == s1 - 1)
            else:
                v = vm_ref[:, jj]
            nb, r, ct = v.shape
            _stage1_to_scratch(v.reshape(nb * r, ct).astype(BF16), w_ref, tw_ref, jj, scr,
                               s * tj + jj, n1, pitch)

    @pl.when(jnp.logical_and(s >= s1, s < s1 + s2))
    def _():
        for kk in range(tk):
            base = pl.multiple_of(((s - s1) * tk + kk) * pitch, SUBLANES)
            sr, si = _stage2_fwd(scr, base, n2, f_ref)
            kr, ki = _unpack(kf_ref[0, kk])
            y = jnp.concatenate([sr * kr - si * ki, sr * ki + si * kr], axis=0).astype(BF16)
            a = jnp.dot(g_ref[...], y, preferred_element_type=F32)
            _store_rows(scr, base, n2, _pack(a[:n2], a[n2:]))

    @pl.when(s >= s1 + s2)
    def _():
        jp = s - s1 - s2
        for jj in range(tj):
            gate = _short_conv_plane(jj, gm_ref, gp_ref, gn_ref, gcw_ref, gcb_ref, jp == 0, jp == s1 - 1)
            ar, ai = _unpack(_load_cols(scr, jp * tj + jj, n1, pitch))
            twr, twi = tw_ref[0, 0, :, jj:jj + 1], tw_ref[0, 1, :, jj:jj + 1]
            x = jnp.concatenate([ar * twr + ai * twi, ai * twr - ar * twi], axis=0).astype(BF16)
            y = jnp.dot(wi_ref[...], x, preferred_element_type=F32)
            o_ref[:, jj] = gate * y.reshape(gate.shape)


def _lconv(vsrc, v_goff, v_cw, v_cb, u, g_goff, g_cw, g_cb, kf, order, tabs, cfg, name):
    n2, r, n1, tj, tk, ct, pitch = (cfg[k] for k in ("N2", "R", "N1", "tj", "tk", "ct", "pitch"))
    p = vsrc.shape[0] // 2
    s1, s2 = n2 // tj, n1 // tk
    conv = v_cw is not None

    def jv(s):
        return jnp.minimum(s, s1 - 1)

    def jg(s):
        return jnp.maximum(s - s1 - s2, 0)

    def jt(s):
        return jnp.where(s < s1, s, jg(s))

    def plane_specs(goff, jf):
        return [pl.BlockSpec((2, tj, r, ct), lambda p_, c, s: (p_, jf(s), 0, goff + c)),
                pl.BlockSpec((2, 1, r, ct), lambda p_, c, s: (p_, (jf(s) * tj + n2 - 1) % n2, 0, goff + c)),
                pl.BlockSpec((2, 1, r, ct), lambda p_, c, s: (p_, ((jf(s) + 1) * tj) % n2, 0, goff + c))]

    wspecs = [pl.BlockSpec((3, ct), lambda p_, c, s: (0, c)), pl.BlockSpec((1, ct), lambda p_, c, s: (0, c))]
    if conv:
        in_specs = plane_specs(v_goff, jv) + wspecs
        args = [vsrc, vsrc, vsrc, v_cw, v_cb]
    else:
        in_specs = plane_specs(v_goff, jv)[:1]
        args = [vsrc]
    in_specs += plane_specs(g_goff, jg) + wspecs
    args += [u, u, u, g_cw, g_cb]
    in_specs += [pl.BlockSpec((1, tk, n2, ct), lambda p_, c, s: (order, jnp.clip(s - s1, 0, s2 - 1), 0, c)),
                 _const_spec((2 * n1, 2 * r)), _const_spec((2 * r, 2 * n1)),
                 _const_spec((2 * n2, 2 * n2)), _const_spec((2 * n2, 2 * n2)),
                 pl.BlockSpec((1, 2, n1, tj), lambda p_, c, s: (jt(s), 0, 0, 0))]
    args += [kf, tabs["w1"], tabs["w1inv"], tabs["f2"], tabs["g2"], tabs["tw"]]
    return pl.pallas_call(
        functools.partial(_lconv_body, conv=conv, n2=n2, n1=n1, pitch=pitch, tj=tj, tk=tk, s1=s1, s2=s2),
        grid=(p, HY_CH // ct, 2 * s1 + s2),
        in_specs=in_specs,
        out_specs=pl.BlockSpec((2, tj, r, ct), lambda p_, c, s: (p_, jg(s), 0, c)),
        out_shape=jax.ShapeDtypeStruct((2 * p, n2, r, HY_CH), F32),
        scratch_shapes=[pltpu.VMEM((ct // LANES, n1 * pitch, LANES), U32)],
        compiler_params=_cparams("parallel", "arbitrary", "arbitrary"),
        name=name,
    )(*args)


def _filt_pos_half(shape, n2, n2_total, d):
    row = lax.broadcasted_iota(jnp.int32, shape, 0)
    if d == 0:
        return (row * n2_total + n2).astype(F32)
    pos_b = (shape[0] - row) * n2_total - n2
    return jnp.where(jnp.logical_and(row == 0, n2 == 0), 0, pos_b).astype(F32)


def _filt_pos(shape, n2, n2_total):
    half = (shape[0] // 2, shape[1])
    return jnp.concatenate([_filt_pos_half(half, n2, n2_total, d) for d in range(2)], axis=0)


def _filt_hidden_body(fv_ref, w1_ref, b1_ref, w2_ref, b2_ref, fr_ref, h_ref, *, seq_len, n2_total):
    tj, n1 = h_ref.shape[0], h_ref.shape[1]
    lane = lax.broadcasted_iota(jnp.int32, (n1, LANES), 1)
    for jj in range(tj):
        pos = _filt_pos((n1, LANES), pl.program_id(0) * tj + jj, n2_total)
        t = pos / float(max(seq_len - 1, 1))
        ang = ((2.0 * math.pi / seq_len) * pos) * fv_ref[...]
        z = jnp.where(lane == 0, t, jnp.where(lane <= POS_BANDS, jnp.cos(ang),
                                              jnp.where(lane <= 2 * POS_BANDS, -jnp.sin(ang), 0.0)))
        h = jnp.sin(fr_ref[...] * (jnp.dot(z, w1_ref[...], precision=HI, preferred_element_type=F32) + b1_ref[...]))
        h = jnp.sin(fr_ref[...] * (jnp.dot(h, w2_ref[...], precision=HI, preferred_element_type=F32) + b2_ref[...]))
        hi = h.astype(BF16)
        lo = (h - hi.astype(F32)).astype(BF16)
        h_ref[jj] = jnp.concatenate([hi, hi, lo], axis=1)


def _filt_hidden(fvec, w1p, b1, w2, b2, freq, cfg, seq_len):
    n2, n1, tj = cfg["N2"], cfg["N1"], cfg["tj"]
    return pl.pallas_call(
        functools.partial(_filt_hidden_body, seq_len=seq_len, n2_total=n2),
        grid=(n2 // tj,),
        in_specs=[_const_spec((1, LANES)), _const_spec((LANES, FILT_HIDDEN)), _const_spec((1, FILT_HIDDEN)),
                  _const_spec((FILT_HIDDEN, FILT_HIDDEN)), _const_spec((1, FILT_HIDDEN)),
                  _const_spec((1, FILT_HIDDEN))],
        out_specs=pl.BlockSpec((tj, n1, 3 * FILT_HIDDEN), lambda j: (j, 0, 0)),
        out_shape=jax.ShapeDtypeStruct((n2, n1, 3 * FILT_HIDDEN), BF16),
        compiler_params=_cparams("parallel"),
        name="filt_hidden",
    )(fvec, w1p, b1, w2, b2, freq)


def _filt_spec_body(h_ref, w3_ref, dec_ref, bias_ref, wf_ref, f_ref, tw_ref, o_ref, scr, nrm_ref,
                    *, seq_len, n2, n1, pitch, tj, tk, s1):
    s = pl.program_id(2)
    r = n1 // 2
    ct = o_ref.shape[-1]

    @pl.when(s == 0)
    def _():
        nrm_ref[...] = jnp.zeros_like(nrm_ref)

    @pl.when(s < s1)
    def _():
        row = lax.broadcasted_iota(jnp.int32, (r, ct), 0)
        for jj in range(tj):
            n2i = s * tj + jj
            halves = []
            for d in range(2):
                val = jnp.dot(h_ref[jj, d * r:(d + 1) * r, :], w3_ref[0, d], preferred_element_type=F32)
                t = _filt_pos_half((r, ct), n2i, n2, d) / float(max(seq_len - 1, 1))
                val = val * jnp.exp(-t * jnp.abs(dec_ref[0, d:d + 1, :]))
                nrm_ref[...] += jnp.sum(jnp.abs(val), axis=0, keepdims=True)
                if d == 1:
                    val = jnp.where(jnp.logical_and(row == 0, n2i == 0), 0.0, val)
                halves.append(val)
            _stage1_to_scratch(jnp.concatenate(halves, axis=0).astype(BF16), wf_ref, tw_ref, jj, scr,
                               n2i, n1, pitch)

    @pl.when(s >= s1)
    def _():
        inv = 1.0 / nrm_ref[...]
        bias = bias_ref[0]
        for kk in range(tk):
            base = pl.multiple_of(((s - s1) * tk + kk) * pitch, SUBLANES)
            sr, si = _stage2_fwd(scr, base, n2, f_ref)
            o_ref[0, kk] = _pack(sr * inv + bias, si * inv)


def _filt_spec(hid, w3, dec, bias, tabs, cfg, seq_len):
    n2, n1, tj, tk, ct, pitch = (cfg[k] for k in ("N2", "N1", "tj", "tk", "ct", "pitch"))
    s1, s2 = n2 // tj, n1 // tk
    return pl.pallas_call(
        functools.partial(_filt_spec_body, seq_len=seq_len, n2=n2, n1=n1, pitch=pitch, tj=tj, tk=tk, s1=s1),
        grid=(2, HY_CH // ct, s1 + s2),
        in_specs=[pl.BlockSpec((tj, n1, 3 * FILT_HIDDEN), lambda o, c, s: (jnp.minimum(s, s1 - 1), 0, 0)),
                  pl.BlockSpec((1, 2, 3 * FILT_HIDDEN, ct), lambda o, c, s: (o, 0, 0, c)),
                  pl.BlockSpec((1, 2, ct), lambda o, c, s: (o, 0, c)),
                  pl.BlockSpec((1, 1, ct), lambda o, c, s: (o, 0, c)),
                  _const_spec((2 * n1, n1)), _const_spec((2 * n2, 2 * n2)),
                  pl.BlockSpec((1, 2, n1, tj), lambda o, c, s: (jnp.minimum(s, s1 - 1), 0, 0, 0))],
        out_specs=pl.BlockSpec((1, tk, n2, ct), lambda o, c, s: (o, jnp.maximum(s - s1, 0), 0, c)),
        out_shape=jax.ShapeDtypeStruct((2, n1, n2, HY_CH), U32),
        scratch_shapes=[pltpu.VMEM((ct // LANES, n1 * pitch, LANES), U32), pltpu.VMEM((1, ct), F32)],
        compiler_params=_cparams("parallel", "arbitrary", "arbitrary"),
        name="filt_spec",
    )(hid, w3, dec, bias, tabs["wf"], tabs["f2"], tabs["tw"])


def _pool_body(m_ref, p_ref, n_ref, pw_ref, ps_ref, gp_ref, o_ref, *, seq_len, n2_total):
    tj, r = m_ref.shape[1], m_ref.shape[2]
    j = pl.program_id(1)
    first, last = j == 0, j == pl.num_programs(1) - 1
    h = POOL_HALO
    j0 = j * tj
    plane = lax.broadcasted_iota(jnp.int32, (tj, r, POOL_GROUP), 0)
    row = lax.broadcasted_iota(jnp.int32, (tj, r, POOL_GROUP), 1)
    t = row * n2_total + j0 + plane
    outs = []
    for g, win in enumerate(POOL_WINDOWS):
        sl = slice(g * POOL_GROUP, (g + 1) * POOL_GROUP)
        ext = jnp.concatenate([_wrap_prev(p_ref[0, :, :, sl], first), m_ref[0, :, :, sl],
                               _wrap_next(n_ref[0, :, :, sl], last)], axis=0)
        acc = ext[h - win // 2:h - win // 2 + tj]
        for o in range(1 - win // 2, win // 2):
            acc = acc + ext[h + o:h + o + tj]
        lo = jnp.maximum(t - win // 2, 0)
        hi = jnp.minimum(t + win // 2 - 1, seq_len - 1)
        cnt = (hi - lo + 1).astype(F32)
        dgrp = acc / cnt - ext[h:h + tj]
        outs.append(jnp.dot(dgrp.reshape(tj * r, POOL_GROUP).astype(BF16), pw_ref[g],
                            preferred_element_type=F32))
    y = jnp.concatenate(outs, axis=-1) * ps_ref[...]
    o_ref[0] = _rms(y, gp_ref[...]).astype(BF16).reshape(tj, r, POOL_CH)


def _pool(u, pool_w, pool_scale, g_pool, cfg, seq_len):
    n2, r, tj = cfg["N2"], cfg["R"], cfg["tj_pool"]
    b = u.shape[0]
    goff = 3 * HY_CH // POOL_CH
    h = POOL_HALO
    nh = n2 // h
    return pl.pallas_call(
        functools.partial(_pool_body, seq_len=seq_len, n2_total=n2),
        grid=(b, n2 // tj),
        in_specs=[pl.BlockSpec((1, tj, r, POOL_CH), lambda i, j: (i, j, 0, goff)),
                  pl.BlockSpec((1, h, r, POOL_CH), lambda i, j: (i, (j * (tj // h) + nh - 1) % nh, 0, goff)),
                  pl.BlockSpec((1, h, r, POOL_CH), lambda i, j: (i, ((j + 1) * (tj // h)) % nh, 0, goff)),
                  _const_spec((len(POOL_WINDOWS), POOL_GROUP, POOL_GROUP)),
                  _const_spec((1, POOL_CH)), _const_spec((1, POOL_CH))],
        out_specs=pl.BlockSpec((1, tj, r, POOL_CH), lambda i, j: (i, j, 0, 0)),
        out_shape=jax.ShapeDtypeStruct((b, n2, r, POOL_CH), BF16),
        compiler_params=_cparams("parallel", "arbitrary"),
        name="pool",
    )(u, u, u, pool_w, pool_scale, g_pool)


def _fft_cfg(seq_len):
    n2 = 128 if seq_len >= 8192 else 64
    r = seq_len // n2
    assert r % SUBLANES == 0 and r * n2 == seq_len
    tj = 8 if seq_len >= 8192 else 32
    return dict(N2=n2, R=r, N1=2 * r, tj=tj, tk=tj, ct=256, pitch=n2 + SUBLANES, tj_pool=8)


def _dft_tables(cfg):
    n2, r, n1, tj = cfg["N2"], cfg["R"], cfg["N1"], cfg["tj"]
    n = n1 * n2

    def cis(idx, period):
        a = -2.0 * np.pi * (idx % period) / period
        return np.cos(a), np.sin(a)

    k1 = np.arange(n1)
    mr, mi = cis(k1[:, None] * k1[None, :], n1)
    mrd, mid = mr[:, :r], mi[:, :r]
    w1 = np.block([[mrd, -mid], [mid, mrd]])
    wf = np.concatenate([mr, mi], axis=0)
    k2 = np.arange(n2)
    fr, fi = cis(k2[:, None] * k2[None, :], n2)
    f2 = np.block([[fr, -fi], [fi, fr]])
    g2 = np.block([[fr, fi], [-fi, fr]])
    twr, twi = cis(k1[:, None] * k2[None, :], n)
    tw = np.stack([twr, twi]).reshape(2, n1, n2 // tj, tj).transpose(2, 0, 1, 3)
    return dict(w1=jnp.asarray(w1, BF16), w1inv=jnp.asarray(w1.T / n, BF16), wf=jnp.asarray(wf, BF16),
                f2=jnp.asarray(f2, BF16), g2=jnp.asarray(g2, BF16), tw=jnp.asarray(tw, F32))


def _to_plane_order(x, cfg):
    b, l, d = x.shape
    return jnp.swapaxes(x.reshape(b, cfg["R"], cfg["N2"], d), 1, 2).reshape(b * l, d)


def _from_plane_order(x, b, cfg):
    d = x.shape[-1]
    return jnp.swapaxes(x.reshape(b, cfg["N2"], cfg["R"], d), 1, 2).reshape(b, cfg["N2"] * cfg["R"], d)


def _trunk(x, mem, lw, g_final, depth):
    b, seq_len, d = x.shape
    cfg = _fft_cfg(seq_len)
    tabs = _dft_tables(cfg)
    n2, r, ct = cfg["N2"], cfg["R"], cfg["ct"]
    xs = _to_plane_order(x, cfg)
    memf = mem.reshape(b * N_MEM, d)
    fvec = np.zeros((1, LANES), np.float32)
    bands = np.linspace(1e-4, POS_BANDS - 1, POS_BANDS, dtype=np.float32)
    fvec[0, 1:1 + POS_BANDS] = bands
    fvec[0, 1 + POS_BANDS:1 + 2 * POS_BANDS] = bands
    fvec = jnp.asarray(fvec)
    cpb = HY_CH // ct
    for i in range(depth):
        w = {k: v[i] for k, v in lw.items() if k not in _STACKED_WEIGHTS}
        w1p = jnp.zeros((LANES, FILT_HIDDEN), F32).at[:2 * POS_BANDS + 1].set(w["filt_w1"])
        hid = _filt_hidden(fvec, w1p, w["filt_b1"].reshape(1, -1), w["filt_w2"], w["filt_b2"].reshape(1, -1),
                           w["filt_freq"].reshape(1, -1), cfg, seq_len)
        w3 = jnp.transpose(w["filt_w3"].reshape(FILT_HIDDEN, 2, 2, HY_CH), (1, 2, 0, 3))
        w3hi = w3.astype(BF16)
        w3lo = (w3 - w3hi.astype(F32)).astype(BF16)
        kf = _filt_spec(hid, jnp.concatenate([w3hi, w3lo, w3hi], axis=2),
                        w["hy_decay"], w["hy_bias"].reshape(2, 1, HY_CH), tabs, cfg, seq_len)
        u = _norm_matmul(xs, w["g_mix"], lw["w_in"], i, F32, 1024, 1024, "in_proj").reshape(b, n2, r, D_IN)
        cw, cbias = w["conv_w"], w["conv_b"].reshape(1, -1)
        cws = [cw[:, k * HY_CH:(k + 1) * HY_CH] for k in range(3)]
        cbs = [cbias[:, k * HY_CH:(k + 1) * HY_CH] for k in range(3)]
        z1 = _lconv(u, 2 * cpb, cws[2], cbs[2], u, 0, cws[0], cbs[0], kf, 0, tabs, cfg, "lconv1")
        yh = _lconv(z1, 0, None, None, u, cpb, cws[1], cbs[1], kf, 1, tabs, cfg, "lconv2")
        yp = _pool(u, w["pool_w"], w["pool_scale"].reshape(1, -1), w["g_pool"].reshape(1, -1), cfg, seq_len)
        xs = _outproj(xs, yh.reshape(b * seq_len, HY_CH), yp.reshape(b * seq_len, POOL_CH), w["g_hy"],
                      lw["w_out"], i)
        kv = _norm_matmul(memf, w["g_mem"], lw["w_kv"], i, BF16, 512, 1024, "kv_proj").reshape(b, N_MEM, 2 * d)
        xs = _attn(xs.reshape(b, seq_len, d), kv, w["g_xa"], lw["w_q"], lw["w_o"], i).reshape(b * seq_len, d)
        xs = _mlp(xs, w["g_mlp"], lw["w_up"], lw["w_down"], i, g_final, i == depth - 1)
    return _from_plane_order(xs, b, cfg)


_STACKED_WEIGHTS = ("w_in", "w_out", "w_q", "w_kv", "w_o", "w_up", "w_down")
_MATMUL_WEIGHTS = _STACKED_WEIGHTS + ("pool_w",)


def kernel(x_prompt, x_sample, mem_prompt, mem_sample, g_mix, w_in, conv_w, conv_b, filt_w1, filt_b1, filt_w2, filt_b2, filt_freq, filt_w3, hy_decay, hy_bias, pool_w, pool_scale, g_hy, g_pool, w_out, g_xa, g_mem, w_q, w_kv, w_o, g_mlp, w_up, w_down, g_final):
    lw = dict(g_mix=g_mix, w_in=w_in, conv_w=conv_w, conv_b=conv_b, filt_w1=filt_w1, filt_b1=filt_b1,
              filt_w2=filt_w2, filt_b2=filt_b2, filt_freq=filt_freq, filt_w3=filt_w3, hy_decay=hy_decay,
              hy_bias=hy_bias, pool_w=pool_w, pool_scale=pool_scale, g_hy=g_hy, g_pool=g_pool, w_out=w_out,
              g_xa=g_xa, g_mem=g_mem, w_q=w_q, w_kv=w_kv, w_o=w_o, g_mlp=g_mlp, w_up=w_up, w_down=w_down)
    lw = {k: (v.astype(BF16) if k in _MATMUL_WEIGHTS else v) for k, v in lw.items()}
    depth = g_mix.shape[0]
    y_prompt = _trunk(x_prompt, mem_prompt, lw, g_final, depth)
    y_sample = _trunk(x_sample, mem_sample, lw, g_final, depth)
    return (y_prompt, y_sample)
```

```python
import functools
import math

import numpy as np
import jax
import jax.numpy as jnp
from jax import lax
from jax.experimental import pallas as pl
from jax.experimental.pallas import tpu as pltpu

F32 = jnp.float32
BF16 = jnp.bfloat16
U32 = jnp.uint32

D_MODEL = 2048
HY_CH = 1024
POOL_CH = 1024
D_IN = 3 * HY_CH + POOL_CH
POOL_WINDOWS = (2, 4, 8, 16)
POOL_GROUP = POOL_CH // len(POOL_WINDOWS)
POOL_HALO = max(POOL_WINDOWS) // 2
POS_BANDS = 16
FILT_HIDDEN = 64
N_MEM = 256
XA_HEADS = 4
XA_HEAD_DIM = D_MODEL // XA_HEADS
D_FF = 4 * D_MODEL
EPS = 1e-6

LANES = 128
SUBLANES = 8
VMEM_LIMIT_BYTES = 58 * 1024 * 1024
NORM_CHUNK_ROWS = 256
HI = lax.Precision.HIGHEST


def _cparams(*sem):
    return pltpu.CompilerParams(dimension_semantics=sem, vmem_limit_bytes=VMEM_LIMIT_BYTES)


def _rms(x, g):
    return x * lax.rsqrt(jnp.mean(x * x, axis=-1, keepdims=True) + EPS) * g


def _const_spec(shape):
    nd = len(shape)
    return pl.BlockSpec(shape, lambda *_: (0,) * nd)


def _norm_matmul_body(x_ref, g_ref, w_ref, o_ref, h_ref):
    rows = NORM_CHUNK_ROWS

    @pl.when(pl.program_id(1) == 0)
    def _():
        for c in range(x_ref.shape[0] // rows):
            sl = slice(c * rows, (c + 1) * rows)
            h = _rms(x_ref[sl, :], g_ref[...]).astype(BF16)
            h_ref[sl, :] = h
            o_ref[sl, :] = jnp.dot(h, w_ref[...], preferred_element_type=F32).astype(o_ref.dtype)

    @pl.when(pl.program_id(1) != 0)
    def _():
        o_ref[...] = jnp.dot(h_ref[...], w_ref[...], preferred_element_type=F32).astype(o_ref.dtype)


def _norm_matmul(x, g, w, layer, out_dtype, tm, tn, name):
    t, d = x.shape
    n = w.shape[2]
    return pl.pallas_call(
        _norm_matmul_body,
        grid=(t // tm, n // tn),
        in_specs=[pl.BlockSpec((tm, d), lambda i, j: (i, 0)),
                  pl.BlockSpec((1, d), lambda i, j: (0, 0)),
                  pl.BlockSpec((None, d, tn), lambda i, j: (layer, 0, j))],
        out_specs=pl.BlockSpec((tm, tn), lambda i, j: (i, j)),
        out_shape=jax.ShapeDtypeStruct((t, n), out_dtype),
        scratch_shapes=[pltpu.VMEM((tm, d), BF16)],
        compiler_params=_cparams("parallel", "arbitrary"),
        name=name,
    )(x, g.reshape(1, d), w)


def _mlp_body(x_ref, g_ref, wu_ref, wd_ref, gf_ref, o_ref, h_ref, *, final_norm):
    f = pl.program_id(1)

    @pl.when(f == 0)
    def _():
        x = x_ref[...]
        h_ref[...] = _rms(x, g_ref[...]).astype(BF16)
        o_ref[...] = x

    hm = jnp.dot(h_ref[...], wu_ref[...], preferred_element_type=F32)
    a = jnp.square(jnp.maximum(hm, 0.0)).astype(BF16)
    o_ref[...] += jnp.dot(a, wd_ref[...], preferred_element_type=F32)

    if final_norm:
        @pl.when(f == pl.num_programs(1) - 1)
        def _():
            o_ref[...] = _rms(o_ref[...], gf_ref[...])


def _mlp(x, g, wu, wd, layer, g_final, final_norm, tm=512, tf=1024):
    t, d = x.shape
    ff = wu.shape[2]
    return pl.pallas_call(
        functools.partial(_mlp_body, final_norm=final_norm),
        grid=(t // tm, ff // tf),
        in_specs=[pl.BlockSpec((tm, d), lambda i, f: (i, 0)),
                  pl.BlockSpec((1, d), lambda i, f: (0, 0)),
                  pl.BlockSpec((None, d, tf), lambda i, f: (layer, 0, f)),
                  pl.BlockSpec((None, tf, d), lambda i, f: (layer, f, 0)),
                  pl.BlockSpec((1, d), lambda i, f: (0, 0))],
        out_specs=pl.BlockSpec((tm, d), lambda i, f: (i, 0)),
        out_shape=jax.ShapeDtypeStruct((t, d), F32),
        scratch_shapes=[pltpu.VMEM((tm, d), BF16)],
        compiler_params=_cparams("parallel", "arbitrary"),
        name="mlp",
    )(x, g.reshape(1, d), wu, wd, g_final.reshape(1, d))


def _attn_body(x_ref, k_ref, v_ref, g_ref, wq_ref, wo_ref, o_ref):
    scale = XA_HEAD_DIM ** -0.5
    rows = NORM_CHUNK_ROWS
    for c in range(x_ref.shape[1] // rows):
        x = x_ref[0, c * rows:(c + 1) * rows, :]
        h = _rms(x, g_ref[...]).astype(BF16)
        q = jnp.dot(h, wq_ref[...], preferred_element_type=F32)
        heads = []
        for hd in range(XA_HEADS):
            sl = slice(hd * XA_HEAD_DIM, (hd + 1) * XA_HEAD_DIM)
            s = lax.dot_general(q[:, sl].astype(BF16), k_ref[0, :, sl], (((1,), (1,)), ((), ())),
                                preferred_element_type=F32) * scale
            p = jnp.exp(s - jnp.max(s, axis=-1, keepdims=True))
            a = (p / jnp.sum(p, axis=-1, keepdims=True)).astype(BF16)
            heads.append(jnp.dot(a, v_ref[0, :, sl], preferred_element_type=F32).astype(BF16))
        o = jnp.concatenate(heads, axis=-1)
        o_ref[0, c * rows:(c + 1) * rows, :] = x + jnp.dot(o, wo_ref[...], preferred_element_type=F32)


def _attn(x, kv, g, wq, wo, layer, tq=512):
    b, l, d = x.shape
    return pl.pallas_call(
        _attn_body,
        grid=(b, l // tq),
        in_specs=[pl.BlockSpec((1, tq, d), lambda i, j: (i, j, 0)),
                  pl.BlockSpec((1, N_MEM, d), lambda i, j: (i, 0, 0)),
                  pl.BlockSpec((1, N_MEM, d), lambda i, j: (i, 0, 1)),
                  pl.BlockSpec((1, d), lambda i, j: (0, 0)),
                  pl.BlockSpec((None, d, d), lambda i, j: (layer, 0, 0)),
                  pl.BlockSpec((None, d, d), lambda i, j: (layer, 0, 0))],
        out_specs=pl.BlockSpec((1, tq, d), lambda i, j: (i, j, 0)),
        out_shape=jax.ShapeDtypeStruct((b, l, d), F32),
        compiler_params=_cparams("parallel", "arbitrary"),
        name="attn",
    )(x, kv, kv, g.reshape(1, d), wq, wo)


def _outproj_body(x_ref, yh_ref, yp_ref, gh_ref, wt_ref, wb_ref, o_ref):
    yh = _rms(yh_ref[...], gh_ref[...]).astype(BF16)
    acc = jnp.dot(yh, wt_ref[...], preferred_element_type=F32)
    acc += jnp.dot(yp_ref[...], wb_ref[...], preferred_element_type=F32)
    o_ref[...] = x_ref[...] + acc


def _outproj(x, yh, yp, g_hy, w_out, layer, tm=512):
    t, d = x.shape
    return pl.pallas_call(
        _outproj_body,
        grid=(t // tm,),
        in_specs=[pl.BlockSpec((tm, d), lambda i: (i, 0)),
                  pl.BlockSpec((tm, HY_CH), lambda i: (i, 0)),
                  pl.BlockSpec((tm, POOL_CH), lambda i: (i, 0)),
                  pl.BlockSpec((1, HY_CH), lambda i: (0, 0)),
                  pl.BlockSpec((None, HY_CH, d), lambda i: (layer, 0, 0)),
                  pl.BlockSpec((None, POOL_CH, d), lambda i: (layer, 1, 0))],
        out_specs=pl.BlockSpec((tm, d), lambda i: (i, 0)),
        out_shape=jax.ShapeDtypeStruct((t, d), F32),
        compiler_params=_cparams("parallel"),
        name="outproj",
    )(x, yh, yp, g_hy.reshape(1, HY_CH), w_out, w_out)


def _shift_rows(x, down):
    r = x.shape[-2]
    rows = lax.broadcasted_iota(jnp.int32, x.shape, x.ndim - 2)
    if down:
        return jnp.where(rows == 0, 0.0, pltpu.roll(x, 1, axis=x.ndim - 2))
    return jnp.where(rows == r - 1, 0.0, pltpu.roll(x, r - 1, axis=x.ndim - 2))


def _wrap_prev(x, is_first):
    return jnp.where(is_first, _shift_rows(x, True), x)


def _wrap_next(x, is_last):
    return jnp.where(is_last, _shift_rows(x, False), x)


def _short_conv_plane(jj, main_ref, prev_ref, next_ref, w_ref, b_ref, is_first, is_last):
    tj = main_ref.shape[1]
    lo = main_ref[:, jj - 1] if jj > 0 else _wrap_prev(prev_ref[:, 0], is_first)
    hi = main_ref[:, jj + 1] if jj < tj - 1 else _wrap_next(next_ref[:, 0], is_last)
    return lo * w_ref[0:1, :] + main_ref[:, jj] * w_ref[1:2, :] + hi * w_ref[2:3, :] + b_ref[...]


def _pack(re, im):
    rb = lax.bitcast_convert_type(re.astype(BF16).astype(F32), U32)
    ib = lax.bitcast_convert_type(im.astype(BF16).astype(F32), U32)
    return rb | (ib >> 16)


def _unpack(p):
    return (lax.bitcast_convert_type(p & jnp.uint32(0xFFFF0000), F32),
            lax.bitcast_convert_type(p << 16, F32))


def _split_ri(x):
    m, c = x.shape
    x4 = x.reshape(m // (2 * SUBLANES), 2, SUBLANES, c)
    return x4[:, 0].reshape(m // 2, c), x4[:, 1].reshape(m // 2, c)


def _join_ri(re, im):
    n, c = re.shape
    g = n // SUBLANES
    return jnp.stack([re.reshape(g, SUBLANES, c), im.reshape(g, SUBLANES, c)], axis=1).reshape(2 * n, c)


def _store_cols(scr, n2i, n1, pitch, val):
    for l in range(scr.shape[0]):
        scr[l, pl.ds(n2i, n1, stride=pitch), :] = val[:, l * LANES:(l + 1) * LANES]


def _load_cols(scr, n2i, n1, pitch):
    return jnp.concatenate([scr[l, pl.ds(n2i, n1, stride=pitch), :] for l in range(scr.shape[0])], axis=1)


def _store_rows(scr, base, n2, val):
    for l in range(scr.shape[0]):
        scr[l, pl.ds(base, n2), :] = val[:, l * LANES:(l + 1) * LANES]


def _load_rows(scr, base, n2):
    return jnp.concatenate([scr[l, pl.ds(base, n2), :] for l in range(scr.shape[0])], axis=1)


def _stage1_to_scratch(x_bf16, w_ref, tw_ref, jj, scr, n2i, n1, pitch):
    res = jnp.dot(w_ref[...], x_bf16, preferred_element_type=F32)
    ar, ai = _split_ri(res)
    twr, twi = tw_ref[0, 0, :, jj:jj + 1], tw_ref[0, 1, :, jj:jj + 1]
    _store_cols(scr, n2i, n1, pitch, _pack(ar * twr - ai * twi, ar * twi + ai * twr))


def _stage2_fwd(scr, base, n2, f_ref):
    xr, xi = _unpack(_load_rows(scr, base, n2))
    s = jnp.dot(f_ref[...], _join_ri(xr, xi).astype(BF16), preferred_element_type=F32)
    return _split_ri(s)


def _lconv_body(*refs, conv, n2, n1, pitch, tj, tk, s1, s2):
    if conv:
        (vm_ref, vp_ref, vn_ref, vcw_ref, vcb_ref, gm_ref, gp_ref, gn_ref, gcw_ref, gcb_ref,
         kf_ref, w_ref, wi_ref, f_ref, g_ref, tw_ref, o_ref, scr) = refs
    else:
        (vm_ref, gm_ref, gp_ref, gn_ref, gcw_ref, gcb_ref,
         kf_ref, w_ref, wi_ref, f_ref, g_ref, tw_ref, o_ref, scr) = refs
    s = pl.program_id(2)

    @pl.when(s < s1)
    def _():
        for jj in range(tj):
            if conv:
                v = _short_conv_plane(jj, vm_ref, vp_ref, vn_ref, vcw_ref, vcb_ref, s == 0, s == s1 - 1)
            else:
                v = vm_ref[:, jj]
            nb, r, ct = v.shape
            _stage1_to_scratch(v.reshape(nb * r, ct).astype(BF16), w_ref, tw_ref, jj, scr,
                               s * tj + jj, n1, pitch)

    @pl.when(jnp.logical_and(s >= s1, s < s1 + s2))
    def _():
        for kk in range(tk):
            base = pl.multiple_of(((s - s1) * tk + kk) * pitch, SUBLANES)
            sr, si = _stage2_fwd(scr, base, n2, f_ref)
            kr, ki = _unpack(kf_ref[0, kk])
            y = _join_ri(sr * kr - si * ki, sr * ki + si * kr).astype(BF16)
            a = jnp.dot(g_ref[...], y, preferred_element_type=F32)
            _store_rows(scr, base, n2, _pack(*_split_ri(a)))

    @pl.when(s >= s1 + s2)
    def _():
        jp = s - s1 - s2
        for jj in range(tj):
            gate = _short_conv_plane(jj, gm_ref, gp_ref, gn_ref, gcw_ref, gcb_ref, jp == 0, jp == s1 - 1)
            ar, ai = _unpack(_load_cols(scr, jp * tj + jj, n1, pitch))
            twr, twi = tw_ref[0, 0, :, jj:jj + 1], tw_ref[0, 1, :, jj:jj + 1]
            x = _join_ri(ar * twr + ai * twi, ai * twr - ar * twi).astype(BF16)
            y = jnp.dot(wi_ref[...], x, preferred_element_type=F32)
            o_ref[:, jj] = (gate * y.reshape(gate.shape)).astype(o_ref.dtype)


def _lconv(vsrc, v_goff, v_cw, v_cb, u, g_goff, g_cw, g_cb, kf, order, tabs, cfg, out_dtype, name):
    n2, r, n1, tj, tk, ct, pitch = (cfg[k] for k in ("N2", "R", "N1", "tj", "tk", "ct", "pitch"))
    p = vsrc.shape[0] // 2
    s1, s2 = n2 // tj, n1 // tk
    conv = v_cw is not None

    def jv(s):
        return jnp.minimum(s, s1 - 1)

    def jg(s):
        return jnp.maximum(s - s1 - s2, 0)

    def jt(s):
        return jnp.where(s < s1, s, jg(s))

    def plane_specs(goff, jf):
        return [pl.BlockSpec((2, tj, r, ct), lambda p_, c, s: (p_, jf(s), 0, goff + c)),
                pl.BlockSpec((2, 1, r, ct), lambda p_, c, s: (p_, (jf(s) * tj + n2 - 1) % n2, 0, goff + c)),
                pl.BlockSpec((2, 1, r, ct), lambda p_, c, s: (p_, ((jf(s) + 1) * tj) % n2, 0, goff + c))]

    wspecs = [pl.BlockSpec((3, ct), lambda p_, c, s: (0, c)), pl.BlockSpec((1, ct), lambda p_, c, s: (0, c))]
    if conv:
        in_specs = plane_specs(v_goff, jv) + wspecs
        args = [vsrc, vsrc, vsrc, v_cw, v_cb]
    else:
        in_specs = plane_specs(v_goff, jv)[:1]
        args = [vsrc]
    in_specs += plane_specs(g_goff, jg) + wspecs
    args += [u, u, u, g_cw, g_cb]
    in_specs += [pl.BlockSpec((1, tk, n2, ct), lambda p_, c, s: (order, jnp.clip(s - s1, 0, s2 - 1), 0, c)),
                 _const_spec((2 * n1, 2 * r)), _const_spec((2 * r, 2 * n1)),
                 _const_spec((2 * n2, 2 * n2)), _const_spec((2 * n2, 2 * n2)),
                 pl.BlockSpec((1, 2, n1, tj), lambda p_, c, s: (jt(s), 0, 0, 0))]
    args += [kf, tabs["w1"], tabs["w1inv"], tabs["f2"], tabs["g2"], tabs["tw"]]
    return pl.pallas_call(
        functools.partial(_lconv_body, conv=conv, n2=n2, n1=n1, pitch=pitch, tj=tj, tk=tk, s1=s1, s2=s2),
        grid=(p, HY_CH // ct, 2 * s1 + s2),
        in_specs=in_specs,
        out_specs=pl.BlockSpec((2, tj, r, ct), lambda p_, c, s: (p_, jg(s), 0, c)),
        out_shape=jax.ShapeDtypeStruct((2 * p, n2, r, HY_CH), out_dtype),
        scratch_shapes=[pltpu.VMEM((ct // LANES, n1 * pitch, LANES), U32)],
        compiler_params=_cparams("parallel", "arbitrary", "arbitrary"),
        name=name,
    )(*args)


def _filt_pos_half(shape, n2, n2_total, d):
    row = lax.broadcasted_iota(jnp.int32, shape, 0)
    if d == 0:
        return (row * n2_total + n2).astype(F32)
    pos_b = (shape[0] - row) * n2_total - n2
    return jnp.where(jnp.logical_and(row == 0, n2 == 0), 0, pos_b).astype(F32)


def _filt_pos(shape, n2, n2_total):
    half = (shape[0] // 2, shape[1])
    return jnp.concatenate([_filt_pos_half(half, n2, n2_total, d) for d in range(2)], axis=0)


def _filt_hidden_body(fv_ref, w1_ref, b1_ref, w2_ref, b2_ref, fr_ref, h_ref, *, seq_len, n2_total):
    tj, n1 = h_ref.shape[0], h_ref.shape[1]
    lane = lax.broadcasted_iota(jnp.int32, (n1, LANES), 1)
    for jj in range(tj):
        pos = _filt_pos((n1, LANES), pl.program_id(0) * tj + jj, n2_total)
        t = pos / float(max(seq_len - 1, 1))
        ang = ((2.0 * math.pi / seq_len) * pos) * fv_ref[...]
        z = jnp.where(lane == 0, t, jnp.where(lane <= POS_BANDS, jnp.cos(ang),
                                              jnp.where(lane <= 2 * POS_BANDS, -jnp.sin(ang), 0.0)))
        h = jnp.sin(fr_ref[...] * (jnp.dot(z, w1_ref[...], precision=HI, preferred_element_type=F32) + b1_ref[...]))
        h = jnp.sin(fr_ref[...] * (jnp.dot(h, w2_ref[...], precision=HI, preferred_element_type=F32) + b2_ref[...]))
        hi = h.astype(BF16)
        lo = (h - hi.astype(F32)).astype(BF16)
        h_ref[jj] = jnp.concatenate([hi, hi, lo], axis=1)


def _filt_hidden(fvec, w1p, b1, w2, b2, freq, cfg, seq_len):
    n2, n1, tj = cfg["N2"], cfg["N1"], cfg["tj"]
    return pl.pallas_call(
        functools.partial(_filt_hidden_body, seq_len=seq_len, n2_total=n2),
        grid=(n2 // tj,),
        in_specs=[_const_spec((1, LANES)), _const_spec((LANES, FILT_HIDDEN)), _const_spec((1, FILT_HIDDEN)),
                  _const_spec((FILT_HIDDEN, FILT_HIDDEN)), _const_spec((1, FILT_HIDDEN)),
                  _const_spec((1, FILT_HIDDEN))],
        out_specs=pl.BlockSpec((tj, n1, 3 * FILT_HIDDEN), lambda j: (j, 0, 0)),
        out_shape=jax.ShapeDtypeStruct((n2, n1, 3 * FILT_HIDDEN), BF16),
        compiler_params=_cparams("parallel"),
        name="filt_hidden",
    )(fvec, w1p, b1, w2, b2, freq)


def _filt_spec_body(h_ref, w3_ref, dec_ref, bias_ref, wf_ref, f_ref, tw_ref, o_ref, scr, nrm_ref,
                    *, seq_len, n2, n1, pitch, tj, tk, s1):
    s = pl.program_id(2)
    r = n1 // 2
    ct = o_ref.shape[-1]

    @pl.when(s == 0)
    def _():
        nrm_ref[...] = jnp.zeros_like(nrm_ref)

    @pl.when(s < s1)
    def _():
        row = lax.broadcasted_iota(jnp.int32, (r, ct), 0)
        for jj in range(tj):
            n2i = s * tj + jj
            halves = []
            for d in range(2):
                val = jnp.dot(h_ref[jj, d * r:(d + 1) * r, :], w3_ref[0, d], preferred_element_type=F32)
                t = _filt_pos_half((r, ct), n2i, n2, d) / float(max(seq_len - 1, 1))
                val = val * jnp.exp(-t * jnp.abs(dec_ref[0, d:d + 1, :]))
                nrm_ref[...] += jnp.sum(jnp.abs(val), axis=0, keepdims=True)
                if d == 1:
                    val = jnp.where(jnp.logical_and(row == 0, n2i == 0), 0.0, val)
                halves.append(val)
            _stage1_to_scratch(jnp.concatenate(halves, axis=0).astype(BF16), wf_ref, tw_ref, jj, scr,
                               n2i, n1, pitch)

    @pl.when(s >= s1)
    def _():
        inv = 1.0 / nrm_ref[...]
        bias = bias_ref[0]
        for kk in range(tk):
            base = pl.multiple_of(((s - s1) * tk + kk) * pitch, SUBLANES)
            sr, si = _stage2_fwd(scr, base, n2, f_ref)
            o_ref[0, kk] = _pack(sr * inv + bias, si * inv)


def _filt_spec(hid, w3, dec, bias, tabs, cfg, seq_len):
    n2, n1, tj, tk, ct, pitch = (cfg[k] for k in ("N2", "N1", "tj", "tk", "ct", "pitch"))
    s1, s2 = n2 // tj, n1 // tk
    return pl.pallas_call(
        functools.partial(_filt_spec_body, seq_len=seq_len, n2=n2, n1=n1, pitch=pitch, tj=tj, tk=tk, s1=s1),
        grid=(2, HY_CH // ct, s1 + s2),
        in_specs=[pl.BlockSpec((tj, n1, 3 * FILT_HIDDEN), lambda o, c, s: (jnp.minimum(s, s1 - 1), 0, 0)),
                  pl.BlockSpec((1, 2, 3 * FILT_HIDDEN, ct), lambda o, c, s: (o, 0, 0, c)),
                  pl.BlockSpec((1, 2, ct), lambda o, c, s: (o, 0, c)),
                  pl.BlockSpec((1, 1, ct), lambda o, c, s: (o, 0, c)),
                  _const_spec((2 * n1, n1)), _const_spec((2 * n2, 2 * n2)),
                  pl.BlockSpec((1, 2, n1, tj), lambda o, c, s: (jnp.minimum(s, s1 - 1), 0, 0, 0))],
        out_specs=pl.BlockSpec((1, tk, n2, ct), lambda o, c, s: (o, jnp.maximum(s - s1, 0), 0, c)),
        out_shape=jax.ShapeDtypeStruct((2, n1, n2, HY_CH), U32),
        scratch_shapes=[pltpu.VMEM((ct // LANES, n1 * pitch, LANES), U32), pltpu.VMEM((1, ct), F32)],
        compiler_params=_cparams("parallel", "arbitrary", "arbitrary"),
        name="filt_spec",
    )(hid, w3, dec, bias, tabs["wf"], tabs["f2"], tabs["tw"])


def _pool_body(m_ref, p_ref, n_ref, pw_ref, ps_ref, gp_ref, o_ref, *, seq_len, n2_total):
    tj, r = m_ref.shape[1], m_ref.shape[2]
    j = pl.program_id(1)
    first, last = j == 0, j == pl.num_programs(1) - 1
    h = POOL_HALO
    j0 = j * tj
    plane = lax.broadcasted_iota(jnp.int32, (tj, r, POOL_GROUP), 0)
    row = lax.broadcasted_iota(jnp.int32, (tj, r, POOL_GROUP), 1)
    t = row * n2_total + j0 + plane
    outs = []
    for g, win in enumerate(POOL_WINDOWS):
        sl = slice(g * POOL_GROUP, (g + 1) * POOL_GROUP)
        ext = jnp.concatenate([_wrap_prev(p_ref[0, :, :, sl], first), m_ref[0, :, :, sl],
                               _wrap_next(n_ref[0, :, :, sl], last)], axis=0)
        acc = ext[h - win // 2:h - win // 2 + tj]
        for o in range(1 - win // 2, win // 2):
            acc = acc + ext[h + o:h + o + tj]
        lo = jnp.maximum(t - win // 2, 0)
        hi = jnp.minimum(t + win // 2 - 1, seq_len - 1)
        cnt = (hi - lo + 1).astype(F32)
        dgrp = acc / cnt - ext[h:h + tj]
        outs.append(jnp.dot(dgrp.reshape(tj * r, POOL_GROUP).astype(BF16), pw_ref[g],
                            preferred_element_type=F32))
    y = jnp.concatenate(outs, axis=-1) * ps_ref[...]
    o_ref[0] = _rms(y, gp_ref[...]).astype(BF16).reshape(tj, r, POOL_CH)


def _pool(u, pool_w, pool_scale, g_pool, cfg, seq_len):
    n2, r, tj = cfg["N2"], cfg["R"], cfg["tj_pool"]
    b = u.shape[0]
    goff = 3 * HY_CH // POOL_CH
    h = POOL_HALO
    nh = n2 // h
    return pl.pallas_call(
        functools.partial(_pool_body, seq_len=seq_len, n2_total=n2),
        grid=(b, n2 // tj),
        in_specs=[pl.BlockSpec((1, tj, r, POOL_CH), lambda i, j: (i, j, 0, goff)),
                  pl.BlockSpec((1, h, r, POOL_CH), lambda i, j: (i, (j * (tj // h) + nh - 1) % nh, 0, goff)),
                  pl.BlockSpec((1, h, r, POOL_CH), lambda i, j: (i, ((j + 1) * (tj // h)) % nh, 0, goff)),
                  _const_spec((len(POOL_WINDOWS), POOL_GROUP, POOL_GROUP)),
                  _const_spec((1, POOL_CH)), _const_spec((1, POOL_CH))],
        out_specs=pl.BlockSpec((1, tj, r, POOL_CH), lambda i, j: (i, j, 0, 0)),
        out_shape=jax.ShapeDtypeStruct((b, n2, r, POOL_CH), BF16),
        compiler_params=_cparams("parallel", "arbitrary"),
        name="pool",
    )(u, u, u, pool_w, pool_scale, g_pool)


def _fft_cfg(seq_len):
    n2 = 128 if seq_len >= 8192 else 64
    r = seq_len // n2
    assert r % SUBLANES == 0 and r * n2 == seq_len
    tj = 8 if seq_len >= 8192 else 32
    return dict(N2=n2, R=r, N1=2 * r, tj=tj, tk=tj, ct=256, pitch=n2 + SUBLANES, tj_pool=8)


def _dft_tables(cfg):
    n2, r, n1, tj = cfg["N2"], cfg["R"], cfg["N1"], cfg["tj"]
    n = n1 * n2

    def cis(idx, period):
        a = -2.0 * np.pi * (idx % period) / period
        return np.cos(a), np.sin(a)

    def ri_order(m):
        pos = np.arange(2 * m)
        return (pos % (2 * SUBLANES)) // SUBLANES * m + pos // (2 * SUBLANES) * SUBLANES + pos % SUBLANES

    k1 = np.arange(n1)
    mr, mi = cis(k1[:, None] * k1[None, :], n1)
    mrd, mid = mr[:, :r], mi[:, :r]
    o1, o2 = ri_order(n1), ri_order(n2)
    w1 = np.block([[mrd, -mid], [mid, mrd]])[o1]
    wf = np.concatenate([mr, mi], axis=0)[o1]
    k2 = np.arange(n2)
    fr, fi = cis(k2[:, None] * k2[None, :], n2)
    f2 = np.block([[fr, -fi], [fi, fr]])[o2][:, o2]
    g2 = np.block([[fr, fi], [-fi, fr]])[o2][:, o2]
    twr, twi = cis(k1[:, None] * k2[None, :], n)
    tw = np.stack([twr, twi]).reshape(2, n1, n2 // tj, tj).transpose(2, 0, 1, 3)
    return dict(w1=jnp.asarray(w1, BF16), w1inv=jnp.asarray(w1.T / n, BF16), wf=jnp.asarray(wf, BF16),
                f2=jnp.asarray(f2, BF16), g2=jnp.asarray(g2, BF16), tw=jnp.asarray(tw, F32))


def _to_plane_order(x, cfg):
    b, l, d = x.shape
    return jnp.swapaxes(x.reshape(b, cfg["R"], cfg["N2"], d), 1, 2).reshape(b * l, d)


def _from_plane_order(x, b, cfg):
    d = x.shape[-1]
    return jnp.swapaxes(x.reshape(b, cfg["N2"], cfg["R"], d), 1, 2).reshape(b, cfg["N2"] * cfg["R"], d)


def _trunk(x, mem, lw, g_final, depth):
    b, seq_len, d = x.shape
    cfg = _fft_cfg(seq_len)
    tabs = _dft_tables(cfg)
    n2, r, ct = cfg["N2"], cfg["R"], cfg["ct"]
    xs = _to_plane_order(x, cfg)
    memf = mem.reshape(b * N_MEM, d)
    fvec = np.zeros((1, LANES), np.float32)
    bands = np.linspace(1e-4, POS_BANDS - 1, POS_BANDS, dtype=np.float32)
    fvec[0, 1:1 + POS_BANDS] = bands
    fvec[0, 1 + POS_BANDS:1 + 2 * POS_BANDS] = bands
    fvec = jnp.asarray(fvec)
    cpb = HY_CH // ct
    for i in range(depth):
        w = {k: v[i] for k, v in lw.items() if k not in _STACKED_WEIGHTS}
        w1p = jnp.zeros((LANES, FILT_HIDDEN), F32).at[:2 * POS_BANDS + 1].set(w["filt_w1"])
        hid = _filt_hidden(fvec, w1p, w["filt_b1"].reshape(1, -1), w["filt_w2"], w["filt_b2"].reshape(1, -1),
                           w["filt_freq"].reshape(1, -1), cfg, seq_len)
        w3 = jnp.transpose(w["filt_w3"].reshape(FILT_HIDDEN, 2, 2, HY_CH), (1, 2, 0, 3))
        w3hi = w3.astype(BF16)
        w3lo = (w3 - w3hi.astype(F32)).astype(BF16)
        kf = _filt_spec(hid, jnp.concatenate([w3hi, w3lo, w3hi], axis=2),
                        w["hy_decay"], w["hy_bias"].reshape(2, 1, HY_CH), tabs, cfg, seq_len)
        u = _norm_matmul(xs, w["g_mix"], lw["w_in"], i, F32, 1024, 1024, "in_proj").reshape(b, n2, r, D_IN)
        cw, cbias = w["conv_w"], w["conv_b"].reshape(1, -1)
        cws = [cw[:, k * HY_CH:(k + 1) * HY_CH] for k in range(3)]
        cbs = [cbias[:, k * HY_CH:(k + 1) * HY_CH] for k in range(3)]
        z1 = _lconv(u, 2 * cpb, cws[2], cbs[2], u, 0, cws[0], cbs[0], kf, 0, tabs, cfg, BF16, "lconv1")
        yh = _lconv(z1, 0, None, None, u, cpb, cws[1], cbs[1], kf, 1, tabs, cfg, F32, "lconv2")
        yp = _pool(u, w["pool_w"], w["pool_scale"].reshape(1, -1), w["g_pool"].reshape(1, -1), cfg, seq_len)
        xs = _outproj(xs, yh.reshape(b * seq_len, HY_CH), yp.reshape(b * seq_len, POOL_CH), w["g_hy"],
                      lw["w_out"], i)
        kv = _norm_matmul(memf, w["g_mem"], lw["w_kv"], i, BF16, 512, 1024, "kv_proj").reshape(b, N_MEM, 2 * d)
        xs = _attn(xs.reshape(b, seq_len, d), kv, w["g_xa"], lw["w_q"], lw["w_o"], i).reshape(b * seq_len, d)
        xs = _mlp(xs, w["g_mlp"], lw["w_up"], lw["w_down"], i, g_final, i == depth - 1)
    return _from_plane_order(xs, b, cfg)


_STACKED_WEIGHTS = ("w_in", "w_out", "w_q", "w_kv", "w_o", "w_up", "w_down")
_MATMUL_WEIGHTS = _STACKED_WEIGHTS + ("pool_w",)


def kernel(x_prompt, x_sample, mem_prompt, mem_sample, g_mix, w_in, conv_w, conv_b, filt_w1, filt_b1, filt_w2, filt_b2, filt_freq, filt_w3, hy_decay, hy_bias, pool_w, pool_scale, g_hy, g_pool, w_out, g_xa, g_mem, w_q, w_kv, w_o, g_mlp, w_up, w_down, g_final):
    lw = dict(g_mix=g_mix, w_in=w_in, conv_w=conv_w, conv_b=conv_b, filt_w1=filt_w1, filt_b1=filt_b1,
              filt_w2=filt_w2, filt_b2=filt_b2, filt_freq=filt_freq, filt_w3=filt_w3, hy_decay=hy_decay,
              hy_bias=hy_bias, pool_w=pool_w, pool_scale=pool_scale, g_hy=g_hy, g_pool=g_pool, w_out=w_out,
              g_xa=g_xa, g_mem=g_mem, w_q=w_q, w_kv=w_kv, w_o=w_o, g_mlp=g_mlp, w_up=w_up, w_down=w_down)
    lw = {k: (v.astype(BF16) if k in _MATMUL_WEIGHTS else v) for k, v in lw.items()}
    depth = g_mix.shape[0]
    y_prompt = _trunk(x_prompt, mem_prompt, lw, g_final, depth)
    y_sample = _trunk(x_sample, mem_sample, lw, g_final, depth)
    return (y_prompt, y_sample)
```

```python
import functools
import math

import numpy as np
import jax
import jax.numpy as jnp
from jax import lax
from jax.experimental import pallas as pl
from jax.experimental.pallas import tpu as pltpu

F32 = jnp.float32
BF16 = jnp.bfloat16
U32 = jnp.uint32

D_MODEL = 2048
HY_CH = 1024
POOL_CH = 1024
D_IN = 3 * HY_CH + POOL_CH
POOL_WINDOWS = (2, 4, 8, 16)
POOL_GROUP = POOL_CH // len(POOL_WINDOWS)
POOL_HALO = max(POOL_WINDOWS) // 2
POS_BANDS = 16
FILT_HIDDEN = 64
N_MEM = 256
XA_HEADS = 4
XA_HEAD_DIM = D_MODEL // XA_HEADS
D_FF = 4 * D_MODEL
EPS = 1e-6

LANES = 128
SUBLANES = 8
VMEM_LIMIT_BYTES = 58 * 1024 * 1024
NORM_CHUNK_ROWS = 256
FILT_LANES = 1024
HI = lax.Precision.HIGHEST


def _cparams(*sem):
    return pltpu.CompilerParams(dimension_semantics=sem, vmem_limit_bytes=VMEM_LIMIT_BYTES)


def _rms(x, g):
    return x * lax.rsqrt(jnp.mean(x * x, axis=-1, keepdims=True) + EPS) * g


def _const_spec(shape):
    nd = len(shape)
    return pl.BlockSpec(shape, lambda *_: (0,) * nd)


def _norm_matmul_body(x_ref, g_ref, w_ref, o_ref, h_ref):
    rows = NORM_CHUNK_ROWS

    @pl.when(pl.program_id(1) == 0)
    def _():
        for c in range(x_ref.shape[0] // rows):
            sl = slice(c * rows, (c + 1) * rows)
            h = _rms(x_ref[sl, :], g_ref[...]).astype(BF16)
            h_ref[sl, :] = h
            o_ref[sl, :] = jnp.dot(h, w_ref[...], preferred_element_type=F32).astype(o_ref.dtype)

    @pl.when(pl.program_id(1) != 0)
    def _():
        o_ref[...] = jnp.dot(h_ref[...], w_ref[...], preferred_element_type=F32).astype(o_ref.dtype)


def _norm_matmul(x, g, w, layer, out_dtype, tm, tn, name):
    t, d = x.shape
    n = w.shape[2]
    return pl.pallas_call(
        _norm_matmul_body,
        grid=(t // tm, n // tn),
        in_specs=[pl.BlockSpec((tm, d), lambda i, j: (i, 0)),
                  pl.BlockSpec((1, d), lambda i, j: (0, 0)),
                  pl.BlockSpec((None, d, tn), lambda i, j: (layer, 0, j))],
        out_specs=pl.BlockSpec((tm, tn), lambda i, j: (i, j)),
        out_shape=jax.ShapeDtypeStruct((t, n), out_dtype),
        scratch_shapes=[pltpu.VMEM((tm, d), BF16)],
        compiler_params=_cparams("parallel", "arbitrary"),
        name=name,
    )(x, g.reshape(1, d), w)


def _mlp_body(x_ref, g_ref, wu_ref, wd_ref, gf_ref, o_ref, h_ref, *, final_norm):
    f = pl.program_id(1)
    rows = NORM_CHUNK_ROWS

    def up_down(h):
        hm = jnp.dot(h, wu_ref[...], preferred_element_type=F32)
        a = jnp.square(jnp.maximum(hm, 0.0)).astype(BF16)
        return jnp.dot(a, wd_ref[...], preferred_element_type=F32)

    @pl.when(f == 0)
    def _():
        for c in range(x_ref.shape[0] // rows):
            sl = slice(c * rows, (c + 1) * rows)
            x = x_ref[sl, :]
            h = _rms(x, g_ref[...]).astype(BF16)
            h_ref[sl, :] = h
            o_ref[sl, :] = x + up_down(h)

    @pl.when(f != 0)
    def _():
        o_ref[...] += up_down(h_ref[...])

    if final_norm:
        @pl.when(f == pl.num_programs(1) - 1)
        def _():
            o_ref[...] = _rms(o_ref[...], gf_ref[...])


def _mlp(x, g, wu, wd, layer, g_final, final_norm, tm=512, tf=1024):
    t, d = x.shape
    ff = wu.shape[2]
    return pl.pallas_call(
        functools.partial(_mlp_body, final_norm=final_norm),
        grid=(t // tm, ff // tf),
        in_specs=[pl.BlockSpec((tm, d), lambda i, f: (i, 0)),
                  pl.BlockSpec((1, d), lambda i, f: (0, 0)),
                  pl.BlockSpec((None, d, tf), lambda i, f: (layer, 0, f)),
                  pl.BlockSpec((None, tf, d), lambda i, f: (layer, f, 0)),
                  pl.BlockSpec((1, d), lambda i, f: (0, 0))],
        out_specs=pl.BlockSpec((tm, d), lambda i, f: (i, 0)),
        out_shape=jax.ShapeDtypeStruct((t, d), F32),
        scratch_shapes=[pltpu.VMEM((tm, d), BF16)],
        compiler_params=_cparams("parallel", "arbitrary"),
        name="mlp",
    )(x, g.reshape(1, d), wu, wd, g_final.reshape(1, d))


def _attn_body(x_ref, k_ref, v_ref, g_ref, wq_ref, wo_ref, o_ref):
    scale = XA_HEAD_DIM ** -0.5
    x = x_ref[0]
    h = _rms(x, g_ref[...]).astype(BF16)
    q = jnp.dot(h, wq_ref[...], preferred_element_type=F32)
    heads = []
    for hd in range(XA_HEADS):
        sl = slice(hd * XA_HEAD_DIM, (hd + 1) * XA_HEAD_DIM)
        s = lax.dot_general(q[:, sl].astype(BF16), k_ref[0, :, sl], (((1,), (1,)), ((), ())),
                            preferred_element_type=F32) * scale
        p = jnp.exp(s - jnp.max(s, axis=-1, keepdims=True))
        a = (p / jnp.sum(p, axis=-1, keepdims=True)).astype(BF16)
        heads.append(jnp.dot(a, v_ref[0, :, sl], preferred_element_type=F32).astype(BF16))
    o = jnp.concatenate(heads, axis=-1)
    o_ref[0] = x + jnp.dot(o, wo_ref[...], preferred_element_type=F32)


def _attn(x, kv, g, wq, wo, layer, tq=512):
    b, l, d = x.shape
    return pl.pallas_call(
        _attn_body,
        grid=(b, l // tq),
        in_specs=[pl.BlockSpec((1, tq, d), lambda i, j: (i, j, 0)),
                  pl.BlockSpec((1, N_MEM, d), lambda i, j: (i, 0, 0)),
                  pl.BlockSpec((1, N_MEM, d), lambda i, j: (i, 0, 1)),
                  pl.BlockSpec((1, d), lambda i, j: (0, 0)),
                  pl.BlockSpec((None, d, d), lambda i, j: (layer, 0, 0)),
                  pl.BlockSpec((None, d, d), lambda i, j: (layer, 0, 0))],
        out_specs=pl.BlockSpec((1, tq, d), lambda i, j: (i, j, 0)),
        out_shape=jax.ShapeDtypeStruct((b, l, d), F32),
        compiler_params=_cparams("parallel", "arbitrary"),
        name="attn",
    )(x, kv, kv, g.reshape(1, d), wq, wo)


def _outproj_body(x_ref, yh_ref, yp_ref, gh_ref, wt_ref, wb_ref, o_ref):
    yh = _rms(yh_ref[...], gh_ref[...]).astype(BF16)
    acc = jnp.dot(yh, wt_ref[...], preferred_element_type=F32)
    acc += jnp.dot(yp_ref[...], wb_ref[...], preferred_element_type=F32)
    o_ref[...] = x_ref[...] + acc


def _outproj(x, yh, yp, g_hy, w_out, layer, tm=512):
    t, d = x.shape
    return pl.pallas_call(
        _outproj_body,
        grid=(t // tm,),
        in_specs=[pl.BlockSpec((tm, d), lambda i: (i, 0)),
                  pl.BlockSpec((tm, HY_CH), lambda i: (i, 0)),
                  pl.BlockSpec((tm, POOL_CH), lambda i: (i, 0)),
                  pl.BlockSpec((1, HY_CH), lambda i: (0, 0)),
                  pl.BlockSpec((None, HY_CH, d), lambda i: (layer, 0, 0)),
                  pl.BlockSpec((None, POOL_CH, d), lambda i: (layer, 1, 0))],
        out_specs=pl.BlockSpec((tm, d), lambda i: (i, 0)),
        out_shape=jax.ShapeDtypeStruct((t, d), F32),
        compiler_params=_cparams("parallel"),
        name="outproj",
    )(x, yh, yp, g_hy.reshape(1, HY_CH), w_out, w_out)


def _shift_rows(x, down):
    r = x.shape[-2]
    rows = lax.broadcasted_iota(jnp.int32, x.shape, x.ndim - 2)
    if down:
        return jnp.where(rows == 0, 0.0, pltpu.roll(x, 1, axis=x.ndim - 2))
    return jnp.where(rows == r - 1, 0.0, pltpu.roll(x, r - 1, axis=x.ndim - 2))


def _wrap_prev(x, is_first):
    return jnp.where(is_first, _shift_rows(x, True), x)


def _wrap_next(x, is_last):
    return jnp.where(is_last, _shift_rows(x, False), x)


def _short_conv_plane(jj, main_ref, prev_ref, next_ref, w_ref, b_ref, is_first, is_last):
    tj = main_ref.shape[1]
    lo = main_ref[:, jj - 1] if jj > 0 else _wrap_prev(prev_ref[:, 0], is_first)
    hi = main_ref[:, jj + 1] if jj < tj - 1 else _wrap_next(next_ref[:, 0], is_last)
    return lo * w_ref[0:1, :] + main_ref[:, jj] * w_ref[1:2, :] + hi * w_ref[2:3, :] + b_ref[...]


def _pack(re, im):
    half = jnp.uint32(0x8000)
    rb = (lax.bitcast_convert_type(re, U32) + half) & jnp.uint32(0xFFFF0000)
    return rb | ((lax.bitcast_convert_type(im, U32) + half) >> 16)


def _unpack(p):
    return (lax.bitcast_convert_type(p & jnp.uint32(0xFFFF0000), F32),
            lax.bitcast_convert_type(p << 16, F32))


def _split_ri(x):
    m, c = x.shape
    x4 = x.reshape(m // (2 * SUBLANES), 2, SUBLANES, c)
    return x4[:, 0].reshape(m // 2, c), x4[:, 1].reshape(m // 2, c)


def _join_ri(re, im):
    n, c = re.shape
    g = n // SUBLANES
    return jnp.stack([re.reshape(g, SUBLANES, c), im.reshape(g, SUBLANES, c)], axis=1).reshape(2 * n, c)


def _store_cols(scr, n2i, n1, pitch, val):
    for l in range(scr.shape[0]):
        scr[l, pl.ds(n2i, n1, stride=pitch), :] = val[:, l * LANES:(l + 1) * LANES]


def _load_cols(scr, n2i, n1, pitch):
    return jnp.concatenate([scr[l, pl.ds(n2i, n1, stride=pitch), :] for l in range(scr.shape[0])], axis=1)


def _store_rows(scr, base, n2, val):
    for l in range(scr.shape[0]):
        scr[l, pl.ds(base, n2), :] = val[:, l * LANES:(l + 1) * LANES]


def _load_rows(scr, base, n2):
    return jnp.concatenate([scr[l, pl.ds(base, n2), :] for l in range(scr.shape[0])], axis=1)


def _stage1_to_scratch(x_bf16, w_ref, tw_ref, jj, scr, n2i, n1, pitch):
    res = jnp.dot(w_ref[...], x_bf16, preferred_element_type=F32)
    ar, ai = _split_ri(res)
    twr, twi = tw_ref[0, 0, :, jj:jj + 1], tw_ref[0, 1, :, jj:jj + 1]
    _store_cols(scr, n2i, n1, pitch, _pack(ar * twr - ai * twi, ar * twi + ai * twr))


def _stage2_fwd(scr, base, n2, f_ref):
    xr, xi = _unpack(_load_rows(scr, base, n2))
    s = jnp.dot(f_ref[...], _join_ri(xr, xi).astype(BF16), preferred_element_type=F32)
    return _split_ri(s)


def _lconv_body(*refs, conv, n2, n1, pitch, tj, tk, s1, s2):
    if conv:
        (vm_ref, vp_ref, vn_ref, vcw_ref, vcb_ref, gm_ref, gp_ref, gn_ref, gcw_ref, gcb_ref,
         kf_ref, w_ref, wi_ref, f_ref, g_ref, tw_ref, o_ref, scr) = refs
    else:
        (vm_ref, gm_ref, gp_ref, gn_ref, gcw_ref, gcb_ref,
         kf_ref, w_ref, wi_ref, f_ref, g_ref, tw_ref, o_ref, scr) = refs
    s = pl.program_id(2)

    @pl.when(s < s1)
    def _():
        for jj in range(tj):
            if conv:
                v = _short_conv_plane(jj, vm_ref, vp_ref, vn_ref, vcw_ref, vcb_ref, s == 0, s == s1 - 1)
            else:
                v = vm_ref[:, jj]
            nb, r, ct = v.shape
            _stage1_to_scratch(v.reshape(nb * r, ct).astype(BF16), w_ref, tw_ref, jj, scr,
                               s * tj + jj, n1, pitch)

    @pl.when(jnp.logical_and(s >= s1, s < s1 + s2))
    def _():
        for kk in range(tk):
            base = pl.multiple_of(((s - s1) * tk + kk) * pitch, SUBLANES)
            sr, si = _stage2_fwd(scr, base, n2, f_ref)
            kr, ki = _unpack(kf_ref[0, kk])
            y = _join_ri(sr * kr - si * ki, sr * ki + si * kr).astype(BF16)
            a = jnp.dot(g_ref[...], y, preferred_element_type=F32)
            _store_rows(scr, base, n2, _pack(*_split_ri(a)))

    @pl.when(s >= s1 + s2)
    def _():
        jp = s - s1 - s2
        for jj in range(tj):
            gate = _short_conv_plane(jj, gm_ref, gp_ref, gn_ref, gcw_ref, gcb_ref, jp == 0, jp == s1 - 1)
            ar, ai = _unpack(_load_cols(scr, jp * tj + jj, n1, pitch))
            twr, twi = tw_ref[0, 0, :, jj:jj + 1], tw_ref[0, 1, :, jj:jj + 1]
            x = _join_ri(ar * twr + ai * twi, ai * twr - ar * twi).astype(BF16)
            y = jnp.dot(wi_ref[...], x, preferred_element_type=F32)
            o_ref[:, jj] = (gate * y.reshape(gate.shape)).astype(o_ref.dtype)


def _lconv(vsrc, v_goff, v_cw, v_cb, u, g_goff, g_cw, g_cb, kf, order, tabs, cfg, out_dtype, name):
    n2, r, n1, tj, tk, ct, pitch = (cfg[k] for k in ("N2", "R", "N1", "tj", "tk", "ct", "pitch"))
    p = vsrc.shape[0] // 2
    s1, s2 = n2 // tj, n1 // tk
    conv = v_cw is not None

    def jv(s):
        return jnp.minimum(s, s1 - 1)

    def jg(s):
        return jnp.maximum(s - s1 - s2, 0)

    def jt(s):
        return jnp.where(s < s1, s, jg(s))

    def plane_specs(goff, jf):
        return [pl.BlockSpec((2, tj, r, ct), lambda p_, c, s: (p_, jf(s), 0, goff + c)),
                pl.BlockSpec((2, 1, r, ct), lambda p_, c, s: (p_, (jf(s) * tj + n2 - 1) % n2, 0, goff + c)),
                pl.BlockSpec((2, 1, r, ct), lambda p_, c, s: (p_, ((jf(s) + 1) * tj) % n2, 0, goff + c))]

    wspecs = [pl.BlockSpec((3, ct), lambda p_, c, s: (0, c)), pl.BlockSpec((1, ct), lambda p_, c, s: (0, c))]
    if conv:
        in_specs = plane_specs(v_goff, jv) + wspecs
        args = [vsrc, vsrc, vsrc, v_cw, v_cb]
    else:
        in_specs = plane_specs(v_goff, jv)[:1]
        args = [vsrc]
    in_specs += plane_specs(g_goff, jg) + wspecs
    args += [u, u, u, g_cw, g_cb]
    in_specs += [pl.BlockSpec((1, tk, n2, ct), lambda p_, c, s: (order, jnp.clip(s - s1, 0, s2 - 1), 0, c)),
                 _const_spec((2 * n1, 2 * r)), _const_spec((2 * r, 2 * n1)),
                 _const_spec((2 * n2, 2 * n2)), _const_spec((2 * n2, 2 * n2)),
                 pl.BlockSpec((1, 2, n1, tj), lambda p_, c, s: (jt(s), 0, 0, 0))]
    args += [kf, tabs["w1"], tabs["w1inv"], tabs["f2"], tabs["g2"], tabs["tw"]]
    return pl.pallas_call(
        functools.partial(_lconv_body, conv=conv, n2=n2, n1=n1, pitch=pitch, tj=tj, tk=tk, s1=s1, s2=s2),
        grid=(p, HY_CH // ct, 2 * s1 + s2),
        in_specs=in_specs,
        out_specs=pl.BlockSpec((2, tj, r, ct), lambda p_, c, s: (p_, jg(s), 0, c)),
        out_shape=jax.ShapeDtypeStruct((2 * p, n2, r, HY_CH), out_dtype),
        scratch_shapes=[pltpu.VMEM((ct // LANES, n1 * pitch, LANES), U32)],
        compiler_params=_cparams("parallel", "arbitrary", "arbitrary"),
        name=name,
    )(*args)


def _filt_pos_half(shape, n2, n2_total, d):
    row = lax.broadcasted_iota(jnp.int32, shape, 0)
    if d == 0:
        return (row * n2_total + n2).astype(F32)
    pos_b = (shape[0] - row) * n2_total - n2
    return jnp.where(jnp.logical_and(row == 0, n2 == 0), 0, pos_b).astype(F32)


def _filt_hidden_body(bands_ref, w1t_ref, w1c_ref, w1s_ref, b1_ref, w2_ref, b2_ref, fr_ref, h_ref,
                      *, seq_len, n2_total):
    tjh, n1 = h_ref.shape[0], h_ref.shape[1]
    lanes = tjh * n1
    idx = lax.broadcasted_iota(jnp.int32, (1, lanes), 1)
    row = idx & (n1 - 1)
    n2 = pl.program_id(0) * tjh + (idx >> int(math.log2(n1)))
    pos_b = jnp.where(jnp.logical_and(row == n1 // 2, n2 == 0), 0, (n1 - row) * n2_total - n2)
    pos = jnp.where(row < n1 // 2, row * n2_total + n2, pos_b).astype(F32)
    t = pos / float(max(seq_len - 1, 1))
    ang = bands_ref[...] * ((2.0 * math.pi / seq_len) * pos)
    pre = (w1t_ref[...] * t + jnp.dot(w1c_ref[...], jnp.cos(ang), precision=HI, preferred_element_type=F32)
           + jnp.dot(w1s_ref[...], -jnp.sin(ang), precision=HI, preferred_element_type=F32) + b1_ref[...])
    h = jnp.sin(fr_ref[...] * pre)
    h = jnp.sin(fr_ref[...] * (jnp.dot(w2_ref[...], h, precision=HI, preferred_element_type=F32) + b2_ref[...]))
    h = h.T
    hi = h.astype(BF16)
    lo = (h - hi.astype(F32)).astype(BF16)
    h_ref[...] = jnp.concatenate([hi, hi, lo], axis=1).reshape(h_ref.shape)


def _filt_hidden(w1, b1, w2, b2, freq, cfg, seq_len):
    n2, n1 = cfg["N2"], cfg["N1"]
    tjh = FILT_LANES // n1
    bands = np.linspace(1e-4, POS_BANDS - 1, POS_BANDS, dtype=np.float32).reshape(POS_BANDS, 1)
    col = lambda v: v.reshape(FILT_HIDDEN, 1)
    args = (jnp.asarray(bands), col(w1[0]), w1[1:1 + POS_BANDS].T, w1[1 + POS_BANDS:].T, col(b1), w2.T, col(b2),
            col(freq))
    return pl.pallas_call(
        functools.partial(_filt_hidden_body, seq_len=seq_len, n2_total=n2),
        grid=(n2 // tjh,),
        in_specs=[_const_spec(a.shape) for a in args],
        out_specs=pl.BlockSpec((tjh, n1, 3 * FILT_HIDDEN), lambda j: (j, 0, 0)),
        out_shape=jax.ShapeDtypeStruct((n2, n1, 3 * FILT_HIDDEN), BF16),
        compiler_params=_cparams("parallel"),
        name="filt_hidden",
    )(*args)


def _filt_spec_body(h_ref, w3_ref, dec_ref, bias_ref, wf_ref, f_ref, tw_ref, o_ref, scr, nrm_ref,
                    *, seq_len, n2, n1, pitch, tj, tk, s1):
    s = pl.program_id(2)
    r = n1 // 2
    ct = o_ref.shape[-1]

    @pl.when(s == 0)
    def _():
        nrm_ref[...] = jnp.zeros_like(nrm_ref)

    @pl.when(s < s1)
    def _():
        row = lax.broadcasted_iota(jnp.int32, (r, ct), 0)
        for jj in range(tj):
            n2i = s * tj + jj
            halves = []
            for d in range(2):
                val = jnp.dot(h_ref[jj, d * r:(d + 1) * r, :], w3_ref[0, d], preferred_element_type=F32)
                t = _filt_pos_half((r, ct), n2i, n2, d) / float(max(seq_len - 1, 1))
                val = val * jnp.exp(-t * jnp.abs(dec_ref[0, d:d + 1, :]))
                nrm_ref[...] += jnp.sum(jnp.abs(val), axis=0, keepdims=True)
                if d == 1:
                    val = jnp.where(jnp.logical_and(row == 0, n2i == 0), 0.0, val)
                halves.append(val)
            _stage1_to_scratch(jnp.concatenate(halves, axis=0).astype(BF16), wf_ref, tw_ref, jj, scr,
                               n2i, n1, pitch)

    @pl.when(s >= s1)
    def _():
        inv = 1.0 / nrm_ref[...]
        bias = bias_ref[0]
        for kk in range(tk):
            base = pl.multiple_of(((s - s1) * tk + kk) * pitch, SUBLANES)
            sr, si = _stage2_fwd(scr, base, n2, f_ref)
            o_ref[0, kk] = _pack(sr * inv + bias, si * inv)


def _filt_spec(hid, w3, dec, bias, tabs, cfg, seq_len):
    n2, n1, tj, tk, ct, pitch = (cfg[k] for k in ("N2", "N1", "tj", "tk", "ct", "pitch"))
    s1, s2 = n2 // tj, n1 // tk
    return pl.pallas_call(
        functools.partial(_filt_spec_body, seq_len=seq_len, n2=n2, n1=n1, pitch=pitch, tj=tj, tk=tk, s1=s1),
        grid=(2, HY_CH // ct, s1 + s2),
        in_specs=[pl.BlockSpec((tj, n1, 3 * FILT_HIDDEN), lambda o, c, s: (jnp.minimum(s, s1 - 1), 0, 0)),
                  pl.BlockSpec((1, 2, 3 * FILT_HIDDEN, ct), lambda o, c, s: (o, 0, 0, c)),
                  pl.BlockSpec((1, 2, ct), lambda o, c, s: (o, 0, c)),
                  pl.BlockSpec((1, 1, ct), lambda o, c, s: (o, 0, c)),
                  _const_spec((2 * n1, n1)), _const_spec((2 * n2, 2 * n2)),
                  pl.BlockSpec((1, 2, n1, tj), lambda o, c, s: (jnp.minimum(s, s1 - 1), 0, 0, 0))],
        out_specs=pl.BlockSpec((1, tk, n2, ct), lambda o, c, s: (o, jnp.maximum(s - s1, 0), 0, c)),
        out_shape=jax.ShapeDtypeStruct((2, n1, n2, HY_CH), U32),
        scratch_shapes=[pltpu.VMEM((ct // LANES, n1 * pitch, LANES), U32), pltpu.VMEM((1, ct), F32)],
        compiler_params=_cparams("parallel", "arbitrary", "arbitrary"),
        name="filt_spec",
    )(hid, w3, dec, bias, tabs["wf"], tabs["f2"], tabs["tw"])


def _pool_body(m_ref, p_ref, n_ref, pw_ref, ps_ref, gp_ref, o_ref, *, seq_len, n2_total):
    tj, r = m_ref.shape[1], m_ref.shape[2]
    j = pl.program_id(1)
    first, last = j == 0, j == pl.num_programs(1) - 1
    h = POOL_HALO
    j0 = j * tj
    plane = lax.broadcasted_iota(jnp.int32, (tj, r, LANES), 0)
    row = lax.broadcasted_iota(jnp.int32, (tj, r, LANES), 1)
    t = row * n2_total + j0 + plane
    outs = []
    for g, win in enumerate(POOL_WINDOWS):
        sl = slice(g * POOL_GROUP, (g + 1) * POOL_GROUP)
        half = win // 2
        main = m_ref[0, :, :, sl]
        parts = [_wrap_prev(p_ref[0, h - half:h, :, sl], first), main]
        if half > 1:
            parts.append(_wrap_next(n_ref[0, 0:half - 1, :, sl], last))
        acc = jnp.concatenate(parts, axis=0)
        width = 1
        while width < win:
            n = acc.shape[0] - width
            acc = acc[:n] + acc[width:width + n]
            width *= 2
        cnt = (jnp.minimum(t + half - 1, seq_len - 1) - jnp.maximum(t - half, 0) + 1).astype(F32)
        dgrp = acc / jnp.concatenate([cnt] * (POOL_GROUP // LANES), axis=-1) - main
        outs.append(jnp.dot(dgrp.reshape(tj * r, POOL_GROUP).astype(BF16), pw_ref[g],
                            preferred_element_type=F32))
    y = jnp.concatenate(outs, axis=-1) * ps_ref[...]
    o_ref[0] = _rms(y, gp_ref[...]).astype(BF16).reshape(tj, r, POOL_CH)


def _pool(u, pool_w, pool_scale, g_pool, cfg, seq_len):
    n2, r, tj = cfg["N2"], cfg["R"], cfg["tj_pool"]
    b = u.shape[0]
    goff = 3 * HY_CH // POOL_CH
    h = POOL_HALO
    nh = n2 // h
    return pl.pallas_call(
        functools.partial(_pool_body, seq_len=seq_len, n2_total=n2),
        grid=(b, n2 // tj),
        in_specs=[pl.BlockSpec((1, tj, r, POOL_CH), lambda i, j: (i, j, 0, goff)),
                  pl.BlockSpec((1, h, r, POOL_CH), lambda i, j: (i, (j * (tj // h) + nh - 1) % nh, 0, goff)),
                  pl.BlockSpec((1, h, r, POOL_CH), lambda i, j: (i, ((j + 1) * (tj // h)) % nh, 0, goff)),
                  _const_spec((len(POOL_WINDOWS), POOL_GROUP, POOL_GROUP)),
                  _const_spec((1, POOL_CH)), _const_spec((1, POOL_CH))],
        out_specs=pl.BlockSpec((1, tj, r, POOL_CH), lambda i, j: (i, j, 0, 0)),
        out_shape=jax.ShapeDtypeStruct((b, n2, r, POOL_CH), BF16),
        compiler_params=_cparams("parallel", "arbitrary"),
        name="pool",
    )(u, u, u, pool_w, pool_scale, g_pool)


def _fft_cfg(seq_len):
    n2 = 128 if seq_len >= 8192 else 64
    r = seq_len // n2
    assert r % SUBLANES == 0 and r * n2 == seq_len
    tj = 8 if seq_len >= 8192 else 32
    return dict(N2=n2, R=r, N1=2 * r, tj=tj, tk=tj, ct=256, pitch=n2 + SUBLANES, tj_pool=8)


def _dft_tables(cfg):
    n2, r, n1, tj = cfg["N2"], cfg["R"], cfg["N1"], cfg["tj"]
    n = n1 * n2

    def cis(idx, period):
        a = -2.0 * np.pi * (idx % period) / period
        return np.cos(a), np.sin(a)

    def ri_order(m):
        pos = np.arange(2 * m)
        return (pos % (2 * SUBLANES)) // SUBLANES * m + pos // (2 * SUBLANES) * SUBLANES + pos % SUBLANES

    k1 = np.arange(n1)
    mr, mi = cis(k1[:, None] * k1[None, :], n1)
    mrd, mid = mr[:, :r], mi[:, :r]
    o1, o2 = ri_order(n1), ri_order(n2)
    w1 = np.block([[mrd, -mid], [mid, mrd]])[o1]
    wf = np.concatenate([mr, mi], axis=0)[o1]
    k2 = np.arange(n2)
    fr, fi = cis(k2[:, None] * k2[None, :], n2)
    f2 = np.block([[fr, -fi], [fi, fr]])[o2][:, o2]
    g2 = np.block([[fr, fi], [-fi, fr]])[o2][:, o2]
    twr, twi = cis(k1[:, None] * k2[None, :], n)
    tw = np.stack([twr, twi]).reshape(2, n1, n2 // tj, tj).transpose(2, 0, 1, 3)
    return dict(w1=jnp.asarray(w1, BF16), w1inv=jnp.asarray(w1.T / n, BF16), wf=jnp.asarray(wf, BF16),
                f2=jnp.asarray(f2, BF16), g2=jnp.asarray(g2, BF16), tw=jnp.asarray(tw, F32))


def _to_plane_order(x, cfg):
    b, l, d = x.shape
    return jnp.swapaxes(x.reshape(b, cfg["R"], cfg["N2"], d), 1, 2).reshape(b * l, d)


def _from_plane_order(x, b, cfg):
    d = x.shape[-1]
    return jnp.swapaxes(x.reshape(b, cfg["N2"], cfg["R"], d), 1, 2).reshape(b, cfg["N2"] * cfg["R"], d)


def _trunk(x, mem, lw, g_final, depth):
    b, seq_len, d = x.shape
    cfg = _fft_cfg(seq_len)
    tabs = _dft_tables(cfg)
    n2, r, ct = cfg["N2"], cfg["R"], cfg["ct"]
    xs = _to_plane_order(x, cfg)
    memf = mem.reshape(b * N_MEM, d)
    cpb = HY_CH // ct
    for i in range(depth):
        w = {k: v[i] for k, v in lw.items() if k not in _STACKED_WEIGHTS}
        hid = _filt_hidden(w["filt_w1"], w["filt_b1"], w["filt_w2"], w["filt_b2"], w["filt_freq"], cfg, seq_len)
        w3 = jnp.transpose(w["filt_w3"].reshape(FILT_HIDDEN, 2, 2, HY_CH), (1, 2, 0, 3))
        w3hi = w3.astype(BF16)
        w3lo = (w3 - w3hi.astype(F32)).astype(BF16)
        kf = _filt_spec(hid, jnp.concatenate([w3hi, w3lo, w3hi], axis=2),
                        w["hy_decay"], w["hy_bias"].reshape(2, 1, HY_CH), tabs, cfg, seq_len)
        u = _norm_matmul(xs, w["g_mix"], lw["w_in"], i, F32, 1024, 1024, "in_proj").reshape(b, n2, r, D_IN)
        cw, cbias = w["conv_w"], w["conv_b"].reshape(1, -1)
        cws = [cw[:, k * HY_CH:(k + 1) * HY_CH] for k in range(3)]
        cbs = [cbias[:, k * HY_CH:(k + 1) * HY_CH] for k in range(3)]
        z1 = _lconv(u, 2 * cpb, cws[2], cbs[2], u, 0, cws[0], cbs[0], kf, 0, tabs, cfg, BF16, "lconv1")
        yh = _lconv(z1, 0, None, None, u, cpb, cws[1], cbs[1], kf, 1, tabs, cfg, F32, "lconv2")
        yp = _pool(u, w["pool_w"], w["pool_scale"].reshape(1, -1), w["g_pool"].reshape(1, -1), cfg, seq_len)
        xs = _outproj(xs, yh.reshape(b * seq_len, HY_CH), yp.reshape(b * seq_len, POOL_CH), w["g_hy"],
                      lw["w_out"], i)
        kv = _norm_matmul(memf, w["g_mem"], lw["w_kv"], i, BF16, 512, 1024, "kv_proj").reshape(b, N_MEM, 2 * d)
        xs = _attn(xs.reshape(b, seq_len, d), kv, w["g_xa"], lw["w_q"], lw["w_o"], i).reshape(b * seq_len, d)
        xs = _mlp(xs, w["g_mlp"], lw["w_up"], lw["w_down"], i, g_final, i == depth - 1)
    return _from_plane_order(xs, b, cfg)


_STACKED_WEIGHTS = ("w_in", "w_out", "w_q", "w_kv", "w_o", "w_up", "w_down")
_MATMUL_WEIGHTS = _STACKED_WEIGHTS + ("pool_w",)


def kernel(x_prompt, x_sample, mem_prompt, mem_sample, g_mix, w_in, conv_w, conv_b, filt_w1, filt_b1, filt_w2, filt_b2, filt_freq, filt_w3, hy_decay, hy_bias, pool_w, pool_scale, g_hy, g_pool, w_out, g_xa, g_mem, w_q, w_kv, w_o, g_mlp, w_up, w_down, g_final):
    lw = dict(g_mix=g_mix, w_in=w_in, conv_w=conv_w, conv_b=conv_b, filt_w1=filt_w1, filt_b1=filt_b1,
              filt_w2=filt_w2, filt_b2=filt_b2, filt_freq=filt_freq, filt_w3=filt_w3, hy_decay=hy_decay,
              hy_bias=hy_bias, pool_w=pool_w, pool_scale=pool_scale, g_hy=g_hy, g_pool=g_pool, w_out=w_out,
              g_xa=g_xa, g_mem=g_mem, w_q=w_q, w_kv=w_kv, w_o=w_o, g_mlp=g_mlp, w_up=w_up, w_down=w_down)
    lw = {k: (v.astype(BF16) if k in _MATMUL_WEIGHTS else v) for k, v in lw.items()}
    depth = g_mix.shape[0]
    y_prompt = _trunk(x_prompt, mem_prompt, lw, g_final, depth)
    y_sample = _trunk(x_sample, mem_sample, lw, g_final, depth)
    return (y_prompt, y_sample)
```

```python
import functools
import math

import numpy as np
import jax
import jax.numpy as jnp
from jax import lax
from jax.experimental import pallas as pl
from jax.experimental.pallas import tpu as pltpu

F32 = jnp.float32
BF16 = jnp.bfloat16
U32 = jnp.uint32

D_MODEL = 2048
HY_CH = 1024
POOL_CH = 1024
D_IN = 3 * HY_CH + POOL_CH
POOL_WINDOWS = (2, 4, 8, 16)
POOL_GROUP = POOL_CH // len(POOL_WINDOWS)
POS_BANDS = 16
FILT_HIDDEN = 64
N_MEM = 256
XA_HEADS = 4
XA_HEAD_DIM = D_MODEL // XA_HEADS
D_FF = 4 * D_MODEL
EPS = 1e-6

LANES = 128
SUBLANES = 8
VMEM_LIMIT_BYTES = 58 * 1024 * 1024
NORM_CHUNK_ROWS = 256
FILT_LANES = 1024
IN_PROJ_TOKENS = 1024
HY_CONV_BLOCKS = 3
HI = lax.Precision.HIGHEST


def _cparams(*sem):
    return pltpu.CompilerParams(dimension_semantics=sem, vmem_limit_bytes=VMEM_LIMIT_BYTES)


def _rms(x, g):
    return x * lax.rsqrt(jnp.mean(x * x, axis=-1, keepdims=True) + EPS) * g


def _const_spec(shape):
    nd = len(shape)
    return pl.BlockSpec(shape, lambda *_: (0,) * nd)


def _norm_matmul_body(x_ref, g_ref, w_ref, o_ref, h_ref):
    rows = NORM_CHUNK_ROWS

    @pl.when(pl.program_id(1) == 0)
    def _():
        for c in range(x_ref.shape[0] // rows):
            sl = slice(c * rows, (c + 1) * rows)
            h = _rms(x_ref[sl, :], g_ref[...]).astype(BF16)
            h_ref[sl, :] = h
            o_ref[sl, :] = jnp.dot(h, w_ref[...], preferred_element_type=F32).astype(o_ref.dtype)

    @pl.when(pl.program_id(1) != 0)
    def _():
        o_ref[...] = jnp.dot(h_ref[...], w_ref[...], preferred_element_type=F32).astype(o_ref.dtype)


def _norm_matmul(x, g, w, layer, out_dtype, tm, tn, name):
    t, d = x.shape
    n = w.shape[2]
    return pl.pallas_call(
        _norm_matmul_body,
        grid=(t // tm, n // tn),
        in_specs=[pl.BlockSpec((tm, d), lambda i, j: (i, 0)),
                  pl.BlockSpec((1, d), lambda i, j: (0, 0)),
                  pl.BlockSpec((None, d, tn), lambda i, j: (layer, 0, j))],
        out_specs=pl.BlockSpec((tm, tn), lambda i, j: (i, j)),
        out_shape=jax.ShapeDtypeStruct((t, n), out_dtype),
        scratch_shapes=[pltpu.VMEM((tm, d), BF16)],
        compiler_params=_cparams("parallel", "arbitrary"),
        name=name,
    )(x, g.reshape(1, d), w)


def _in_proj_body(x_ref, xp_ref, xn_ref, g_ref, w_ref, cw_ref, cb_ref, o_ref, h_ref, hh_ref, u_ref):
    gi, j = pl.program_id(1), pl.program_id(2)
    n2, rt, d = x_ref.shape[1:]
    tn = w_ref.shape[1]
    rows = NORM_CHUNK_ROWS
    ppc = rows // rt

    def emit(first, conv):
        if conv:
            w0, w1, w2, b = cw_ref[0:1, :], cw_ref[1:2, :], cw_ref[2:3, :], cb_ref[...]
        if first:
            hh_ref[...] = _rms(jnp.concatenate([xp_ref[0, 0], xn_ref[0, 0]], axis=0), g_ref[...]).astype(BF16)
        for c in range(n2 // ppc):
            if first:
                h = _rms(x_ref[0, c * ppc:(c + 1) * ppc].reshape(rows, d), g_ref[...]).astype(BF16)
                h_ref[c * rows:(c + 1) * rows, :] = h
            else:
                h = h_ref[c * rows:(c + 1) * rows, :]
            uc = jnp.dot(h, w_ref[...], preferred_element_type=F32).reshape(ppc, rt, tn)
            if not conv:
                o_ref[0, c * ppc:(c + 1) * ppc] = uc
                continue
            u_ref[c * ppc:(c + 1) * ppc] = uc
            lo, hi = max(c * ppc - 1, 1), min((c + 1) * ppc - 1, n2 - 1)
            o_ref[0, lo:hi] = u_ref[lo - 1:hi - 1] * w0 + u_ref[lo:hi] * w1 + u_ref[lo + 1:hi + 1] * w2 + b
        if conv:
            uh = jnp.dot(hh_ref[...], w_ref[...], preferred_element_type=F32)
            row = lax.broadcasted_iota(jnp.int32, (rt, tn), 0)
            prev_tok = jnp.where(gi == 0, 0.0, uh[SUBLANES - 1:SUBLANES])
            next_tok = jnp.where(gi == pl.num_programs(1) - 1, 0.0, uh[SUBLANES:SUBLANES + 1])
            wrap_prev = jnp.where(row == 0, prev_tok, pltpu.roll(u_ref[n2 - 1], 1, axis=0))
            wrap_next = jnp.where(row == rt - 1, next_tok, pltpu.roll(u_ref[0], rt - 1, axis=0))
            o_ref[0, 0] = wrap_prev * w0 + u_ref[0] * w1 + u_ref[1] * w2 + b
            o_ref[0, n2 - 1] = u_ref[n2 - 2] * w0 + u_ref[n2 - 1] * w1 + wrap_next * w2 + b

    pl.when(j == 0)(lambda: emit(True, True))
    pl.when(jnp.logical_and(j > 0, j < HY_CONV_BLOCKS))(lambda: emit(False, True))
    pl.when(j >= HY_CONV_BLOCKS)(lambda: emit(False, False))


def _in_proj(x4, g, w, layer, conv_w, conv_b, cfg):
    b, n2, r, d = x4.shape
    rt = IN_PROJ_TOKENS // n2
    tn = HY_CH
    rg = r // SUBLANES
    return pl.pallas_call(
        _in_proj_body,
        grid=(b, r // rt, D_IN // tn),
        in_specs=[pl.BlockSpec((1, n2, rt, d), lambda i, g_, j: (i, 0, g_, 0)),
                  pl.BlockSpec((1, 1, SUBLANES, d),
                               lambda i, g_, j: (i, n2 - 1, jnp.maximum(g_ * (rt // SUBLANES) - 1, 0), 0)),
                  pl.BlockSpec((1, 1, SUBLANES, d),
                               lambda i, g_, j: (i, 0, jnp.minimum((g_ + 1) * (rt // SUBLANES), rg - 1), 0)),
                  pl.BlockSpec((1, d), lambda i, g_, j: (0, 0)),
                  pl.BlockSpec((None, d, tn), lambda i, g_, j: (layer, 0, j)),
                  pl.BlockSpec((3, tn), lambda i, g_, j: (0, jnp.minimum(j, HY_CONV_BLOCKS - 1))),
                  pl.BlockSpec((1, tn), lambda i, g_, j: (0, jnp.minimum(j, HY_CONV_BLOCKS - 1)))],
        out_specs=pl.BlockSpec((1, n2, rt, tn), lambda i, g_, j: (i, 0, g_, j)),
        out_shape=jax.ShapeDtypeStruct((b, n2, r, D_IN), F32),
        scratch_shapes=[pltpu.VMEM((n2 * rt, d), BF16), pltpu.VMEM((2 * SUBLANES, d), BF16),
                        pltpu.VMEM((n2, rt, tn), F32)],
        compiler_params=_cparams("parallel", "parallel", "arbitrary"),
        name="in_proj",
    )(x4, x4, x4, g.reshape(1, d), w, conv_w, conv_b)


def _mlp_body(x_ref, g_ref, wu_ref, wd_ref, gf_ref, o_ref, h_ref, *, final_norm):
    f = pl.program_id(1)
    rows = NORM_CHUNK_ROWS

    def up_down(h):
        hm = jnp.dot(h, wu_ref[...], preferred_element_type=F32)
        a = jnp.square(jnp.maximum(hm, 0.0)).astype(BF16)
        return jnp.dot(a, wd_ref[...], preferred_element_type=F32)

    @pl.when(f == 0)
    def _():
        for c in range(x_ref.shape[0] // rows):
            sl = slice(c * rows, (c + 1) * rows)
            x = x_ref[sl, :]
            h = _rms(x, g_ref[...]).astype(BF16)
            h_ref[sl, :] = h
            o_ref[sl, :] = x + up_down(h)

    @pl.when(f != 0)
    def _():
        o_ref[...] += up_down(h_ref[...])

    if final_norm:
        @pl.when(f == pl.num_programs(1) - 1)
        def _():
            o_ref[...] = _rms(o_ref[...], gf_ref[...])


def _mlp(x, g, wu, wd, layer, g_final, final_norm, tm=512, tf=1024):
    t, d = x.shape
    ff = wu.shape[2]
    return pl.pallas_call(
        functools.partial(_mlp_body, final_norm=final_norm),
        grid=(t // tm, ff // tf),
        in_specs=[pl.BlockSpec((tm, d), lambda i, f: (i, 0)),
                  pl.BlockSpec((1, d), lambda i, f: (0, 0)),
                  pl.BlockSpec((None, d, tf), lambda i, f: (layer, 0, f)),
                  pl.BlockSpec((None, tf, d), lambda i, f: (layer, f, 0)),
                  pl.BlockSpec((1, d), lambda i, f: (0, 0))],
        out_specs=pl.BlockSpec((tm, d), lambda i, f: (i, 0)),
        out_shape=jax.ShapeDtypeStruct((t, d), F32),
        scratch_shapes=[pltpu.VMEM((tm, d), BF16)],
        compiler_params=_cparams("parallel", "arbitrary"),
        name="mlp",
    )(x, g.reshape(1, d), wu, wd, g_final.reshape(1, d))


def _attn_body(x_ref, k_ref, v_ref, g_ref, wq_ref, wo_ref, o_ref):
    scale = XA_HEAD_DIM ** -0.5
    x = x_ref[0]
    h = _rms(x, g_ref[...]).astype(BF16)
    q = jnp.dot(h, wq_ref[...], preferred_element_type=F32)
    heads = []
    for hd in range(XA_HEADS):
        sl = slice(hd * XA_HEAD_DIM, (hd + 1) * XA_HEAD_DIM)
        s = lax.dot_general(q[:, sl].astype(BF16), k_ref[0, :, sl], (((1,), (1,)), ((), ())),
                            preferred_element_type=F32) * scale
        p = jnp.exp(s - jnp.max(s, axis=-1, keepdims=True))
        a = (p / jnp.sum(p, axis=-1, keepdims=True)).astype(BF16)
        heads.append(jnp.dot(a, v_ref[0, :, sl], preferred_element_type=F32).astype(BF16))
    o = jnp.concatenate(heads, axis=-1)
    o_ref[0] = x + jnp.dot(o, wo_ref[...], preferred_element_type=F32)


def _attn(x, kv, g, wq, wo, layer, tq=512):
    b, l, d = x.shape
    return pl.pallas_call(
        _attn_body,
        grid=(b, l // tq),
        in_specs=[pl.BlockSpec((1, tq, d), lambda i, j: (i, j, 0)),
                  pl.BlockSpec((1, N_MEM, d), lambda i, j: (i, 0, 0)),
                  pl.BlockSpec((1, N_MEM, d), lambda i, j: (i, 0, 1)),
                  pl.BlockSpec((1, d), lambda i, j: (0, 0)),
                  pl.BlockSpec((None, d, d), lambda i, j: (layer, 0, 0)),
                  pl.BlockSpec((None, d, d), lambda i, j: (layer, 0, 0))],
        out_specs=pl.BlockSpec((1, tq, d), lambda i, j: (i, j, 0)),
        out_shape=jax.ShapeDtypeStruct((b, l, d), F32),
        compiler_params=_cparams("parallel", "arbitrary"),
        name="attn",
    )(x, kv, kv, g.reshape(1, d), wq, wo)


def _outproj_body(x_ref, yh_ref, yp_ref, gh_ref, wt_ref, wb_ref, o_ref):
    yh = _rms(yh_ref[...], gh_ref[...]).astype(BF16)
    acc = jnp.dot(yh, wt_ref[...], preferred_element_type=F32)
    acc += jnp.dot(yp_ref[...], wb_ref[...], preferred_element_type=F32)
    o_ref[...] = x_ref[...] + acc


def _outproj(x, yh, yp, g_hy, w_out, layer, tm=512):
    t, d = x.shape
    return pl.pallas_call(
        _outproj_body,
        grid=(t // tm,),
        in_specs=[pl.BlockSpec((tm, d), lambda i: (i, 0)),
                  pl.BlockSpec((tm, HY_CH), lambda i: (i, 0)),
                  pl.BlockSpec((tm, POOL_CH), lambda i: (i, 0)),
                  pl.BlockSpec((1, HY_CH), lambda i: (0, 0)),
                  pl.BlockSpec((None, HY_CH, d), lambda i: (layer, 0, 0)),
                  pl.BlockSpec((None, POOL_CH, d), lambda i: (layer, 1, 0))],
        out_specs=pl.BlockSpec((tm, d), lambda i: (i, 0)),
        out_shape=jax.ShapeDtypeStruct((t, d), F32),
        compiler_params=_cparams("parallel"),
        name="outproj",
    )(x, yh, yp, g_hy.reshape(1, HY_CH), w_out, w_out)


def _shift_rows(x, down):
    r = x.shape[-2]
    rows = lax.broadcasted_iota(jnp.int32, x.shape, x.ndim - 2)
    if down:
        return jnp.where(rows == 0, 0.0, pltpu.roll(x, 1, axis=x.ndim - 2))
    return jnp.where(rows == r - 1, 0.0, pltpu.roll(x, r - 1, axis=x.ndim - 2))


def _wrap_prev(x, is_first):
    return jnp.where(is_first, _shift_rows(x, True), x)


def _wrap_next(x, is_last):
    return jnp.where(is_last, _shift_rows(x, False), x)


def _pack(re, im):
    half = jnp.uint32(0x8000)
    rb = (lax.bitcast_convert_type(re, U32) + half) & jnp.uint32(0xFFFF0000)
    return rb | ((lax.bitcast_convert_type(im, U32) + half) >> 16)


def _unpack(p):
    return (lax.bitcast_convert_type(p & jnp.uint32(0xFFFF0000), F32),
            lax.bitcast_convert_type(p << 16, F32))


def _split_ri(x):
    m, c = x.shape
    x4 = x.reshape(m // (2 * SUBLANES), 2, SUBLANES, c)
    return x4[:, 0].reshape(m // 2, c), x4[:, 1].reshape(m // 2, c)


def _join_ri(re, im):
    n, c = re.shape
    g = n // SUBLANES
    return jnp.stack([re.reshape(g, SUBLANES, c), im.reshape(g, SUBLANES, c)], axis=1).reshape(2 * n, c)


def _store_cols(scr, n2i, n1, pitch, val):
    for l in range(scr.shape[0]):
        scr[l, pl.ds(n2i, n1, stride=pitch), :] = val[:, l * LANES:(l + 1) * LANES]


def _load_cols(scr, n2i, n1, pitch):
    return jnp.concatenate([scr[l, pl.ds(n2i, n1, stride=pitch), :] for l in range(scr.shape[0])], axis=1)


def _store_rows(scr, base, n2, val):
    for l in range(scr.shape[0]):
        scr[l, pl.ds(base, n2), :] = val[:, l * LANES:(l + 1) * LANES]


def _load_rows(scr, base, n2):
    return jnp.concatenate([scr[l, pl.ds(base, n2), :] for l in range(scr.shape[0])], axis=1)


def _stage1_to_scratch(x_bf16, w_ref, tw_ref, jj, scr, n2i, n1, pitch):
    res = jnp.dot(w_ref[...], x_bf16, preferred_element_type=F32)
    ar, ai = _split_ri(res)
    twr, twi = tw_ref[0, 0, :, jj:jj + 1], tw_ref[0, 1, :, jj:jj + 1]
    _store_cols(scr, n2i, n1, pitch, _pack(ar * twr - ai * twi, ar * twi + ai * twr))


def _stage2_fwd(scr, base, n2, f_ref):
    xr, xi = _unpack(_load_rows(scr, base, n2))
    s = jnp.dot(f_ref[...], _join_ri(xr, xi).astype(BF16), preferred_element_type=F32)
    return _split_ri(s)


def _lconv_body(vm_ref, gm_ref, kf_ref, w_ref, wi_ref, f_ref, g_ref, tw_ref, o_ref, scr,
                *, n2, n1, pitch, tj, tk, s1, s2):
    s = pl.program_id(2)

    @pl.when(s < s1)
    def _():
        for jj in range(tj):
            v = vm_ref[:, jj]
            nb, r, ct = v.shape
            _stage1_to_scratch(v.reshape(nb * r, ct).astype(BF16), w_ref, tw_ref, jj, scr,
                               s * tj + jj, n1, pitch)

    @pl.when(jnp.logical_and(s >= s1, s < s1 + s2))
    def _():
        for kk in range(tk):
            base = pl.multiple_of(((s - s1) * tk + kk) * pitch, SUBLANES)
            sr, si = _stage2_fwd(scr, base, n2, f_ref)
            kr, ki = _unpack(kf_ref[0, kk])
            y = _join_ri(sr * kr - si * ki, sr * ki + si * kr).astype(BF16)
            a = jnp.dot(g_ref[...], y, preferred_element_type=F32)
            _store_rows(scr, base, n2, _pack(*_split_ri(a)))

    @pl.when(s >= s1 + s2)
    def _():
        jp = s - s1 - s2
        for jj in range(tj):
            gate = gm_ref[:, jj]
            ar, ai = _unpack(_load_cols(scr, jp * tj + jj, n1, pitch))
            twr, twi = tw_ref[0, 0, :, jj:jj + 1], tw_ref[0, 1, :, jj:jj + 1]
            x = _join_ri(ar * twr + ai * twi, ai * twr - ar * twi).astype(BF16)
            y = jnp.dot(wi_ref[...], x, preferred_element_type=F32)
            o_ref[:, jj] = (gate * y.reshape(gate.shape)).astype(o_ref.dtype)


def _lconv(vsrc, v_goff, u, g_goff, kf, order, tabs, cfg, out_dtype, name):
    n2, r, n1, tj, tk, ct, pitch = (cfg[k] for k in ("N2", "R", "N1", "tj", "tk", "ct", "pitch"))
    p = vsrc.shape[0] // 2
    s1, s2 = n2 // tj, n1 // tk

    def jv(s):
        return jnp.minimum(s, s1 - 1)

    def jg(s):
        return jnp.maximum(s - s1 - s2, 0)

    def jt(s):
        return jnp.where(s < s1, s, jg(s))

    def plane_spec(goff, jf):
        return pl.BlockSpec((2, tj, r, ct), lambda p_, c, s: (p_, jf(s), 0, goff + c))

    in_specs = [plane_spec(v_goff, jv), plane_spec(g_goff, jg)]
    args = [vsrc, u]
    in_specs += [pl.BlockSpec((1, tk, n2, ct), lambda p_, c, s: (order, jnp.clip(s - s1, 0, s2 - 1), 0, c)),
                 _const_spec((2 * n1, 2 * r)), _const_spec((2 * r, 2 * n1)),
                 _const_spec((2 * n2, 2 * n2)), _const_spec((2 * n2, 2 * n2)),
                 pl.BlockSpec((1, 2, n1, tj), lambda p_, c, s: (jt(s), 0, 0, 0))]
    args += [kf, tabs["w1"], tabs["w1inv"], tabs["f2"], tabs["g2"], tabs["tw"]]
    return pl.pallas_call(
        functools.partial(_lconv_body, n2=n2, n1=n1, pitch=pitch, tj=tj, tk=tk, s1=s1, s2=s2),
        grid=(p, HY_CH // ct, 2 * s1 + s2),
        in_specs=in_specs,
        out_specs=pl.BlockSpec((2, tj, r, ct), lambda p_, c, s: (p_, jg(s), 0, c)),
        out_shape=jax.ShapeDtypeStruct((2 * p, n2, r, HY_CH), out_dtype),
        scratch_shapes=[pltpu.VMEM((ct // LANES, n1 * pitch, LANES), U32)],
        compiler_params=_cparams("parallel", "arbitrary", "arbitrary"),
        name=name,
    )(*args)


def _filt_pos_half(shape, n2, n2_total, d):
    row = lax.broadcasted_iota(jnp.int32, shape, 0)
    if d == 0:
        return (row * n2_total + n2).astype(F32)
    pos_b = (shape[0] - row) * n2_total - n2
    return jnp.where(jnp.logical_and(row == 0, n2 == 0), 0, pos_b).astype(F32)


def _filt_hidden_body(bands_ref, w1t_ref, w1c_ref, w1s_ref, b1_ref, w2_ref, b2_ref, fr_ref, h_ref,
                      *, seq_len, n2_total):
    tjh, n1 = h_ref.shape[0], h_ref.shape[1]
    lanes = tjh * n1
    idx = lax.broadcasted_iota(jnp.int32, (1, lanes), 1)
    row = idx & (n1 - 1)
    n2 = pl.program_id(0) * tjh + (idx >> int(math.log2(n1)))
    pos_b = jnp.where(jnp.logical_and(row == n1 // 2, n2 == 0), 0, (n1 - row) * n2_total - n2)
    pos = jnp.where(row < n1 // 2, row * n2_total + n2, pos_b).astype(F32)
    t = pos / float(max(seq_len - 1, 1))
    ang = bands_ref[...] * ((2.0 * math.pi / seq_len) * pos)
    pre = (w1t_ref[...] * t + jnp.dot(w1c_ref[...], jnp.cos(ang), precision=HI, preferred_element_type=F32)
           + jnp.dot(w1s_ref[...], -jnp.sin(ang), precision=HI, preferred_element_type=F32) + b1_ref[...])
    h = jnp.sin(fr_ref[...] * pre)
    h = jnp.sin(fr_ref[...] * (jnp.dot(w2_ref[...], h, precision=HI, preferred_element_type=F32) + b2_ref[...]))
    h = h.T
    hi = h.astype(BF16)
    lo = (h - hi.astype(F32)).astype(BF16)
    h_ref[...] = jnp.concatenate([hi, hi, lo], axis=1).reshape(h_ref.shape)


def _filt_hidden(w1, b1, w2, b2, freq, cfg, seq_len):
    n2, n1 = cfg["N2"], cfg["N1"]
    tjh = FILT_LANES // n1
    bands = np.linspace(1e-4, POS_BANDS - 1, POS_BANDS, dtype=np.float32).reshape(POS_BANDS, 1)
    col = lambda v: v.reshape(FILT_HIDDEN, 1)
    args = (jnp.asarray(bands), col(w1[0]), w1[1:1 + POS_BANDS].T, w1[1 + POS_BANDS:].T, col(b1), w2.T, col(b2),
            col(freq))
    return pl.pallas_call(
        functools.partial(_filt_hidden_body, seq_len=seq_len, n2_total=n2),
        grid=(n2 // tjh,),
        in_specs=[_const_spec(a.shape) for a in args],
        out_specs=pl.BlockSpec((tjh, n1, 3 * FILT_HIDDEN), lambda j: (j, 0, 0)),
        out_shape=jax.ShapeDtypeStruct((n2, n1, 3 * FILT_HIDDEN), BF16),
        compiler_params=_cparams("parallel"),
        name="filt_hidden",
    )(*args)


def _filt_spec_body(h_ref, w3_ref, dec_ref, bias_ref, wf_ref, f_ref, tw_ref, o_ref, scr, nrm_ref,
                    *, seq_len, n2, n1, pitch, tj, tk, s1):
    s = pl.program_id(2)
    r = n1 // 2
    ct = o_ref.shape[-1]

    @pl.when(s == 0)
    def _():
        nrm_ref[...] = jnp.zeros_like(nrm_ref)

    @pl.when(s < s1)
    def _():
        row = lax.broadcasted_iota(jnp.int32, (r, ct), 0)
        for jj in range(tj):
            n2i = s * tj + jj
            halves = []
            for d in range(2):
                val = jnp.dot(h_ref[jj, d * r:(d + 1) * r, :], w3_ref[0, d], preferred_element_type=F32)
                t = _filt_pos_half((r, ct), n2i, n2, d) / float(max(seq_len - 1, 1))
                val = val * jnp.exp(-t * jnp.abs(dec_ref[0, d:d + 1, :]))
                nrm_ref[...] += jnp.sum(jnp.abs(val), axis=0, keepdims=True)
                if d == 1:
                    val = jnp.where(jnp.logical_and(row == 0, n2i == 0), 0.0, val)
                halves.append(val)
            _stage1_to_scratch(jnp.concatenate(halves, axis=0).astype(BF16), wf_ref, tw_ref, jj, scr,
                               n2i, n1, pitch)

    @pl.when(s >= s1)
    def _():
        inv = 1.0 / nrm_ref[...]
        bias = bias_ref[0]
        for kk in range(tk):
            base = pl.multiple_of(((s - s1) * tk + kk) * pitch, SUBLANES)
            sr, si = _stage2_fwd(scr, base, n2, f_ref)
            o_ref[0, kk] = _pack(sr * inv + bias, si * inv)


def _filt_spec(hid, w3, dec, bias, tabs, cfg, seq_len):
    n2, n1, tj, tk, ct, pitch = (cfg[k] for k in ("N2", "N1", "tj", "tk", "ct", "pitch"))
    s1, s2 = n2 // tj, n1 // tk
    return pl.pallas_call(
        functools.partial(_filt_spec_body, seq_len=seq_len, n2=n2, n1=n1, pitch=pitch, tj=tj, tk=tk, s1=s1),
        grid=(2, HY_CH // ct, s1 + s2),
        in_specs=[pl.BlockSpec((tj, n1, 3 * FILT_HIDDEN), lambda o, c, s: (jnp.minimum(s, s1 - 1), 0, 0)),
                  pl.BlockSpec((1, 2, 3 * FILT_HIDDEN, ct), lambda o, c, s: (o, 0, 0, c)),
                  pl.BlockSpec((1, 2, ct), lambda o, c, s: (o, 0, c)),
                  pl.BlockSpec((1, 1, ct), lambda o, c, s: (o, 0, c)),
                  _const_spec((2 * n1, n1)), _const_spec((2 * n2, 2 * n2)),
                  pl.BlockSpec((1, 2, n1, tj), lambda o, c, s: (jnp.minimum(s, s1 - 1), 0, 0, 0))],
        out_specs=pl.BlockSpec((1, tk, n2, ct), lambda o, c, s: (o, jnp.maximum(s - s1, 0), 0, c)),
        out_shape=jax.ShapeDtypeStruct((2, n1, n2, HY_CH), U32),
        scratch_shapes=[pltpu.VMEM((ct // LANES, n1 * pitch, LANES), U32), pltpu.VMEM((1, ct), F32)],
        compiler_params=_cparams("parallel", "arbitrary", "arbitrary"),
        name="filt_spec",
    )(hid, w3, dec, bias, tabs["wf"], tabs["f2"], tabs["tw"])


def _pool_body(*refs, seq_len, n2_total):
    ng = len(POOL_WINDOWS)
    m_ref, p_refs, n_refs = refs[0], refs[1:1 + ng], refs[1 + ng:2 * ng]
    pw_ref, ps_ref, gp_ref, o_ref = refs[2 * ng:]
    tj, r = m_ref.shape[1], m_ref.shape[2]
    j = pl.program_id(1)
    first, last = j == 0, j == pl.num_programs(1) - 1
    j0 = j * tj
    plane = lax.broadcasted_iota(jnp.int32, (tj, r, LANES), 0)
    row = lax.broadcasted_iota(jnp.int32, (tj, r, LANES), 1)
    t = row * n2_total + j0 + plane
    outs = []
    for g, win in enumerate(POOL_WINDOWS):
        sl = slice(g * POOL_GROUP, (g + 1) * POOL_GROUP)
        half = win // 2
        main = m_ref[0, :, :, sl]
        parts = [_wrap_prev(p_refs[g][0], first), main]
        if half > 1:
            parts.append(_wrap_next(n_refs[g - 1][0, 0:half - 1], last))
        acc = jnp.concatenate(parts, axis=0)
        width = 1
        while width < win:
            n = acc.shape[0] - width
            acc = acc[:n] + acc[width:width + n]
            width *= 2
        cnt = (jnp.minimum(t + half - 1, seq_len - 1) - jnp.maximum(t - half, 0) + 1).astype(F32)
        dgrp = acc / jnp.concatenate([cnt] * (POOL_GROUP // LANES), axis=-1) - main
        outs.append(jnp.dot(dgrp.reshape(tj * r, POOL_GROUP).astype(BF16), pw_ref[g],
                            preferred_element_type=F32))
    y = jnp.concatenate(outs, axis=-1) * ps_ref[...]
    o_ref[0] = _rms(y, gp_ref[...]).astype(BF16).reshape(tj, r, POOL_CH)


def _pool(u, pool_w, pool_scale, g_pool, cfg, seq_len):
    n2, r, tj = cfg["N2"], cfg["R"], cfg["tj_pool"]
    b = u.shape[0]
    goff = 3 * HY_CH // POOL_CH

    def halo_spec(g, nxt):
        h = POOL_WINDOWS[g] // 2
        nh = n2 // h
        cb = goff * (POOL_CH // POOL_GROUP) + g
        if nxt:
            return pl.BlockSpec((1, h, r, POOL_GROUP), lambda i, j: (i, ((j + 1) * (tj // h)) % nh, 0, cb))
        return pl.BlockSpec((1, h, r, POOL_GROUP), lambda i, j: (i, (j * (tj // h) + nh - 1) % nh, 0, cb))

    ng = len(POOL_WINDOWS)
    return pl.pallas_call(
        functools.partial(_pool_body, seq_len=seq_len, n2_total=n2),
        grid=(b, n2 // tj),
        in_specs=[pl.BlockSpec((1, tj, r, POOL_CH), lambda i, j: (i, j, 0, goff))]
        + [halo_spec(g, False) for g in range(ng)] + [halo_spec(g, True) for g in range(1, ng)]
        + [_const_spec((ng, POOL_GROUP, POOL_GROUP)), _const_spec((1, POOL_CH)), _const_spec((1, POOL_CH))],
        out_specs=pl.BlockSpec((1, tj, r, POOL_CH), lambda i, j: (i, j, 0, 0)),
        out_shape=jax.ShapeDtypeStruct((b, n2, r, POOL_CH), BF16),
        compiler_params=_cparams("parallel", "arbitrary"),
        name="pool",
    )(*([u] * (2 * ng)), pool_w, pool_scale, g_pool)


def _fft_cfg(seq_len):
    n2 = 128 if seq_len >= 8192 else 64
    r = seq_len // n2
    assert r % SUBLANES == 0 and r * n2 == seq_len
    tj = 8 if seq_len >= 8192 else 32
    return dict(N2=n2, R=r, N1=2 * r, tj=tj, tk=max(tj, 16), ct=256, pitch=n2 + SUBLANES, tj_pool=16)


def _dft_tables(cfg):
    n2, r, n1, tj = cfg["N2"], cfg["R"], cfg["N1"], cfg["tj"]
    n = n1 * n2

    def cis(idx, period):
        a = -2.0 * np.pi * (idx % period) / period
        return np.cos(a), np.sin(a)

    def ri_order(m):
        pos = np.arange(2 * m)
        return (pos % (2 * SUBLANES)) // SUBLANES * m + pos // (2 * SUBLANES) * SUBLANES + pos % SUBLANES

    k1 = np.arange(n1)
    mr, mi = cis(k1[:, None] * k1[None, :], n1)
    mrd, mid = mr[:, :r], mi[:, :r]
    o1, o2 = ri_order(n1), ri_order(n2)
    w1 = np.block([[mrd, -mid], [mid, mrd]])[o1]
    wf = np.concatenate([mr, mi], axis=0)[o1]
    k2 = np.arange(n2)
    fr, fi = cis(k2[:, None] * k2[None, :], n2)
    f2 = np.block([[fr, -fi], [fi, fr]])[o2][:, o2]
    g2 = np.block([[fr, fi], [-fi, fr]])[o2][:, o2]
    twr, twi = cis(k1[:, None] * k2[None, :], n)
    tw = np.stack([twr, twi]).reshape(2, n1, n2 // tj, tj).transpose(2, 0, 1, 3)
    return dict(w1=jnp.asarray(w1, BF16), w1inv=jnp.asarray(w1.T / n, BF16), wf=jnp.asarray(wf, BF16),
                f2=jnp.asarray(f2, BF16), g2=jnp.asarray(g2, BF16), tw=jnp.asarray(tw, F32))


def _to_plane_order(x, cfg):
    b, l, d = x.shape
    return jnp.swapaxes(x.reshape(b, cfg["R"], cfg["N2"], d), 1, 2).reshape(b * l, d)


def _from_plane_order(x, b, cfg):
    d = x.shape[-1]
    return jnp.swapaxes(x.reshape(b, cfg["N2"], cfg["R"], d), 1, 2).reshape(b, cfg["N2"] * cfg["R"], d)


def _trunk(x, mem, lw, g_final, depth):
    b, seq_len, d = x.shape
    cfg = _fft_cfg(seq_len)
    tabs = _dft_tables(cfg)
    n2, r, ct = cfg["N2"], cfg["R"], cfg["ct"]
    xs = _to_plane_order(x, cfg)
    memf = mem.reshape(b * N_MEM, d)
    cpb = HY_CH // ct
    for i in range(depth):
        w = {k: v[i] for k, v in lw.items() if k not in _STACKED_WEIGHTS}
        hid = _filt_hidden(w["filt_w1"], w["filt_b1"], w["filt_w2"], w["filt_b2"], w["filt_freq"], cfg, seq_len)
        w3 = jnp.transpose(w["filt_w3"].reshape(FILT_HIDDEN, 2, 2, HY_CH), (1, 2, 0, 3))
        w3hi = w3.astype(BF16)
        w3lo = (w3 - w3hi.astype(F32)).astype(BF16)
        kf = _filt_spec(hid, jnp.concatenate([w3hi, w3lo, w3hi], axis=2),
                        w["hy_decay"], w["hy_bias"].reshape(2, 1, HY_CH), tabs, cfg, seq_len)
        u = _in_proj(xs.reshape(b, n2, r, d), w["g_mix"], lw["w_in"], i, w["conv_w"], w["conv_b"].reshape(1, -1), cfg)
        z1 = _lconv(u, 2 * cpb, u, 0, kf, 0, tabs, cfg, BF16, "lconv1")
        yh = _lconv(z1, 0, u, cpb, kf, 1, tabs, cfg, F32, "lconv2")
        yp = _pool(u, w["pool_w"], w["pool_scale"].reshape(1, -1), w["g_pool"].reshape(1, -1), cfg, seq_len)
        xs = _outproj(xs, yh.reshape(b * seq_len, HY_CH), yp.reshape(b * seq_len, POOL_CH), w["g_hy"],
                      lw["w_out"], i)
        kv = _norm_matmul(memf, w["g_mem"], lw["w_kv"], i, BF16, 512, 1024, "kv_proj").reshape(b, N_MEM, 2 * d)
        xs = _attn(xs.reshape(b, seq_len, d), kv, w["g_xa"], lw["w_q"], lw["w_o"], i).reshape(b * seq_len, d)
        xs = _mlp(xs, w["g_mlp"], lw["w_up"], lw["w_down"], i, g_final, i == depth - 1)
    return _from_plane_order(xs, b, cfg)


_STACKED_WEIGHTS = ("w_in", "w_out", "w_q", "w_kv", "w_o", "w_up", "w_down")
_MATMUL_WEIGHTS = _STACKED_WEIGHTS + ("pool_w",)


def kernel(x_prompt, x_sample, mem_prompt, mem_sample, g_mix, w_in, conv_w, conv_b, filt_w1, filt_b1, filt_w2, filt_b2, filt_freq, filt_w3, hy_decay, hy_bias, pool_w, pool_scale, g_hy, g_pool, w_out, g_xa, g_mem, w_q, w_kv, w_o, g_mlp, w_up, w_down, g_final):
    lw = dict(g_mix=g_mix, w_in=w_in, conv_w=conv_w, conv_b=conv_b, filt_w1=filt_w1, filt_b1=filt_b1,
              filt_w2=filt_w2, filt_b2=filt_b2, filt_freq=filt_freq, filt_w3=filt_w3, hy_decay=hy_decay,
              hy_bias=hy_bias, pool_w=pool_w, pool_scale=pool_scale, g_hy=g_hy, g_pool=g_pool, w_out=w_out,
              g_xa=g_xa, g_mem=g_mem, w_q=w_q, w_kv=w_kv, w_o=w_o, g_mlp=g_mlp, w_up=w_up, w_down=w_down)
    lw = {k: (v.astype(BF16) if k in _MATMUL_WEIGHTS else v) for k, v in lw.items()}
    depth = g_mix.shape[0]
    y_prompt = _trunk(x_prompt, mem_prompt, lw, g_final, depth)
    y_sample = _trunk(x_sample, mem_sample, lw, g_final, depth)
    return (y_prompt, y_sample)
```

```python
import functools
import math

import numpy as np
import jax
import jax.numpy as jnp
from jax import lax
from jax.experimental import pallas as pl
from jax.experimental.pallas import tpu as pltpu

F32 = jnp.float32
BF16 = jnp.bfloat16
U32 = jnp.uint32

D_MODEL = 2048
HY_CH = 1024
POOL_CH = 1024
D_IN = 3 * HY_CH + POOL_CH
POOL_WINDOWS = (2, 4, 8, 16)
POOL_GROUP = POOL_CH // len(POOL_WINDOWS)
POS_BANDS = 16
FILT_HIDDEN = 64
N_MEM = 256
XA_HEADS = 4
XA_HEAD_DIM = D_MODEL // XA_HEADS
D_FF = 4 * D_MODEL
EPS = 1e-6

LANES = 128
SUBLANES = 8
VMEM_LIMIT_BYTES = 58 * 1024 * 1024
NORM_CHUNK_ROWS = 256
FILT_LANES = 1024
IN_PROJ_TOKENS = 1024
HY_CONV_BLOCKS = 3
HI = lax.Precision.HIGHEST


def _cparams(*sem):
    return pltpu.CompilerParams(dimension_semantics=sem, vmem_limit_bytes=VMEM_LIMIT_BYTES)


def _rms(x, g):
    return x * lax.rsqrt(jnp.mean(x * x, axis=-1, keepdims=True) + EPS) * g


def _const_spec(shape):
    nd = len(shape)
    return pl.BlockSpec(shape, lambda *_: (0,) * nd)


def _norm_matmul_body(x_ref, g_ref, w_ref, o_ref, h_ref):
    rows = NORM_CHUNK_ROWS

    @pl.when(pl.program_id(1) == 0)
    def _():
        for c in range(x_ref.shape[0] // rows):
            sl = slice(c * rows, (c + 1) * rows)
            h = _rms(x_ref[sl, :], g_ref[...]).astype(BF16)
            h_ref[sl, :] = h
            o_ref[sl, :] = jnp.dot(h, w_ref[...], preferred_element_type=F32).astype(o_ref.dtype)

    @pl.when(pl.program_id(1) != 0)
    def _():
        o_ref[...] = jnp.dot(h_ref[...], w_ref[...], preferred_element_type=F32).astype(o_ref.dtype)


def _norm_matmul(x, g, w, layer, out_dtype, tm, tn, name):
    t, d = x.shape
    n = w.shape[2]
    return pl.pallas_call(
        _norm_matmul_body,
        grid=(t // tm, n // tn),
        in_specs=[pl.BlockSpec((tm, d), lambda i, j: (i, 0)),
                  pl.BlockSpec((1, d), lambda i, j: (0, 0)),
                  pl.BlockSpec((None, d, tn), lambda i, j: (layer, 0, j))],
        out_specs=pl.BlockSpec((tm, tn), lambda i, j: (i, j)),
        out_shape=jax.ShapeDtypeStruct((t, n), out_dtype),
        scratch_shapes=[pltpu.VMEM((tm, d), BF16)],
        compiler_params=_cparams("parallel", "arbitrary"),
        name=name,
    )(x, g.reshape(1, d), w)


def _in_proj_body(x_ref, xp_ref, xn_ref, g_ref, w_ref, cw_ref, cb_ref, o_ref, h_ref, hh_ref):
    gi, j = pl.program_id(1), pl.program_id(2)
    n2, rt, d = x_ref.shape[1:]
    tn = w_ref.shape[1]
    rows = NORM_CHUNK_ROWS
    ppc = rows // rt

    def emit(first, conv):
        if conv:
            w0, w1, w2, b = cw_ref[0:1, :], cw_ref[1:2, :], cw_ref[2:3, :], cb_ref[...]
            tap = lambda lo_, mid_, hi_: lo_ * w0 + mid_ * w1 + hi_ * w2 + b
        if first:
            hh_ref[...] = _rms(jnp.concatenate([xp_ref[0, 0], xn_ref[0, 0]], axis=0), g_ref[...]).astype(BF16)
        for c in range(n2 // ppc):
            if first:
                h = _rms(x_ref[0, c * ppc:(c + 1) * ppc].reshape(rows, d), g_ref[...]).astype(BF16)
                h_ref[c * rows:(c + 1) * rows, :] = h
            else:
                h = h_ref[c * rows:(c + 1) * rows, :]
            uc = jnp.dot(h, w_ref[...], preferred_element_type=F32).reshape(ppc, rt, tn)
            p0 = c * ppc
            if not conv:
                o_ref[0, p0:p0 + ppc] = uc
                continue
            if c == 0:
                u0, u1 = uc[0], uc[1]
            else:
                o_ref[0, p0 - 1] = tap(t2, t1, uc[0])
                o_ref[0, p0] = tap(t1, uc[0], uc[1])
            o_ref[0, p0 + 1:p0 + ppc - 1] = tap(uc[0:ppc - 2], uc[1:ppc - 1], uc[2:ppc])
            t2, t1 = uc[ppc - 2], uc[ppc - 1]
        if conv:
            uh = jnp.dot(hh_ref[...], w_ref[...], preferred_element_type=F32)
            row = lax.broadcasted_iota(jnp.int32, (rt, tn), 0)
            prev_tok = jnp.where(gi == 0, 0.0, uh[SUBLANES - 1:SUBLANES])
            next_tok = jnp.where(gi == pl.num_programs(1) - 1, 0.0, uh[SUBLANES:SUBLANES + 1])
            wrap_prev = jnp.where(row == 0, prev_tok, pltpu.roll(t1, 1, axis=0))
            wrap_next = jnp.where(row == rt - 1, next_tok, pltpu.roll(u0, rt - 1, axis=0))
            o_ref[0, 0] = tap(wrap_prev, u0, u1)
            o_ref[0, n2 - 1] = tap(t2, t1, wrap_next)

    pl.when(j == 0)(lambda: emit(True, True))
    pl.when(jnp.logical_and(j > 0, j < HY_CONV_BLOCKS))(lambda: emit(False, True))
    pl.when(j >= HY_CONV_BLOCKS)(lambda: emit(False, False))


def _in_proj(x4, g, w, layer, conv_w, conv_b, cfg):
    b, n2, r, d = x4.shape
    rt = IN_PROJ_TOKENS // n2
    tn = HY_CH
    rg = r // SUBLANES
    return pl.pallas_call(
        _in_proj_body,
        grid=(b, r // rt, D_IN // tn),
        in_specs=[pl.BlockSpec((1, n2, rt, d), lambda i, g_, j: (i, 0, g_, 0)),
                  pl.BlockSpec((1, 1, SUBLANES, d),
                               lambda i, g_, j: (i, n2 - 1, jnp.maximum(g_ * (rt // SUBLANES) - 1, 0), 0)),
                  pl.BlockSpec((1, 1, SUBLANES, d),
                               lambda i, g_, j: (i, 0, jnp.minimum((g_ + 1) * (rt // SUBLANES), rg - 1), 0)),
                  pl.BlockSpec((1, d), lambda i, g_, j: (0, 0)),
                  pl.BlockSpec((None, d, tn), lambda i, g_, j: (layer, 0, j)),
                  pl.BlockSpec((3, tn), lambda i, g_, j: (0, jnp.minimum(j, HY_CONV_BLOCKS - 1))),
                  pl.BlockSpec((1, tn), lambda i, g_, j: (0, jnp.minimum(j, HY_CONV_BLOCKS - 1)))],
        out_specs=pl.BlockSpec((1, n2, rt, tn), lambda i, g_, j: (i, 0, g_, j)),
        out_shape=jax.ShapeDtypeStruct((b, n2, r, D_IN), F32),
        scratch_shapes=[pltpu.VMEM((n2 * rt, d), BF16), pltpu.VMEM((2 * SUBLANES, d), BF16)],
        compiler_params=_cparams("parallel", "parallel", "arbitrary"),
        name="in_proj",
    )(x4, x4, x4, g.reshape(1, d), w, conv_w, conv_b)


def _mlp_body(x_ref, g_ref, wu_ref, wd_ref, gf_ref, o_ref, h_ref, *, final_norm):
    f = pl.program_id(1)
    rows = NORM_CHUNK_ROWS

    def up_down(h):
        hm = jnp.dot(h, wu_ref[...], preferred_element_type=F32)
        a = jnp.square(jnp.maximum(hm, 0.0)).astype(BF16)
        return jnp.dot(a, wd_ref[...], preferred_element_type=F32)

    @pl.when(f == 0)
    def _():
        for c in range(x_ref.shape[0] // rows):
            sl = slice(c * rows, (c + 1) * rows)
            x = x_ref[sl, :]
            h = _rms(x, g_ref[...]).astype(BF16)
            h_ref[sl, :] = h
            o_ref[sl, :] = x + up_down(h)

    @pl.when(f != 0)
    def _():
        o_ref[...] += up_down(h_ref[...])

    if final_norm:
        @pl.when(f == pl.num_programs(1) - 1)
        def _():
            o_ref[...] = _rms(o_ref[...], gf_ref[...])


def _mlp(x, g, wu, wd, layer, g_final, final_norm, tm=512, tf=1024):
    t, d = x.shape
    ff = wu.shape[2]
    return pl.pallas_call(
        functools.partial(_mlp_body, final_norm=final_norm),
        grid=(t // tm, ff // tf),
        in_specs=[pl.BlockSpec((tm, d), lambda i, f: (i, 0)),
                  pl.BlockSpec((1, d), lambda i, f: (0, 0)),
                  pl.BlockSpec((None, d, tf), lambda i, f: (layer, 0, f)),
                  pl.BlockSpec((None, tf, d), lambda i, f: (layer, f, 0)),
                  pl.BlockSpec((1, d), lambda i, f: (0, 0))],
        out_specs=pl.BlockSpec((tm, d), lambda i, f: (i, 0)),
        out_shape=jax.ShapeDtypeStruct((t, d), F32),
        scratch_shapes=[pltpu.VMEM((tm, d), BF16)],
        compiler_params=_cparams("parallel", "arbitrary"),
        name="mlp",
    )(x, g.reshape(1, d), wu, wd, g_final.reshape(1, d))


def _attn_body(x_ref, k_ref, v_ref, g_ref, wq_ref, wo_ref, o_ref):
    scale = XA_HEAD_DIM ** -0.5
    x = x_ref[0]
    h = _rms(x, g_ref[...]).astype(BF16)
    q = jnp.dot(h, wq_ref[...], preferred_element_type=F32)
    heads = []
    for hd in range(XA_HEADS):
        sl = slice(hd * XA_HEAD_DIM, (hd + 1) * XA_HEAD_DIM)
        s = lax.dot_general(q[:, sl].astype(BF16), k_ref[0, :, sl], (((1,), (1,)), ((), ())),
                            preferred_element_type=F32) * scale
        p = jnp.exp(s - jnp.max(s, axis=-1, keepdims=True))
        a = (p / jnp.sum(p, axis=-1, keepdims=True)).astype(BF16)
        heads.append(jnp.dot(a, v_ref[0, :, sl], preferred_element_type=F32).astype(BF16))
    o = jnp.concatenate(heads, axis=-1)
    o_ref[0] = x + jnp.dot(o, wo_ref[...], preferred_element_type=F32)


def _attn(x, kv, g, wq, wo, layer, tq=512):
    b, l, d = x.shape
    return pl.pallas_call(
        _attn_body,
        grid=(b, l // tq),
        in_specs=[pl.BlockSpec((1, tq, d), lambda i, j: (i, j, 0)),
                  pl.BlockSpec((1, N_MEM, d), lambda i, j: (i, 0, 0)),
                  pl.BlockSpec((1, N_MEM, d), lambda i, j: (i, 0, 1)),
                  pl.BlockSpec((1, d), lambda i, j: (0, 0)),
                  pl.BlockSpec((None, d, d), lambda i, j: (layer, 0, 0)),
                  pl.BlockSpec((None, d, d), lambda i, j: (layer, 0, 0))],
        out_specs=pl.BlockSpec((1, tq, d), lambda i, j: (i, j, 0)),
        out_shape=jax.ShapeDtypeStruct((b, l, d), F32),
        compiler_params=_cparams("parallel", "arbitrary"),
        name="attn",
    )(x, kv, kv, g.reshape(1, d), wq, wo)


def _outproj_body(x_ref, yh_ref, yp_ref, gh_ref, wt_ref, wb_ref, o_ref):
    yh = _rms(yh_ref[...], gh_ref[...]).astype(BF16)
    acc = jnp.dot(yh, wt_ref[...], preferred_element_type=F32)
    acc += jnp.dot(yp_ref[...], wb_ref[...], preferred_element_type=F32)
    o_ref[...] = x_ref[...] + acc


def _outproj(x, yh, yp, g_hy, w_out, layer, tm=512):
    t, d = x.shape
    return pl.pallas_call(
        _outproj_body,
        grid=(t // tm,),
        in_specs=[pl.BlockSpec((tm, d), lambda i: (i, 0)),
                  pl.BlockSpec((tm, HY_CH), lambda i: (i, 0)),
                  pl.BlockSpec((tm, POOL_CH), lambda i: (i, 0)),
                  pl.BlockSpec((1, HY_CH), lambda i: (0, 0)),
                  pl.BlockSpec((None, HY_CH, d), lambda i: (layer, 0, 0)),
                  pl.BlockSpec((None, POOL_CH, d), lambda i: (layer, 1, 0))],
        out_specs=pl.BlockSpec((tm, d), lambda i: (i, 0)),
        out_shape=jax.ShapeDtypeStruct((t, d), F32),
        compiler_params=_cparams("parallel"),
        name="outproj",
    )(x, yh, yp, g_hy.reshape(1, HY_CH), w_out, w_out)


def _shift_rows(x, down):
    r = x.shape[-2]
    rows = lax.broadcasted_iota(jnp.int32, x.shape, x.ndim - 2)
    if down:
        return jnp.where(rows == 0, 0.0, pltpu.roll(x, 1, axis=x.ndim - 2))
    return jnp.where(rows == r - 1, 0.0, pltpu.roll(x, r - 1, axis=x.ndim - 2))


def _wrap_prev(x, is_first):
    return jnp.where(is_first, _shift_rows(x, True), x)


def _wrap_next(x, is_last):
    return jnp.where(is_last, _shift_rows(x, False), x)


def _pack(re, im):
    half = jnp.uint32(0x8000)
    rb = (lax.bitcast_convert_type(re, U32) + half) & jnp.uint32(0xFFFF0000)
    return rb | ((lax.bitcast_convert_type(im, U32) + half) >> 16)


def _unpack(p):
    return (lax.bitcast_convert_type(p & jnp.uint32(0xFFFF0000), F32),
            lax.bitcast_convert_type(p << 16, F32))


def _split_ri(x):
    m, c = x.shape
    x4 = x.reshape(m // (2 * SUBLANES), 2, SUBLANES, c)
    return x4[:, 0].reshape(m // 2, c), x4[:, 1].reshape(m // 2, c)


def _join_ri(re, im):
    n, c = re.shape
    g = n // SUBLANES
    return jnp.stack([re.reshape(g, SUBLANES, c), im.reshape(g, SUBLANES, c)], axis=1).reshape(2 * n, c)


def _store_cols(scr, n2i, n1, pitch, val):
    for l in range(scr.shape[0]):
        scr[l, pl.ds(n2i, n1, stride=pitch), :] = val[:, l * LANES:(l + 1) * LANES]


def _load_cols(scr, n2i, n1, pitch):
    return jnp.concatenate([scr[l, pl.ds(n2i, n1, stride=pitch), :] for l in range(scr.shape[0])], axis=1)


def _store_rows(scr, base, n2, val):
    for l in range(scr.shape[0]):
        scr[l, pl.ds(base, n2), :] = val[:, l * LANES:(l + 1) * LANES]


def _load_rows(scr, base, n2):
    return jnp.concatenate([scr[l, pl.ds(base, n2), :] for l in range(scr.shape[0])], axis=1)


def _stage1_to_scratch(x_bf16, w_ref, tw_ref, jj, scr, n2i, n1, pitch):
    res = jnp.dot(w_ref[...], x_bf16, preferred_element_type=F32)
    ar, ai = _split_ri(res)
    twr, twi = tw_ref[0, 0, :, jj:jj + 1], tw_ref[0, 1, :, jj:jj + 1]
    _store_cols(scr, n2i, n1, pitch, _pack(ar * twr - ai * twi, ar * twi + ai * twr))


def _stage2_fwd(scr, base, n2, f_ref):
    xr, xi = _unpack(_load_rows(scr, base, n2))
    s = jnp.dot(f_ref[...], _join_ri(xr, xi).astype(BF16), preferred_element_type=F32)
    return _split_ri(s)


def _lconv_body(vm_ref, gm_ref, kf_ref, w_ref, wi_ref, f_ref, g_ref, tw_ref, o_ref, scr,
                *, n2, n1, pitch, tj, tk, s1, s2):
    s = pl.program_id(2)

    @pl.when(s < s1)
    def _():
        for jj in range(tj):
            v = vm_ref[:, jj]
            nb, r, ct = v.shape
            _stage1_to_scratch(v.reshape(nb * r, ct).astype(BF16), w_ref, tw_ref, jj, scr,
                               s * tj + jj, n1, pitch)

    @pl.when(jnp.logical_and(s >= s1, s < s1 + s2))
    def _():
        for kk in range(tk):
            base = pl.multiple_of(((s - s1) * tk + kk) * pitch, SUBLANES)
            sr, si = _stage2_fwd(scr, base, n2, f_ref)
            kr, ki = _unpack(kf_ref[0, kk])
            y = _join_ri(sr * kr - si * ki, sr * ki + si * kr).astype(BF16)
            a = jnp.dot(g_ref[...], y, preferred_element_type=F32)
            _store_rows(scr, base, n2, _pack(*_split_ri(a)))

    @pl.when(s >= s1 + s2)
    def _():
        jp = s - s1 - s2
        for jj in range(tj):
            gate = gm_ref[:, jj]
            ar, ai = _unpack(_load_cols(scr, jp * tj + jj, n1, pitch))
            twr, twi = tw_ref[0, 0, :, jj:jj + 1], tw_ref[0, 1, :, jj:jj + 1]
            x = _join_ri(ar * twr + ai * twi, ai * twr - ar * twi).astype(BF16)
            y = jnp.dot(wi_ref[...], x, preferred_element_type=F32)
            o_ref[:, jj] = (gate * y.reshape(gate.shape)).astype(o_ref.dtype)


def _lconv(vsrc, v_goff, u, g_goff, kf, order, tabs, cfg, out_dtype, name):
    n2, r, n1, tj, tk, ct, pitch = (cfg[k] for k in ("N2", "R", "N1", "tj", "tk", "ct", "pitch"))
    p = vsrc.shape[0] // 2
    s1, s2 = n2 // tj, n1 // tk

    def jv(s):
        return jnp.minimum(s, s1 - 1)

    def jg(s):
        return jnp.maximum(s - s1 - s2, 0)

    def jt(s):
        return jnp.where(s < s1, s, jg(s))

    def plane_spec(goff, jf):
        return pl.BlockSpec((2, tj, r, ct), lambda p_, c, s: (p_, jf(s), 0, goff + c))

    in_specs = [plane_spec(v_goff, jv), plane_spec(g_goff, jg)]
    args = [vsrc, u]
    in_specs += [pl.BlockSpec((1, tk, n2, ct), lambda p_, c, s: (order, jnp.clip(s - s1, 0, s2 - 1), 0, c)),
                 _const_spec((2 * n1, 2 * r)), _const_spec((2 * r, 2 * n1)),
                 _const_spec((2 * n2, 2 * n2)), _const_spec((2 * n2, 2 * n2)),
                 pl.BlockSpec((1, 2, n1, tj), lambda p_, c, s: (jt(s), 0, 0, 0))]
    args += [kf, tabs["w1"], tabs["w1inv"], tabs["f2"], tabs["g2"], tabs["tw"]]
    return pl.pallas_call(
        functools.partial(_lconv_body, n2=n2, n1=n1, pitch=pitch, tj=tj, tk=tk, s1=s1, s2=s2),
        grid=(p, HY_CH // ct, 2 * s1 + s2),
        in_specs=in_specs,
        out_specs=pl.BlockSpec((2, tj, r, ct), lambda p_, c, s: (p_, jg(s), 0, c)),
        out_shape=jax.ShapeDtypeStruct((2 * p, n2, r, HY_CH), out_dtype),
        scratch_shapes=[pltpu.VMEM((ct // LANES, n1 * pitch, LANES), U32)],
        compiler_params=_cparams("parallel", "arbitrary", "arbitrary"),
        name=name,
    )(*args)


def _filt_pos_half(shape, n2, n2_total, d):
    row = lax.broadcasted_iota(jnp.int32, shape, 0)
    if d == 0:
        return (row * n2_total + n2).astype(F32)
    pos_b = (shape[0] - row) * n2_total - n2
    return jnp.where(jnp.logical_and(row == 0, n2 == 0), 0, pos_b).astype(F32)


def _filt_hidden_body(bands_ref, w1t_ref, w1c_ref, w1s_ref, b1_ref, w2_ref, b2_ref, fr_ref, h_ref,
                      *, seq_len, n2_total):
    tjh, n1 = h_ref.shape[0], h_ref.shape[1]
    lanes = tjh * n1
    idx = lax.broadcasted_iota(jnp.int32, (1, lanes), 1)
    row = idx & (n1 - 1)
    n2 = pl.program_id(0) * tjh + (idx >> int(math.log2(n1)))
    pos_b = jnp.where(jnp.logical_and(row == n1 // 2, n2 == 0), 0, (n1 - row) * n2_total - n2)
    pos = jnp.where(row < n1 // 2, row * n2_total + n2, pos_b).astype(F32)
    t = pos / float(max(seq_len - 1, 1))
    ang = bands_ref[...] * ((2.0 * math.pi / seq_len) * pos)
    pre = (w1t_ref[...] * t + jnp.dot(w1c_ref[...], jnp.cos(ang), precision=HI, preferred_element_type=F32)
           + jnp.dot(w1s_ref[...], -jnp.sin(ang), precision=HI, preferred_element_type=F32) + b1_ref[...])
    h = jnp.sin(fr_ref[...] * pre)
    h = jnp.sin(fr_ref[...] * (jnp.dot(w2_ref[...], h, precision=HI, preferred_element_type=F32) + b2_ref[...]))
    h = h.T
    hi = h.astype(BF16)
    lo = (h - hi.astype(F32)).astype(BF16)
    h_ref[...] = jnp.concatenate([hi, hi, lo], axis=1).reshape(h_ref.shape)


def _filt_hidden(w1, b1, w2, b2, freq, cfg, seq_len):
    n2, n1 = cfg["N2"], cfg["N1"]
    tjh = FILT_LANES // n1
    bands = np.linspace(1e-4, POS_BANDS - 1, POS_BANDS, dtype=np.float32).reshape(POS_BANDS, 1)
    col = lambda v: v.reshape(FILT_HIDDEN, 1)
    args = (jnp.asarray(bands), col(w1[0]), w1[1:1 + POS_BANDS].T, w1[1 + POS_BANDS:].T, col(b1), w2.T, col(b2),
            col(freq))
    return pl.pallas_call(
        functools.partial(_filt_hidden_body, seq_len=seq_len, n2_total=n2),
        grid=(n2 // tjh,),
        in_specs=[_const_spec(a.shape) for a in args],
        out_specs=pl.BlockSpec((tjh, n1, 3 * FILT_HIDDEN), lambda j: (j, 0, 0)),
        out_shape=jax.ShapeDtypeStruct((n2, n1, 3 * FILT_HIDDEN), BF16),
        compiler_params=_cparams("parallel"),
        name="filt_hidden",
    )(*args)


def _filt_spec_body(h_ref, w3_ref, dec_ref, bias_ref, wf_ref, f_ref, tw_ref, o_ref, scr, nrm_ref,
                    *, seq_len, n2, n1, pitch, tj, tk, s1):
    s = pl.program_id(2)
    r = n1 // 2
    ct = o_ref.shape[-1]

    @pl.when(s == 0)
    def _():
        nrm_ref[...] = jnp.zeros_like(nrm_ref)

    @pl.when(s < s1)
    def _():
        row = lax.broadcasted_iota(jnp.int32, (r, ct), 0)
        for jj in range(tj):
            n2i = s * tj + jj
            halves = []
            for d in range(2):
                val = jnp.dot(h_ref[jj, d * r:(d + 1) * r, :], w3_ref[0, d], preferred_element_type=F32)
                t = _filt_pos_half((r, ct), n2i, n2, d) / float(max(seq_len - 1, 1))
                val = val * jnp.exp(-t * jnp.abs(dec_ref[0, d:d + 1, :]))
                nrm_ref[...] += jnp.sum(jnp.abs(val), axis=0, keepdims=True)
                if d == 1:
                    val = jnp.where(jnp.logical_and(row == 0, n2i == 0), 0.0, val)
                halves.append(val)
            _stage1_to_scratch(jnp.concatenate(halves, axis=0).astype(BF16), wf_ref, tw_ref, jj, scr,
                               n2i, n1, pitch)

    @pl.when(s >= s1)
    def _():
        inv = 1.0 / nrm_ref[...]
        bias = bias_ref[0]
        for kk in range(tk):
            base = pl.multiple_of(((s - s1) * tk + kk) * pitch, SUBLANES)
            sr, si = _stage2_fwd(scr, base, n2, f_ref)
            o_ref[0, kk] = _pack(sr * inv + bias, si * inv)


def _filt_spec(hid, w3, dec, bias, tabs, cfg, seq_len):
    n2, n1, tj, tk, ct, pitch = (cfg[k] for k in ("N2", "N1", "tj", "tk", "ct", "pitch"))
    s1, s2 = n2 // tj, n1 // tk
    return pl.pallas_call(
        functools.partial(_filt_spec_body, seq_len=seq_len, n2=n2, n1=n1, pitch=pitch, tj=tj, tk=tk, s1=s1),
        grid=(2, HY_CH // ct, s1 + s2),
        in_specs=[pl.BlockSpec((tj, n1, 3 * FILT_HIDDEN), lambda o, c, s: (jnp.minimum(s, s1 - 1), 0, 0)),
                  pl.BlockSpec((1, 2, 3 * FILT_HIDDEN, ct), lambda o, c, s: (o, 0, 0, c)),
                  pl.BlockSpec((1, 2, ct), lambda o, c, s: (o, 0, c)),
                  pl.BlockSpec((1, 1, ct), lambda o, c, s: (o, 0, c)),
                  _const_spec((2 * n1, n1)), _const_spec((2 * n2, 2 * n2)),
                  pl.BlockSpec((1, 2, n1, tj), lambda o, c, s: (jnp.minimum(s, s1 - 1), 0, 0, 0))],
        out_specs=pl.BlockSpec((1, tk, n2, ct), lambda o, c, s: (o, jnp.maximum(s - s1, 0), 0, c)),
        out_shape=jax.ShapeDtypeStruct((2, n1, n2, HY_CH), U32),
        scratch_shapes=[pltpu.VMEM((ct // LANES, n1 * pitch, LANES), U32), pltpu.VMEM((1, ct), F32)],
        compiler_params=_cparams("parallel", "arbitrary", "arbitrary"),
        name="filt_spec",
    )(hid, w3, dec, bias, tabs["wf"], tabs["f2"], tabs["tw"])


def _pool_body(*refs, seq_len, n2_total):
    ng = len(POOL_WINDOWS)
    m_ref, p_refs, n_refs = refs[0], refs[1:1 + ng], refs[1 + ng:2 * ng]
    pw_ref, ps_ref, gp_ref, o_ref = refs[2 * ng:]
    tj, r = m_ref.shape[1], m_ref.shape[2]
    j = pl.program_id(1)
    first, last = j == 0, j == pl.num_programs(1) - 1
    j0 = j * tj
    plane = lax.broadcasted_iota(jnp.int32, (tj, r, LANES), 0)
    row = lax.broadcasted_iota(jnp.int32, (tj, r, LANES), 1)
    t = row * n2_total + j0 + plane
    outs = []
    for g, win in enumerate(POOL_WINDOWS):
        sl = slice(g * POOL_GROUP, (g + 1) * POOL_GROUP)
        half = win // 2
        main = m_ref[0, :, :, sl]
        parts = [_wrap_prev(p_refs[g][0], first), main]
        if half > 1:
            parts.append(_wrap_next(n_refs[g - 1][0, 0:half - 1], last))
        acc = jnp.concatenate(parts, axis=0)
        width = 1
        while width < win:
            n = acc.shape[0] - width
            acc = acc[:n] + acc[width:width + n]
            width *= 2
        cnt = (jnp.minimum(t + half - 1, seq_len - 1) - jnp.maximum(t - half, 0) + 1).astype(F32)
        dgrp = acc / jnp.concatenate([cnt] * (POOL_GROUP // LANES), axis=-1) - main
        outs.append(jnp.dot(dgrp.reshape(tj * r, POOL_GROUP).astype(BF16), pw_ref[g],
                            preferred_element_type=F32))
    y = jnp.concatenate(outs, axis=-1) * ps_ref[...]
    o_ref[0] = _rms(y, gp_ref[...]).astype(BF16).reshape(tj, r, POOL_CH)


def _pool(u, pool_w, pool_scale, g_pool, cfg, seq_len):
    n2, r, tj = cfg["N2"], cfg["R"], cfg["tj_pool"]
    b = u.shape[0]
    goff = 3 * HY_CH // POOL_CH

    def halo_spec(g, nxt):
        h = POOL_WINDOWS[g] // 2
        nh = n2 // h
        cb = goff * (POOL_CH // POOL_GROUP) + g
        if nxt:
            return pl.BlockSpec((1, h, r, POOL_GROUP), lambda i, j: (i, ((j + 1) * (tj // h)) % nh, 0, cb))
        return pl.BlockSpec((1, h, r, POOL_GROUP), lambda i, j: (i, (j * (tj // h) + nh - 1) % nh, 0, cb))

    ng = len(POOL_WINDOWS)
    return pl.pallas_call(
        functools.partial(_pool_body, seq_len=seq_len, n2_total=n2),
        grid=(b, n2 // tj),
        in_specs=[pl.BlockSpec((1, tj, r, POOL_CH), lambda i, j: (i, j, 0, goff))]
        + [halo_spec(g, False) for g in range(ng)] + [halo_spec(g, True) for g in range(1, ng)]
        + [_const_spec((ng, POOL_GROUP, POOL_GROUP)), _const_spec((1, POOL_CH)), _const_spec((1, POOL_CH))],
        out_specs=pl.BlockSpec((1, tj, r, POOL_CH), lambda i, j: (i, j, 0, 0)),
        out_shape=jax.ShapeDtypeStruct((b, n2, r, POOL_CH), BF16),
        compiler_params=_cparams("parallel", "arbitrary"),
        name="pool",
    )(*([u] * (2 * ng)), pool_w, pool_scale, g_pool)


def _fft_cfg(seq_len):
    n2 = 128 if seq_len >= 8192 else 64
    r = seq_len // n2
    assert r % SUBLANES == 0 and r * n2 == seq_len
    tj = 8 if seq_len >= 8192 else 32
    return dict(N2=n2, R=r, N1=2 * r, tj=tj, tk=max(tj, 16), ct=256, pitch=n2 + SUBLANES, tj_pool=16)


def _dft_tables(cfg):
    n2, r, n1, tj = cfg["N2"], cfg["R"], cfg["N1"], cfg["tj"]
    n = n1 * n2

    def cis(idx, period):
        a = -2.0 * np.pi * (idx % period) / period
        return np.cos(a), np.sin(a)

    def ri_order(m):
        pos = np.arange(2 * m)
        return (pos % (2 * SUBLANES)) // SUBLANES * m + pos // (2 * SUBLANES) * SUBLANES + pos % SUBLANES

    k1 = np.arange(n1)
    mr, mi = cis(k1[:, None] * k1[None, :], n1)
    mrd, mid = mr[:, :r], mi[:, :r]
    o1, o2 = ri_order(n1), ri_order(n2)
    w1 = np.block([[mrd, -mid], [mid, mrd]])[o1]
    wf = np.concatenate([mr, mi], axis=0)[o1]
    k2 = np.arange(n2)
    fr, fi = cis(k2[:, None] * k2[None, :], n2)
    f2 = np.block([[fr, -fi], [fi, fr]])[o2][:, o2]
    g2 = np.block([[fr, fi], [-fi, fr]])[o2][:, o2]
    twr, twi = cis(k1[:, None] * k2[None, :], n)
    tw = np.stack([twr, twi]).reshape(2, n1, n2 // tj, tj).transpose(2, 0, 1, 3)
    return dict(w1=jnp.asarray(w1, BF16), w1inv=jnp.asarray(w1.T / n, BF16), wf=jnp.asarray(wf, BF16),
                f2=jnp.asarray(f2, BF16), g2=jnp.asarray(g2, BF16), tw=jnp.asarray(tw, F32))


def _to_plane_order(x, cfg):
    b, l, d = x.shape
    xs = jnp.swapaxes(x.reshape(b, cfg["R"], cfg["N2"], d), 1, 2).reshape(b * l, d)
    return lax.optimization_barrier(xs)


def _from_plane_order(x, b, cfg):
    d = x.shape[-1]
    return jnp.swapaxes(x.reshape(b, cfg["N2"], cfg["R"], d), 1, 2).reshape(b, cfg["N2"] * cfg["R"], d)


def _trunk(x, mem, lw, g_final, depth):
    b, seq_len, d = x.shape
    cfg = _fft_cfg(seq_len)
    tabs = _dft_tables(cfg)
    n2, r, ct = cfg["N2"], cfg["R"], cfg["ct"]
    xs = _to_plane_order(x, cfg)
    memf = mem.reshape(b * N_MEM, d)
    cpb = HY_CH // ct
    for i in range(depth):
        w = {k: v[i] for k, v in lw.items() if k not in _STACKED_WEIGHTS}
        hid = _filt_hidden(w["filt_w1"], w["filt_b1"], w["filt_w2"], w["filt_b2"], w["filt_freq"], cfg, seq_len)
        w3 = jnp.transpose(w["filt_w3"].reshape(FILT_HIDDEN, 2, 2, HY_CH), (1, 2, 0, 3))
        w3hi = w3.astype(BF16)
        w3lo = (w3 - w3hi.astype(F32)).astype(BF16)
        kf = _filt_spec(hid, jnp.concatenate([w3hi, w3lo, w3hi], axis=2),
                        w["hy_decay"], w["hy_bias"].reshape(2, 1, HY_CH), tabs, cfg, seq_len)
        u = _in_proj(xs.reshape(b, n2, r, d), w["g_mix"], lw["w_in"], i, w["conv_w"], w["conv_b"].reshape(1, -1), cfg)
        z1 = _lconv(u, 2 * cpb, u, 0, kf, 0, tabs, cfg, BF16, "lconv1")
        yh = _lconv(z1, 0, u, cpb, kf, 1, tabs, cfg, F32, "lconv2")
        yp = _pool(u, w["pool_w"], w["pool_scale"].reshape(1, -1), w["g_pool"].reshape(1, -1), cfg, seq_len)
        xs = _outproj(xs, yh.reshape(b * seq_len, HY_CH), yp.reshape(b * seq_len, POOL_CH), w["g_hy"],
                      lw["w_out"], i)
        kv = _norm_matmul(memf, w["g_mem"], lw["w_kv"], i, BF16, 512, 1024, "kv_proj").reshape(b, N_MEM, 2 * d)
        xs = _attn(xs.reshape(b, seq_len, d), kv, w["g_xa"], lw["w_q"], lw["w_o"], i).reshape(b * seq_len, d)
        xs = _mlp(xs, w["g_mlp"], lw["w_up"], lw["w_down"], i, g_final, i == depth - 1)
    return _from_plane_order(xs, b, cfg)


_STACKED_WEIGHTS = ("w_in", "w_out", "w_q", "w_kv", "w_o", "w_up", "w_down")
_MATMUL_WEIGHTS = _STACKED_WEIGHTS + ("pool_w",)


def kernel(x_prompt, x_sample, mem_prompt, mem_sample, g_mix, w_in, conv_w, conv_b, filt_w1, filt_b1, filt_w2, filt_b2, filt_freq, filt_w3, hy_decay, hy_bias, pool_w, pool_scale, g_hy, g_pool, w_out, g_xa, g_mem, w_q, w_kv, w_o, g_mlp, w_up, w_down, g_final):
    lw = dict(g_mix=g_mix, w_in=w_in, conv_w=conv_w, conv_b=conv_b, filt_w1=filt_w1, filt_b1=filt_b1,
              filt_w2=filt_w2, filt_b2=filt_b2, filt_freq=filt_freq, filt_w3=filt_w3, hy_decay=hy_decay,
              hy_bias=hy_bias, pool_w=pool_w, pool_scale=pool_scale, g_hy=g_hy, g_pool=g_pool, w_out=w_out,
              g_xa=g_xa, g_mem=g_mem, w_q=w_q, w_kv=w_kv, w_o=w_o, g_mlp=g_mlp, w_up=w_up, w_down=w_down)
    lw = {k: (v.astype(BF16) if k in _MATMUL_WEIGHTS else v) for k, v in lw.items()}
    depth = g_mix.shape[0]
    y_prompt = _trunk(x_prompt, mem_prompt, lw, g_final, depth)
    y_sample = _trunk(x_sample, mem_sample, lw, g_final, depth)
    return (y_prompt, y_sample)
```

```python
import functools
import math

import numpy as np
import jax
import jax.numpy as jnp
from jax import lax
from jax.experimental import pallas as pl
from jax.experimental.pallas import tpu as pltpu

F32 = jnp.float32
BF16 = jnp.bfloat16
U32 = jnp.uint32

D_MODEL = 2048
HY_CH = 1024
POOL_CH = 1024
D_IN = 3 * HY_CH + POOL_CH
POOL_WINDOWS = (2, 4, 8, 16)
POOL_GROUP = POOL_CH // len(POOL_WINDOWS)
POS_BANDS = 16
FILT_HIDDEN = 64
N_MEM = 256
XA_HEADS = 4
XA_HEAD_DIM = D_MODEL // XA_HEADS
D_FF = 4 * D_MODEL
EPS = 1e-6

LANES = 128
SUBLANES = 8
VMEM_LIMIT_BYTES = 58 * 1024 * 1024
NORM_CHUNK_ROWS = 256
FILT_LANES = 1024
IN_PROJ_TOKENS = 1024
HY_CONV_BLOCKS = 3
HI = lax.Precision.HIGHEST


def _cparams(*sem):
    return pltpu.CompilerParams(dimension_semantics=sem, vmem_limit_bytes=VMEM_LIMIT_BYTES)


def _rms(x, g):
    return x * lax.rsqrt(jnp.mean(x * x, axis=-1, keepdims=True) + EPS) * g


def _const_spec(shape):
    nd = len(shape)
    return pl.BlockSpec(shape, lambda *_: (0,) * nd)


def _kv_proj_body(x_ref, g_ref, w_ref, o_ref, h_ref):
    @pl.when(pl.program_id(2) == 0)
    def _():
        h_ref[...] = _rms(x_ref[...], g_ref[...]).astype(BF16)

    o_ref[...] = jnp.dot(h_ref[...], w_ref[...], preferred_element_type=F32).astype(o_ref.dtype)


def _kv_proj(mem, g_mem, w_kv, tm=512, tn=1024):
    t, d = mem.shape
    depth, _, n = w_kv.shape
    return pl.pallas_call(
        _kv_proj_body,
        grid=(depth, t // tm, n // tn),
        in_specs=[pl.BlockSpec((tm, d), lambda l, i, j: (i, 0)),
                  pl.BlockSpec((None, 1, d), lambda l, i, j: (l, 0, 0)),
                  pl.BlockSpec((None, d, tn), lambda l, i, j: (l, 0, j))],
        out_specs=pl.BlockSpec((None, tm, tn), lambda l, i, j: (l, i, j)),
        out_shape=jax.ShapeDtypeStruct((depth, t, n), BF16),
        scratch_shapes=[pltpu.VMEM((tm, d), BF16)],
        compiler_params=_cparams("parallel", "parallel", "arbitrary"),
        name="kv_proj",
    )(mem, g_mem.reshape(depth, 1, d), w_kv)


def _in_proj_body(x_ref, xp_ref, xn_ref, g_ref, w_ref, cw_ref, cb_ref, o_ref, h_ref, hh_ref):
    gi, j = pl.program_id(1), pl.program_id(2)
    n2, rt, d = x_ref.shape[1:]
    tn = w_ref.shape[2]
    rows = NORM_CHUNK_ROWS
    ppc = rows // rt

    def emit(first, conv):
        if conv:
            w0, w1, w2, b = cw_ref[0:1, :], cw_ref[1:2, :], cw_ref[2:3, :], cb_ref[...]
            tap = lambda lo_, mid_, hi_: lo_ * w0 + mid_ * w1 + hi_ * w2 + b
        if first:
            hh_ref[...] = _rms(jnp.concatenate([xp_ref[0, 0], xn_ref[0, 0]], axis=0), g_ref[...]).astype(BF16)
        for c in range(n2 // ppc):
            if first:
                h = _rms(x_ref[0, c * ppc:(c + 1) * ppc].reshape(rows, d), g_ref[...]).astype(BF16)
                h_ref[c * rows:(c + 1) * rows, :] = h
            else:
                h = h_ref[c * rows:(c + 1) * rows, :]
            uc = jnp.dot(h, w_ref[j], preferred_element_type=F32).reshape(ppc, rt, tn)
            p0 = c * ppc
            if not conv:
                o_ref[0, p0:p0 + ppc] = uc
                continue
            if c == 0:
                u0, u1 = uc[0], uc[1]
            else:
                o_ref[0, p0 - 1] = tap(t2, t1, uc[0])
                o_ref[0, p0] = tap(t1, uc[0], uc[1])
            o_ref[0, p0 + 1:p0 + ppc - 1] = tap(uc[0:ppc - 2], uc[1:ppc - 1], uc[2:ppc])
            t2, t1 = uc[ppc - 2], uc[ppc - 1]
        if conv:
            uh = jnp.dot(hh_ref[...], w_ref[j], preferred_element_type=F32)
            row = lax.broadcasted_iota(jnp.int32, (rt, tn), 0)
            prev_tok = jnp.where(gi == 0, 0.0, uh[SUBLANES - 1:SUBLANES])
            next_tok = jnp.where(gi == pl.num_programs(1) - 1, 0.0, uh[SUBLANES:SUBLANES + 1])
            wrap_prev = jnp.where(row == 0, prev_tok, pltpu.roll(t1, 1, axis=0))
            wrap_next = jnp.where(row == rt - 1, next_tok, pltpu.roll(u0, rt - 1, axis=0))
            o_ref[0, 0] = tap(wrap_prev, u0, u1)
            o_ref[0, n2 - 1] = tap(t2, t1, wrap_next)

    pl.when(j == 0)(lambda: emit(True, True))
    pl.when(jnp.logical_and(j > 0, j < HY_CONV_BLOCKS))(lambda: emit(False, True))
    pl.when(j >= HY_CONV_BLOCKS)(lambda: emit(False, False))


def _in_proj(x4, g, w, layer, conv_w, conv_b, cfg):
    b, n2, r, d = x4.shape
    rt = IN_PROJ_TOKENS // n2
    tn = HY_CH
    rg = r // SUBLANES
    return pl.pallas_call(
        _in_proj_body,
        grid=(b, r // rt, D_IN // tn),
        in_specs=[pl.BlockSpec((1, n2, rt, d), lambda i, g_, j: (i, 0, g_, 0)),
                  pl.BlockSpec((1, 1, SUBLANES, d),
                               lambda i, g_, j: (i, n2 - 1, jnp.maximum(g_ * (rt // SUBLANES) - 1, 0), 0)),
                  pl.BlockSpec((1, 1, SUBLANES, d),
                               lambda i, g_, j: (i, 0, jnp.minimum((g_ + 1) * (rt // SUBLANES), rg - 1), 0)),
                  pl.BlockSpec((1, d), lambda i, g_, j: (0, 0)),
                  pl.BlockSpec((None, D_IN // tn, d, tn), lambda i, g_, j: (layer, 0, 0, 0),
                               pipeline_mode=pl.Buffered(1)),
                  pl.BlockSpec((3, tn), lambda i, g_, j: (0, jnp.minimum(j, HY_CONV_BLOCKS - 1))),
                  pl.BlockSpec((1, tn), lambda i, g_, j: (0, jnp.minimum(j, HY_CONV_BLOCKS - 1)))],
        out_specs=pl.BlockSpec((1, n2, rt, tn), lambda i, g_, j: (i, 0, g_, j)),
        out_shape=jax.ShapeDtypeStruct((b, n2, r, D_IN), F32),
        scratch_shapes=[pltpu.VMEM((n2 * rt, d), BF16), pltpu.VMEM((2 * SUBLANES, d), BF16)],
        compiler_params=_cparams("parallel", "parallel", "arbitrary"),
        name="in_proj",
    )(x4, x4, x4, g.reshape(1, d), w, conv_w, conv_b)


def _mlp_body(x_ref, g_ref, wu_ref, wd_ref, gf_ref, o_ref, h_ref, *, final_norm):
    f = pl.program_id(1)
    rows = NORM_CHUNK_ROWS

    def up_down(h):
        hm = jnp.dot(h, wu_ref[...], preferred_element_type=F32)
        a = jnp.square(jnp.maximum(hm, 0.0)).astype(BF16)
        return jnp.dot(a, wd_ref[...], preferred_element_type=F32)

    @pl.when(f == 0)
    def _():
        for c in range(x_ref.shape[0] // rows):
            sl = slice(c * rows, (c + 1) * rows)
            x = x_ref[sl, :]
            h = _rms(x, g_ref[...]).astype(BF16)
            h_ref[sl, :] = h
            o_ref[sl, :] = x + up_down(h)

    @pl.when(f != 0)
    def _():
        o_ref[...] += up_down(h_ref[...])

    if final_norm:
        @pl.when(f == pl.num_programs(1) - 1)
        def _():
            o_ref[...] = _rms(o_ref[...], gf_ref[...])


def _mlp(x, g, wu, wd, layer, g_final, final_norm, tm=512, tf=1024):
    t, d = x.shape
    ff = wu.shape[2]
    return pl.pallas_call(
        functools.partial(_mlp_body, final_norm=final_norm),
        grid=(t // tm, ff // tf),
        in_specs=[pl.BlockSpec((tm, d), lambda i, f: (i, 0)),
                  pl.BlockSpec((1, d), lambda i, f: (0, 0)),
                  pl.BlockSpec((None, d, tf), lambda i, f: (layer, 0, f)),
                  pl.BlockSpec((None, tf, d), lambda i, f: (layer, f, 0)),
                  pl.BlockSpec((1, d), lambda i, f: (0, 0))],
        out_specs=pl.BlockSpec((tm, d), lambda i, f: (i, 0)),
        out_shape=jax.ShapeDtypeStruct((t, d), F32),
        scratch_shapes=[pltpu.VMEM((tm, d), BF16)],
        compiler_params=_cparams("parallel", "arbitrary"),
        name="mlp",
    )(x, g.reshape(1, d), wu, wd, g_final.reshape(1, d))


def _attn_body(x_ref, k_ref, v_ref, g_ref, wq_ref, wo_ref, o_ref):
    scale = XA_HEAD_DIM ** -0.5
    x = x_ref[0]
    h = _rms(x, g_ref[...]).astype(BF16)
    q = jnp.dot(h, wq_ref[...], preferred_element_type=F32)
    heads = []
    for hd in range(XA_HEADS):
        sl = slice(hd * XA_HEAD_DIM, (hd + 1) * XA_HEAD_DIM)
        s = lax.dot_general(q[:, sl].astype(BF16), k_ref[0, :, sl], (((1,), (1,)), ((), ())),
                            preferred_element_type=F32) * scale
        p = jnp.exp(s - jnp.max(s, axis=-1, keepdims=True))
        a = (p / jnp.sum(p, axis=-1, keepdims=True)).astype(BF16)
        heads.append(jnp.dot(a, v_ref[0, :, sl], preferred_element_type=F32).astype(BF16))
    o = jnp.concatenate(heads, axis=-1)
    o_ref[0] = x + jnp.dot(o, wo_ref[...], preferred_element_type=F32)


def _attn(x, kv, kv_row0, g, wq, wo, layer, tq=512):
    b, l, d = x.shape
    return pl.pallas_call(
        _attn_body,
        grid=(b, l // tq),
        in_specs=[pl.BlockSpec((1, tq, d), lambda i, j: (i, j, 0)),
                  pl.BlockSpec((None, 1, N_MEM, d), lambda i, j: (layer, kv_row0 + i, 0, 0)),
                  pl.BlockSpec((None, 1, N_MEM, d), lambda i, j: (layer, kv_row0 + i, 0, 1)),
                  pl.BlockSpec((1, d), lambda i, j: (0, 0)),
                  pl.BlockSpec((None, d, d), lambda i, j: (layer, 0, 0)),
                  pl.BlockSpec((None, d, d), lambda i, j: (layer, 0, 0))],
        out_specs=pl.BlockSpec((1, tq, d), lambda i, j: (i, j, 0)),
        out_shape=jax.ShapeDtypeStruct((b, l, d), F32),
        compiler_params=_cparams("parallel", "arbitrary"),
        name="attn",
    )(x, kv, kv, g.reshape(1, d), wq, wo)


def _outproj_body(x_ref, yh_ref, yp_ref, gh_ref, wt_ref, wb_ref, o_ref):
    yh = _rms(yh_ref[...], gh_ref[...]).astype(BF16)
    acc = jnp.dot(yh, wt_ref[...], preferred_element_type=F32)
    acc += jnp.dot(yp_ref[...], wb_ref[...], preferred_element_type=F32)
    o_ref[...] = x_ref[...] + acc


def _outproj(x, yh, yp, g_hy, w_out, layer, tm=512):
    t, d = x.shape
    return pl.pallas_call(
        _outproj_body,
        grid=(t // tm,),
        in_specs=[pl.BlockSpec((tm, d), lambda i: (i, 0)),
                  pl.BlockSpec((tm, HY_CH), lambda i: (i, 0)),
                  pl.BlockSpec((tm, POOL_CH), lambda i: (i, 0)),
                  pl.BlockSpec((1, HY_CH), lambda i: (0, 0)),
                  pl.BlockSpec((None, HY_CH, d), lambda i: (layer, 0, 0)),
                  pl.BlockSpec((None, POOL_CH, d), lambda i: (layer, 1, 0))],
        out_specs=pl.BlockSpec((tm, d), lambda i: (i, 0)),
        out_shape=jax.ShapeDtypeStruct((t, d), F32),
        compiler_params=_cparams("parallel"),
        name="outproj",
    )(x, yh, yp, g_hy.reshape(1, HY_CH), w_out, w_out)


def _shift_rows(x, down):
    r = x.shape[-2]
    rows = lax.broadcasted_iota(jnp.int32, x.shape, x.ndim - 2)
    if down:
        return jnp.where(rows == 0, 0.0, pltpu.roll(x, 1, axis=x.ndim - 2))
    return jnp.where(rows == r - 1, 0.0, pltpu.roll(x, r - 1, axis=x.ndim - 2))


def _wrap_prev(x, is_first):
    return jnp.where(is_first, _shift_rows(x, True), x)


def _wrap_next(x, is_last):
    return jnp.where(is_last, _shift_rows(x, False), x)


def _pack(re, im):
    half = jnp.uint32(0x8000)
    rb = (lax.bitcast_convert_type(re, U32) + half) & jnp.uint32(0xFFFF0000)
    return rb | ((lax.bitcast_convert_type(im, U32) + half) >> 16)


def _unpack(p):
    return (lax.bitcast_convert_type(p & jnp.uint32(0xFFFF0000), F32),
            lax.bitcast_convert_type(p << 16, F32))


def _split_ri(x):
    m, c = x.shape
    x4 = x.reshape(m // (2 * SUBLANES), 2, SUBLANES, c)
    return x4[:, 0].reshape(m // 2, c), x4[:, 1].reshape(m // 2, c)


def _join_ri(re, im):
    n, c = re.shape
    g = n // SUBLANES
    return jnp.stack([re.reshape(g, SUBLANES, c), im.reshape(g, SUBLANES, c)], axis=1).reshape(2 * n, c)


def _store_cols(scr, n2i, n1, pitch, val):
    for l in range(scr.shape[0]):
        scr[l, pl.ds(n2i, n1, stride=pitch), :] = val[:, l * LANES:(l + 1) * LANES]


def _load_cols(scr, n2i, n1, pitch):
    return jnp.concatenate([scr[l, pl.ds(n2i, n1, stride=pitch), :] for l in range(scr.shape[0])], axis=1)


def _store_rows(scr, base, n2, val):
    for l in range(scr.shape[0]):
        scr[l, pl.ds(base, n2), :] = val[:, l * LANES:(l + 1) * LANES]


def _load_rows(scr, base, n2):
    return jnp.concatenate([scr[l, pl.ds(base, n2), :] for l in range(scr.shape[0])], axis=1)


def _stage1_to_scratch(x_bf16, w_ref, tw_ref, jj, scr, n2i, n1, pitch):
    res = jnp.dot(w_ref[...], x_bf16, preferred_element_type=F32)
    ar, ai = _split_ri(res)
    twr, twi = tw_ref[0, 0, :, jj:jj + 1], tw_ref[0, 1, :, jj:jj + 1]
    _store_cols(scr, n2i, n1, pitch, _pack(ar * twr - ai * twi, ar * twi + ai * twr))


def _stage2_fwd(scr, base, n2, f_ref):
    xr, xi = _unpack(_load_rows(scr, base, n2))
    s = jnp.dot(f_ref[...], _join_ri(xr, xi).astype(BF16), preferred_element_type=F32)
    return _split_ri(s)


def _lconv_body(vm_ref, gm_ref, kf_ref, w_ref, wi_ref, f_ref, g_ref, tw_ref, o_ref, scr,
                *, n2, n1, pitch, tj, tk, s1, s2):
    s = pl.program_id(2)

    @pl.when(s < s1)
    def _():
        for jj in range(tj):
            v = vm_ref[:, jj]
            nb, r, ct = v.shape
            _stage1_to_scratch(v.reshape(nb * r, ct).astype(BF16), w_ref, tw_ref, jj, scr,
                               s * tj + jj, n1, pitch)

    @pl.when(jnp.logical_and(s >= s1, s < s1 + s2))
    def _():
        for kk in range(tk):
            base = pl.multiple_of(((s - s1) * tk + kk) * pitch, SUBLANES)
            sr, si = _stage2_fwd(scr, base, n2, f_ref)
            kr, ki = _unpack(kf_ref[0, kk])
            y = _join_ri(sr * kr - si * ki, sr * ki + si * kr).astype(BF16)
            a = jnp.dot(g_ref[...], y, preferred_element_type=F32)
            _store_rows(scr, base, n2, _pack(*_split_ri(a)))

    @pl.when(s >= s1 + s2)
    def _():
        jp = s - s1 - s2
        for jj in range(tj):
            gate = gm_ref[:, jj]
            ar, ai = _unpack(_load_cols(scr, jp * tj + jj, n1, pitch))
            twr, twi = tw_ref[0, 0, :, jj:jj + 1], tw_ref[0, 1, :, jj:jj + 1]
            x = _join_ri(ar * twr + ai * twi, ai * twr - ar * twi).astype(BF16)
            y = jnp.dot(wi_ref[...], x, preferred_element_type=F32)
            o_ref[:, jj] = (gate * y.reshape(gate.shape)).astype(o_ref.dtype)


def _lconv(vsrc, v_goff, u, g_goff, kf, order, tabs, cfg, out_dtype, name):
    n2, r, n1, tj, tk, ct, pitch = (cfg[k] for k in ("N2", "R", "N1", "tj", "tk", "ct", "pitch"))
    p = vsrc.shape[0] // 2
    s1, s2 = n2 // tj, n1 // tk

    def jv(s):
        return jnp.minimum(s, s1 - 1)

    def jg(s):
        return jnp.maximum(s - s1 - s2, 0)

    def jt(s):
        return jnp.where(s < s1, s, jg(s))

    def plane_spec(goff, jf):
        return pl.BlockSpec((2, tj, r, ct), lambda p_, c, s: (p_, jf(s), 0, goff + c))

    in_specs = [plane_spec(v_goff, jv), plane_spec(g_goff, jg)]
    args = [vsrc, u]
    in_specs += [pl.BlockSpec((1, tk, n2, ct), lambda p_, c, s: (order, jnp.clip(s - s1, 0, s2 - 1), 0, c)),
                 _const_spec((2 * n1, 2 * r)), _const_spec((2 * r, 2 * n1)),
                 _const_spec((2 * n2, 2 * n2)), _const_spec((2 * n2, 2 * n2)),
                 pl.BlockSpec((1, 2, n1, tj), lambda p_, c, s: (jt(s), 0, 0, 0))]
    args += [kf, tabs["w1"], tabs["w1inv"], tabs["f2"], tabs["g2"], tabs["tw"]]
    return pl.pallas_call(
        functools.partial(_lconv_body, n2=n2, n1=n1, pitch=pitch, tj=tj, tk=tk, s1=s1, s2=s2),
        grid=(p, HY_CH // ct, 2 * s1 + s2),
        in_specs=in_specs,
        out_specs=pl.BlockSpec((2, tj, r, ct), lambda p_, c, s: (p_, jg(s), 0, c)),
        out_shape=jax.ShapeDtypeStruct((2 * p, n2, r, HY_CH), out_dtype),
        scratch_shapes=[pltpu.VMEM((ct // LANES, n1 * pitch, LANES), U32)],
        compiler_params=_cparams("parallel", "arbitrary", "arbitrary"),
        name=name,
    )(*args)


def _filt_pos_half(shape, n2, n2_total, d):
    row = lax.broadcasted_iota(jnp.int32, shape, 0)
    if d == 0:
        return (row * n2_total + n2).astype(F32)
    pos_b = (shape[0] - row) * n2_total - n2
    return jnp.where(jnp.logical_and(row == 0, n2 == 0), 0, pos_b).astype(F32)


def _filt_hidden_body(bands_ref, w1t_ref, w1c_ref, w1s_ref, b1_ref, w2_ref, b2_ref, fr_ref, h_ref,
                      *, seq_len, n2_total):
    tjh, n1 = h_ref.shape[0], h_ref.shape[1]
    lanes = tjh * n1
    idx = lax.broadcasted_iota(jnp.int32, (1, lanes), 1)
    row = idx & (n1 - 1)
    n2 = pl.program_id(0) * tjh + (idx >> int(math.log2(n1)))
    pos_b = jnp.where(jnp.logical_and(row == n1 // 2, n2 == 0), 0, (n1 - row) * n2_total - n2)
    pos = jnp.where(row < n1 // 2, row * n2_total + n2, pos_b).astype(F32)
    t = pos / float(max(seq_len - 1, 1))
    ang = bands_ref[...] * ((2.0 * math.pi / seq_len) * pos)
    pre = (w1t_ref[...] * t + jnp.dot(w1c_ref[...], jnp.cos(ang), precision=HI, preferred_element_type=F32)
           + jnp.dot(w1s_ref[...], -jnp.sin(ang), precision=HI, preferred_element_type=F32) + b1_ref[...])
    h = jnp.sin(fr_ref[...] * pre)
    h = jnp.sin(fr_ref[...] * (jnp.dot(w2_ref[...], h, precision=HI, preferred_element_type=F32) + b2_ref[...]))
    h = h.T
    hi = h.astype(BF16)
    lo = (h - hi.astype(F32)).astype(BF16)
    h_ref[...] = jnp.concatenate([hi, hi, lo], axis=1).reshape(h_ref.shape)


def _filt_hidden(w1, b1, w2, b2, freq, cfg, seq_len):
    n2, n1 = cfg["N2"], cfg["N1"]
    tjh = FILT_LANES // n1
    bands = np.linspace(1e-4, POS_BANDS - 1, POS_BANDS, dtype=np.float32).reshape(POS_BANDS, 1)
    col = lambda v: v.reshape(FILT_HIDDEN, 1)
    args = (jnp.asarray(bands), col(w1[0]), w1[1:1 + POS_BANDS].T, w1[1 + POS_BANDS:].T, col(b1), w2.T, col(b2),
            col(freq))
    return pl.pallas_call(
        functools.partial(_filt_hidden_body, seq_len=seq_len, n2_total=n2),
        grid=(n2 // tjh,),
        in_specs=[_const_spec(a.shape) for a in args],
        out_specs=pl.BlockSpec((tjh, n1, 3 * FILT_HIDDEN), lambda j: (j, 0, 0)),
        out_shape=jax.ShapeDtypeStruct((n2, n1, 3 * FILT_HIDDEN), BF16),
        compiler_params=_cparams("parallel"),
        name="filt_hidden",
    )(*args)


def _filt_spec_body(h_ref, w3_ref, dec_ref, bias_ref, wf_ref, f_ref, tw_ref, o_ref, scr, nrm_ref,
                    *, seq_len, n2, n1, pitch, tj, tk, s1):
    s = pl.program_id(2)
    r = n1 // 2
    ct = o_ref.shape[-1]

    @pl.when(s == 0)
    def _():
        nrm_ref[...] = jnp.zeros_like(nrm_ref)

    @pl.when(s < s1)
    def _():
        row = lax.broadcasted_iota(jnp.int32, (r, ct), 0)
        for jj in range(tj):
            n2i = s * tj + jj
            halves = []
            for d in range(2):
                val = jnp.dot(h_ref[jj, d * r:(d + 1) * r, :], w3_ref[0, d], preferred_element_type=F32)
                t = _filt_pos_half((r, ct), n2i, n2, d) / float(max(seq_len - 1, 1))
                val = val * jnp.exp(-t * jnp.abs(dec_ref[0, d:d + 1, :]))
                nrm_ref[...] += jnp.sum(jnp.abs(val), axis=0, keepdims=True)
                if d == 1:
                    val = jnp.where(jnp.logical_and(row == 0, n2i == 0), 0.0, val)
                halves.append(val)
            _stage1_to_scratch(jnp.concatenate(halves, axis=0).astype(BF16), wf_ref, tw_ref, jj, scr,
                               n2i, n1, pitch)

    @pl.when(s >= s1)
    def _():
        inv = 1.0 / nrm_ref[...]
        bias = bias_ref[0]
        for kk in range(tk):
            base = pl.multiple_of(((s - s1) * tk + kk) * pitch, SUBLANES)
            sr, si = _stage2_fwd(scr, base, n2, f_ref)
            o_ref[0, kk] = _pack(sr * inv + bias, si * inv)


def _filt_spec(hid, w3, dec, bias, tabs, cfg, seq_len):
    n2, n1, tj, tk, ct, pitch = (cfg[k] for k in ("N2", "N1", "tj", "tk", "ct", "pitch"))
    s1, s2 = n2 // tj, n1 // tk
    return pl.pallas_call(
        functools.partial(_filt_spec_body, seq_len=seq_len, n2=n2, n1=n1, pitch=pitch, tj=tj, tk=tk, s1=s1),
        grid=(2, HY_CH // ct, s1 + s2),
        in_specs=[pl.BlockSpec((tj, n1, 3 * FILT_HIDDEN), lambda o, c, s: (jnp.minimum(s, s1 - 1), 0, 0)),
                  pl.BlockSpec((1, 2, 3 * FILT_HIDDEN, ct), lambda o, c, s: (o, 0, 0, c)),
                  pl.BlockSpec((1, 2, ct), lambda o, c, s: (o, 0, c)),
                  pl.BlockSpec((1, 1, ct), lambda o, c, s: (o, 0, c)),
                  _const_spec((2 * n1, n1)), _const_spec((2 * n2, 2 * n2)),
                  pl.BlockSpec((1, 2, n1, tj), lambda o, c, s: (jnp.minimum(s, s1 - 1), 0, 0, 0))],
        out_specs=pl.BlockSpec((1, tk, n2, ct), lambda o, c, s: (o, jnp.maximum(s - s1, 0), 0, c)),
        out_shape=jax.ShapeDtypeStruct((2, n1, n2, HY_CH), U32),
        scratch_shapes=[pltpu.VMEM((ct // LANES, n1 * pitch, LANES), U32), pltpu.VMEM((1, ct), F32)],
        compiler_params=_cparams("parallel", "arbitrary", "arbitrary"),
        name="filt_spec",
    )(hid, w3, dec, bias, tabs["wf"], tabs["f2"], tabs["tw"])


def _pool_body(*refs, seq_len, n2_total):
    ng = len(POOL_WINDOWS)
    m_ref, p_refs, n_refs = refs[0], refs[1:1 + ng], refs[1 + ng:2 * ng]
    pw_ref, ps_ref, gp_ref, o_ref = refs[2 * ng:]
    tj, r = m_ref.shape[1], m_ref.shape[2]
    j = pl.program_id(1)
    first, last = j == 0, j == pl.num_programs(1) - 1
    j0 = j * tj
    plane = lax.broadcasted_iota(jnp.int32, (tj, r, LANES), 0)
    row = lax.broadcasted_iota(jnp.int32, (tj, r, LANES), 1)
    t = row * n2_total + j0 + plane
    outs = []
    for g, win in enumerate(POOL_WINDOWS):
        sl = slice(g * POOL_GROUP, (g + 1) * POOL_GROUP)
        half = win // 2
        main = m_ref[0, :, :, sl]
        parts = [_wrap_prev(p_refs[g][0], first), main]
        if half > 1:
            parts.append(_wrap_next(n_refs[g - 1][0, 0:half - 1], last))
        acc = jnp.concatenate(parts, axis=0)
        width = 1
        while width < win:
            n = acc.shape[0] - width
            acc = acc[:n] + acc[width:width + n]
            width *= 2
        cnt = (jnp.minimum(t + half - 1, seq_len - 1) - jnp.maximum(t - half, 0) + 1).astype(F32)
        dgrp = acc / jnp.concatenate([cnt] * (POOL_GROUP // LANES), axis=-1) - main
        outs.append(jnp.dot(dgrp.reshape(tj * r, POOL_GROUP).astype(BF16), pw_ref[g],
                            preferred_element_type=F32))
    y = jnp.concatenate(outs, axis=-1) * ps_ref[...]
    o_ref[0] = _rms(y, gp_ref[...]).astype(BF16).reshape(tj, r, POOL_CH)


def _pool(u, pool_w, pool_scale, g_pool, cfg, seq_len):
    n2, r, tj = cfg["N2"], cfg["R"], cfg["tj_pool"]
    b = u.shape[0]
    goff = 3 * HY_CH // POOL_CH

    def halo_spec(g, nxt):
        h = POOL_WINDOWS[g] // 2
        nh = n2 // h
        cb = goff * (POOL_CH // POOL_GROUP) + g
        if nxt:
            return pl.BlockSpec((1, h, r, POOL_GROUP), lambda i, j: (i, ((j + 1) * (tj // h)) % nh, 0, cb))
        return pl.BlockSpec((1, h, r, POOL_GROUP), lambda i, j: (i, (j * (tj // h) + nh - 1) % nh, 0, cb))

    ng = len(POOL_WINDOWS)
    return pl.pallas_call(
        functools.partial(_pool_body, seq_len=seq_len, n2_total=n2),
        grid=(b, n2 // tj),
        in_specs=[pl.BlockSpec((1, tj, r, POOL_CH), lambda i, j: (i, j, 0, goff))]
        + [halo_spec(g, False) for g in range(ng)] + [halo_spec(g, True) for g in range(1, ng)]
        + [_const_spec((ng, POOL_GROUP, POOL_GROUP)), _const_spec((1, POOL_CH)), _const_spec((1, POOL_CH))],
        out_specs=pl.BlockSpec((1, tj, r, POOL_CH), lambda i, j: (i, j, 0, 0)),
        out_shape=jax.ShapeDtypeStruct((b, n2, r, POOL_CH), BF16),
        compiler_params=_cparams("parallel", "arbitrary"),
        name="pool",
    )(*([u] * (2 * ng)), pool_w, pool_scale, g_pool)


def _fft_cfg(seq_len):
    n2 = 128 if seq_len >= 8192 else 64
    r = seq_len // n2
    assert r % SUBLANES == 0 and r * n2 == seq_len
    tj = 8 if seq_len >= 8192 else 32
    return dict(N2=n2, R=r, N1=2 * r, tj=tj, tk=max(tj, 16), ct=256, pitch=n2 + SUBLANES, tj_pool=16)


def _dft_tables(cfg):
    n2, r, n1, tj = cfg["N2"], cfg["R"], cfg["N1"], cfg["tj"]
    n = n1 * n2

    def cis(idx, period):
        a = -2.0 * np.pi * (idx % period) / period
        return np.cos(a), np.sin(a)

    def ri_order(m):
        pos = np.arange(2 * m)
        return (pos % (2 * SUBLANES)) // SUBLANES * m + pos // (2 * SUBLANES) * SUBLANES + pos % SUBLANES

    k1 = np.arange(n1)
    mr, mi = cis(k1[:, None] * k1[None, :], n1)
    mrd, mid = mr[:, :r], mi[:, :r]
    o1, o2 = ri_order(n1), ri_order(n2)
    w1 = np.block([[mrd, -mid], [mid, mrd]])[o1]
    wf = np.concatenate([mr, mi], axis=0)[o1]
    k2 = np.arange(n2)
    fr, fi = cis(k2[:, None] * k2[None, :], n2)
    f2 = np.block([[fr, -fi], [fi, fr]])[o2][:, o2]
    g2 = np.block([[fr, fi], [-fi, fr]])[o2][:, o2]
    twr, twi = cis(k1[:, None] * k2[None, :], n)
    tw = np.stack([twr, twi]).reshape(2, n1, n2 // tj, tj).transpose(2, 0, 1, 3)
    return dict(w1=jnp.asarray(w1, BF16), w1inv=jnp.asarray(w1.T / n, BF16), wf=jnp.asarray(wf, BF16),
                f2=jnp.asarray(f2, BF16), g2=jnp.asarray(g2, BF16), tw=jnp.asarray(tw, F32))


def _to_plane_order(x, cfg):
    b, l, d = x.shape
    xs = jnp.swapaxes(x.reshape(b, cfg["R"], cfg["N2"], d), 1, 2).reshape(b * l, d)
    return lax.optimization_barrier(xs)


def _from_plane_order(x, b, cfg):
    d = x.shape[-1]
    return jnp.swapaxes(x.reshape(b, cfg["N2"], cfg["R"], d), 1, 2).reshape(b, cfg["N2"] * cfg["R"], d)


def _trunk(x, kv, kv_row0, lw, g_final, depth):
    b, seq_len, d = x.shape
    cfg = _fft_cfg(seq_len)
    tabs = _dft_tables(cfg)
    n2, r, ct = cfg["N2"], cfg["R"], cfg["ct"]
    xs = _to_plane_order(x, cfg)
    cpb = HY_CH // ct
    for i in range(depth):
        w = {k: v[i] for k, v in lw.items() if k not in _STACKED_WEIGHTS}
        hid = _filt_hidden(w["filt_w1"], w["filt_b1"], w["filt_w2"], w["filt_b2"], w["filt_freq"], cfg, seq_len)
        w3 = jnp.transpose(w["filt_w3"].reshape(FILT_HIDDEN, 2, 2, HY_CH), (1, 2, 0, 3))
        w3hi = w3.astype(BF16)
        w3lo = (w3 - w3hi.astype(F32)).astype(BF16)
        kf = _filt_spec(hid, jnp.concatenate([w3hi, w3lo, w3hi], axis=2),
                        w["hy_decay"], w["hy_bias"].reshape(2, 1, HY_CH), tabs, cfg, seq_len)
        u = _in_proj(xs.reshape(b, n2, r, d), w["g_mix"], lw["w_in"], i, w["conv_w"], w["conv_b"].reshape(1, -1), cfg)
        z1 = _lconv(u, 2 * cpb, u, 0, kf, 0, tabs, cfg, BF16, "lconv1")
        yh = _lconv(z1, 0, u, cpb, kf, 1, tabs, cfg, F32, "lconv2")
        yp = _pool(u, w["pool_w"], w["pool_scale"].reshape(1, -1), w["g_pool"].reshape(1, -1), cfg, seq_len)
        xs = _outproj(xs, yh.reshape(b * seq_len, HY_CH), yp.reshape(b * seq_len, POOL_CH), w["g_hy"],
                      lw["w_out"], i)
        xs = _attn(xs.reshape(b, seq_len, d), kv, kv_row0, w["g_xa"], lw["w_q"], lw["w_o"], i).reshape(b * seq_len, d)
        xs = _mlp(xs, w["g_mlp"], lw["w_up"], lw["w_down"], i, g_final, i == depth - 1)
    return _from_plane_order(xs, b, cfg)


_STACKED_WEIGHTS = ("w_in", "w_out", "w_q", "w_kv", "w_o", "w_up", "w_down")
_MATMUL_WEIGHTS = _STACKED_WEIGHTS + ("pool_w",)


def kernel(x_prompt, x_sample, mem_prompt, mem_sample, g_mix, w_in, conv_w, conv_b, filt_w1, filt_b1, filt_w2, filt_b2, filt_freq, filt_w3, hy_decay, hy_bias, pool_w, pool_scale, g_hy, g_pool, w_out, g_xa, g_mem, w_q, w_kv, w_o, g_mlp, w_up, w_down, g_final):
    lw = dict(g_mix=g_mix, w_in=w_in, conv_w=conv_w, conv_b=conv_b, filt_w1=filt_w1, filt_b1=filt_b1,
              filt_w2=filt_w2, filt_b2=filt_b2, filt_freq=filt_freq, filt_w3=filt_w3, hy_decay=hy_decay,
              hy_bias=hy_bias, pool_w=pool_w, pool_scale=pool_scale, g_hy=g_hy, g_pool=g_pool, w_out=w_out,
              g_xa=g_xa, g_mem=g_mem, w_q=w_q, w_kv=w_kv, w_o=w_o, g_mlp=g_mlp, w_up=w_up, w_down=w_down)
    lw = {k: (v.astype(BF16) if k in _MATMUL_WEIGHTS else v) for k, v in lw.items()}
    depth = g_mix.shape[0]
    lw["w_in"] = jnp.transpose(lw["w_in"].reshape(depth, D_MODEL, D_IN // HY_CH, HY_CH), (0, 2, 1, 3))
    mem = jnp.concatenate([mem_prompt, mem_sample], axis=0)
    kv = _kv_proj(mem.reshape(-1, D_MODEL), lw["g_mem"], lw["w_kv"]).reshape(depth, -1, N_MEM, 2 * D_MODEL)
    y_prompt = _trunk(x_prompt, kv, 0, lw, g_final, depth)
    y_sample = _trunk(x_sample, kv, mem_prompt.shape[0], lw, g_final, depth)
    return (y_prompt, y_sample)
```

```python
import functools
import math

import numpy as np
import jax
import jax.numpy as jnp
from jax import lax
from jax.experimental import pallas as pl
from jax.experimental.pallas import tpu as pltpu

F32 = jnp.float32
BF16 = jnp.bfloat16
U32 = jnp.uint32

D_MODEL = 2048
HY_CH = 1024
POOL_CH = 1024
D_IN = 3 * HY_CH + POOL_CH
POOL_WINDOWS = (2, 4, 8, 16)
POOL_GROUP = POOL_CH // len(POOL_WINDOWS)
POS_BANDS = 16
FILT_HIDDEN = 64
N_MEM = 256
XA_HEADS = 4
XA_HEAD_DIM = D_MODEL // XA_HEADS
D_FF = 4 * D_MODEL
EPS = 1e-6

LANES = 128
SUBLANES = 8
VMEM_LIMIT_BYTES = 58 * 1024 * 1024
NORM_CHUNK_ROWS = 256
FILT_LANES = 1024
IN_PROJ_TOKENS = 1024
HY_CONV_BLOCKS = 3
HI = lax.Precision.HIGHEST


def _cparams(*sem):
    return pltpu.CompilerParams(dimension_semantics=sem, vmem_limit_bytes=VMEM_LIMIT_BYTES)


def _rms(x, g):
    return x * lax.rsqrt(jnp.mean(x * x, axis=-1, keepdims=True) + EPS) * g


def _const_spec(shape):
    nd = len(shape)
    return pl.BlockSpec(shape, lambda *_: (0,) * nd)


def _kv_proj_body(x_ref, g_ref, w_ref, o_ref, h_ref):
    @pl.when(pl.program_id(2) == 0)
    def _():
        h_ref[...] = _rms(x_ref[...], g_ref[...]).astype(BF16)

    o_ref[...] = jnp.dot(h_ref[...], w_ref[...], preferred_element_type=F32).astype(o_ref.dtype)


def _kv_proj(mem, g_mem, w_kv, tm=512, tn=1024):
    t, d = mem.shape
    depth, _, n = w_kv.shape
    return pl.pallas_call(
        _kv_proj_body,
        grid=(depth, t // tm, n // tn),
        in_specs=[pl.BlockSpec((tm, d), lambda l, i, j: (i, 0)),
                  pl.BlockSpec((None, 1, d), lambda l, i, j: (l, 0, 0)),
                  pl.BlockSpec((None, d, tn), lambda l, i, j: (l, 0, j))],
        out_specs=pl.BlockSpec((None, tm, tn), lambda l, i, j: (l, i, j)),
        out_shape=jax.ShapeDtypeStruct((depth, t, n), BF16),
        scratch_shapes=[pltpu.VMEM((tm, d), BF16)],
        compiler_params=_cparams("parallel", "parallel", "arbitrary"),
        name="kv_proj",
    )(mem, g_mem.reshape(depth, 1, d), w_kv)


def _in_proj_body(x_ref, xp_ref, xn_ref, g_ref, w_ref, cw_ref, cb_ref, o_ref, h_ref, hh_ref):
    gi, j = pl.program_id(1), pl.program_id(2)
    n2, rt, d = x_ref.shape[1:]
    tn = o_ref.shape[3]
    rows = NORM_CHUNK_ROWS
    ppc = rows // rt

    def emit(blk):
        first, conv = blk == 0, blk < HY_CONV_BLOCKS
        w_blk = w_ref.at[:, blk * tn:(blk + 1) * tn]
        if conv:
            w0, w1, w2, b = cw_ref[0:1, :], cw_ref[1:2, :], cw_ref[2:3, :], cb_ref[...]
            tap = lambda lo_, mid_, hi_: lo_ * w0 + mid_ * w1 + hi_ * w2 + b
        if first:
            hh_ref[...] = _rms(jnp.concatenate([xp_ref[0, 0], xn_ref[0, 0]], axis=0), g_ref[...]).astype(BF16)
        for c in range(n2 // ppc):
            if first:
                h = _rms(x_ref[0, c * ppc:(c + 1) * ppc].reshape(rows, d), g_ref[...]).astype(BF16)
                h_ref[c * rows:(c + 1) * rows, :] = h
            else:
                h = h_ref[c * rows:(c + 1) * rows, :]
            uc = jnp.dot(h, w_blk[...], preferred_element_type=F32).reshape(ppc, rt, tn)
            p0 = c * ppc
            if not conv:
                o_ref[0, p0:p0 + ppc] = uc
                continue
            if c == 0:
                u0, u1 = uc[0], uc[1]
            else:
                o_ref[0, p0 - 1] = tap(t2, t1, uc[0])
                o_ref[0, p0] = tap(t1, uc[0], uc[1])
            o_ref[0, p0 + 1:p0 + ppc - 1] = tap(uc[0:ppc - 2], uc[1:ppc - 1], uc[2:ppc])
            t2, t1 = uc[ppc - 2], uc[ppc - 1]
        if conv:
            uh = jnp.dot(hh_ref[...], w_blk[...], preferred_element_type=F32)
            row = lax.broadcasted_iota(jnp.int32, (rt, tn), 0)
            prev_tok = jnp.where(gi == 0, 0.0, uh[SUBLANES - 1:SUBLANES])
            next_tok = jnp.where(gi == pl.num_programs(1) - 1, 0.0, uh[SUBLANES:SUBLANES + 1])
            wrap_prev = jnp.where(row == 0, prev_tok, pltpu.roll(t1, 1, axis=0))
            wrap_next = jnp.where(row == rt - 1, next_tok, pltpu.roll(u0, rt - 1, axis=0))
            o_ref[0, 0] = tap(wrap_prev, u0, u1)
            o_ref[0, n2 - 1] = tap(t2, t1, wrap_next)

    for blk in range(w_ref.shape[1] // tn):
        pl.when(j == blk)(functools.partial(emit, blk))


def _in_proj(x4, g, w, layer, conv_w, conv_b, cfg):
    b, n2, r, d = x4.shape
    rt = IN_PROJ_TOKENS // n2
    tn = HY_CH
    rg = r // SUBLANES
    return pl.pallas_call(
        _in_proj_body,
        grid=(b, r // rt, D_IN // tn),
        in_specs=[pl.BlockSpec((1, n2, rt, d), lambda i, g_, j: (i, 0, g_, 0)),
                  pl.BlockSpec((1, 1, SUBLANES, d),
                               lambda i, g_, j: (i, n2 - 1, jnp.maximum(g_ * (rt // SUBLANES) - 1, 0), 0)),
                  pl.BlockSpec((1, 1, SUBLANES, d),
                               lambda i, g_, j: (i, 0, jnp.minimum((g_ + 1) * (rt // SUBLANES), rg - 1), 0)),
                  pl.BlockSpec((1, d), lambda i, g_, j: (0, 0)),
                  pl.BlockSpec((None, d, D_IN), lambda i, g_, j: (layer, 0, 0),
                               pipeline_mode=pl.Buffered(1)),
                  pl.BlockSpec((3, tn), lambda i, g_, j: (0, jnp.minimum(j, HY_CONV_BLOCKS - 1))),
                  pl.BlockSpec((1, tn), lambda i, g_, j: (0, jnp.minimum(j, HY_CONV_BLOCKS - 1)))],
        out_specs=pl.BlockSpec((1, n2, rt, tn), lambda i, g_, j: (i, 0, g_, j)),
        out_shape=jax.ShapeDtypeStruct((b, n2, r, D_IN), F32),
        scratch_shapes=[pltpu.VMEM((n2 * rt, d), BF16), pltpu.VMEM((2 * SUBLANES, d), BF16)],
        compiler_params=_cparams("parallel", "parallel", "arbitrary"),
        name="in_proj",
    )(x4, x4, x4, g.reshape(1, d), w, conv_w, conv_b)


def _mlp_body(x_ref, g_ref, wu_ref, wd_ref, gf_ref, o_ref, h_ref, *, final_norm):
    f = pl.program_id(1)
    rows = NORM_CHUNK_ROWS

    def up_down(h):
        hm = jnp.dot(h, wu_ref[...], preferred_element_type=F32)
        a = jnp.square(jnp.maximum(hm, 0.0)).astype(BF16)
        return jnp.dot(a, wd_ref[...], preferred_element_type=F32)

    @pl.when(f == 0)
    def _():
        for c in range(x_ref.shape[0] // rows):
            sl = slice(c * rows, (c + 1) * rows)
            x = x_ref[sl, :]
            h = _rms(x, g_ref[...]).astype(BF16)
            h_ref[sl, :] = h
            o_ref[sl, :] = x + up_down(h)

    @pl.when(f != 0)
    def _():
        o_ref[...] += up_down(h_ref[...])

    if final_norm:
        @pl.when(f == pl.num_programs(1) - 1)
        def _():
            o_ref[...] = _rms(o_ref[...], gf_ref[...])


def _mlp(x, g, wu, wd, layer, g_final, final_norm, tm=512, tf=1024):
    t, d = x.shape
    ff = wu.shape[2]
    return pl.pallas_call(
        functools.partial(_mlp_body, final_norm=final_norm),
        grid=(t // tm, ff // tf),
        in_specs=[pl.BlockSpec((tm, d), lambda i, f: (i, 0)),
                  pl.BlockSpec((1, d), lambda i, f: (0, 0)),
                  pl.BlockSpec((None, d, tf), lambda i, f: (layer, 0, f)),
                  pl.BlockSpec((None, tf, d), lambda i, f: (layer, f, 0)),
                  pl.BlockSpec((1, d), lambda i, f: (0, 0))],
        out_specs=pl.BlockSpec((tm, d), lambda i, f: (i, 0)),
        out_shape=jax.ShapeDtypeStruct((t, d), F32),
        scratch_shapes=[pltpu.VMEM((tm, d), BF16)],
        compiler_params=_cparams("parallel", "arbitrary"),
        name="mlp",
    )(x, g.reshape(1, d), wu, wd, g_final.reshape(1, d))


def _attn_body(x_ref, k_ref, v_ref, g_ref, wq_ref, wo_ref, o_ref):
    scale = XA_HEAD_DIM ** -0.5
    x = x_ref[0]
    h = _rms(x, g_ref[...]).astype(BF16)
    q = jnp.dot(h, wq_ref[...], preferred_element_type=F32)
    heads = []
    for hd in range(XA_HEADS):
        sl = slice(hd * XA_HEAD_DIM, (hd + 1) * XA_HEAD_DIM)
        s = lax.dot_general(q[:, sl].astype(BF16), k_ref[0, :, sl], (((1,), (1,)), ((), ())),
                            preferred_element_type=F32) * scale
        p = jnp.exp(s - jnp.max(s, axis=-1, keepdims=True))
        a = (p / jnp.sum(p, axis=-1, keepdims=True)).astype(BF16)
        heads.append(jnp.dot(a, v_ref[0, :, sl], preferred_element_type=F32).astype(BF16))
    o = jnp.concatenate(heads, axis=-1)
    o_ref[0] = x + jnp.dot(o, wo_ref[...], preferred_element_type=F32)


def _attn(x, kv, kv_row0, g, wq, wo, layer, tq=512):
    b, l, d = x.shape
    return pl.pallas_call(
        _attn_body,
        grid=(b, l // tq),
        in_specs=[pl.BlockSpec((1, tq, d), lambda i, j: (i, j, 0)),
                  pl.BlockSpec((None, 1, N_MEM, d), lambda i, j: (layer, kv_row0 + i, 0, 0)),
                  pl.BlockSpec((None, 1, N_MEM, d), lambda i, j: (layer, kv_row0 + i, 0, 1)),
                  pl.BlockSpec((1, d), lambda i, j: (0, 0)),
                  pl.BlockSpec((None, d, d), lambda i, j: (layer, 0, 0)),
                  pl.BlockSpec((None, d, d), lambda i, j: (layer, 0, 0))],
        out_specs=pl.BlockSpec((1, tq, d), lambda i, j: (i, j, 0)),
        out_shape=jax.ShapeDtypeStruct((b, l, d), F32),
        compiler_params=_cparams("parallel", "arbitrary"),
        name="attn",
    )(x, kv, kv, g.reshape(1, d), wq, wo)


def _outproj_body(x_ref, yh_ref, yp_ref, gh_ref, wt_ref, wb_ref, o_ref):
    yh = _rms(yh_ref[...], gh_ref[...]).astype(BF16)
    acc = jnp.dot(yh, wt_ref[...], preferred_element_type=F32)
    acc += jnp.dot(yp_ref[...], wb_ref[...], preferred_element_type=F32)
    o_ref[...] = x_ref[...] + acc


def _outproj(x, yh, yp, g_hy, w_out, layer, tm=512):
    t, d = x.shape
    return pl.pallas_call(
        _outproj_body,
        grid=(t // tm,),
        in_specs=[pl.BlockSpec((tm, d), lambda i: (i, 0)),
                  pl.BlockSpec((tm, HY_CH), lambda i: (i, 0)),
                  pl.BlockSpec((tm, POOL_CH), lambda i: (i, 0)),
                  pl.BlockSpec((1, HY_CH), lambda i: (0, 0)),
                  pl.BlockSpec((None, HY_CH, d), lambda i: (layer, 0, 0)),
                  pl.BlockSpec((None, POOL_CH, d), lambda i: (layer, 1, 0))],
        out_specs=pl.BlockSpec((tm, d), lambda i: (i, 0)),
        out_shape=jax.ShapeDtypeStruct((t, d), F32),
        compiler_params=_cparams("parallel"),
        name="outproj",
    )(x, yh, yp, g_hy.reshape(1, HY_CH), w_out, w_out)


def _shift_rows(x, down):
    r = x.shape[-2]
    rows = lax.broadcasted_iota(jnp.int32, x.shape, x.ndim - 2)
    if down:
        return jnp.where(rows == 0, 0.0, pltpu.roll(x, 1, axis=x.ndim - 2))
    return jnp.where(rows == r - 1, 0.0, pltpu.roll(x, r - 1, axis=x.ndim - 2))


def _wrap_prev(x, is_first):
    return jnp.where(is_first, _shift_rows(x, True), x)


def _wrap_next(x, is_last):
    return jnp.where(is_last, _shift_rows(x, False), x)


def _pack(re, im):
    half = jnp.uint32(0x8000)
    rb = (lax.bitcast_convert_type(re, U32) + half) & jnp.uint32(0xFFFF0000)
    return rb | ((lax.bitcast_convert_type(im, U32) + half) >> 16)


def _unpack(p):
    return (lax.bitcast_convert_type(p & jnp.uint32(0xFFFF0000), F32),
            lax.bitcast_convert_type(p << 16, F32))


def _split_ri(x):
    m, c = x.shape
    x4 = x.reshape(m // (2 * SUBLANES), 2, SUBLANES, c)
    return x4[:, 0].reshape(m // 2, c), x4[:, 1].reshape(m // 2, c)


def _join_ri(re, im):
    n, c = re.shape
    g = n // SUBLANES
    return jnp.stack([re.reshape(g, SUBLANES, c), im.reshape(g, SUBLANES, c)], axis=1).reshape(2 * n, c)


def _store_cols(scr, n2i, n1, pitch, val):
    for l in range(scr.shape[0]):
        scr[l, pl.ds(n2i, n1, stride=pitch), :] = val[:, l * LANES:(l + 1) * LANES]


def _load_cols(scr, n2i, n1, pitch):
    return jnp.concatenate([scr[l, pl.ds(n2i, n1, stride=pitch), :] for l in range(scr.shape[0])], axis=1)


def _store_rows(scr, base, n2, val):
    for l in range(scr.shape[0]):
        scr[l, pl.ds(base, n2), :] = val[:, l * LANES:(l + 1) * LANES]


def _load_rows(scr, base, n2):
    return jnp.concatenate([scr[l, pl.ds(base, n2), :] for l in range(scr.shape[0])], axis=1)


def _stage1_to_scratch(x_bf16, w_ref, tw_ref, jj, scr, n2i, n1, pitch):
    res = jnp.dot(w_ref[...], x_bf16, preferred_element_type=F32)
    ar, ai = _split_ri(res)
    twr, twi = tw_ref[0, 0, :, jj:jj + 1], tw_ref[0, 1, :, jj:jj + 1]
    _store_cols(scr, n2i, n1, pitch, _pack(ar * twr - ai * twi, ar * twi + ai * twr))


def _stage2_fwd(scr, base, n2, f_ref):
    xr, xi = _unpack(_load_rows(scr, base, n2))
    s = jnp.dot(f_ref[...], _join_ri(xr, xi).astype(BF16), preferred_element_type=F32)
    return _split_ri(s)


def _lconv_body(vm_ref, gm_ref, kf_ref, w_ref, wi_ref, f_ref, g_ref, tw_ref, o_ref, scr,
                *, n2, n1, pitch, tj, tk, s1, s2):
    s = pl.program_id(2)

    @pl.when(s < s1)
    def _():
        for jj in range(tj):
            v = vm_ref[:, jj]
            nb, r, ct = v.shape
            _stage1_to_scratch(v.reshape(nb * r, ct).astype(BF16), w_ref, tw_ref, jj, scr,
                               s * tj + jj, n1, pitch)

    @pl.when(jnp.logical_and(s >= s1, s < s1 + s2))
    def _():
        for kk in range(tk):
            base = pl.multiple_of(((s - s1) * tk + kk) * pitch, SUBLANES)
            sr, si = _stage2_fwd(scr, base, n2, f_ref)
            kr, ki = _unpack(kf_ref[0, kk])
            y = _join_ri(sr * kr - si * ki, sr * ki + si * kr).astype(BF16)
            a = jnp.dot(g_ref[...], y, preferred_element_type=F32)
            _store_rows(scr, base, n2, _pack(*_split_ri(a)))

    @pl.when(s >= s1 + s2)
    def _():
        jp = s - s1 - s2
        for jj in range(tj):
            gate = gm_ref[:, jj]
            ar, ai = _unpack(_load_cols(scr, jp * tj + jj, n1, pitch))
            twr, twi = tw_ref[0, 0, :, jj:jj + 1], tw_ref[0, 1, :, jj:jj + 1]
            x = _join_ri(ar * twr + ai * twi, ai * twr - ar * twi).astype(BF16)
            y = jnp.dot(wi_ref[...], x, preferred_element_type=F32)
            o_ref[:, jj] = (gate * y.reshape(gate.shape)).astype(o_ref.dtype)


def _lconv(vsrc, v_goff, u, g_goff, kf, order, tabs, cfg, out_dtype, name):
    n2, r, n1, tj, tk, ct, pitch = (cfg[k] for k in ("N2", "R", "N1", "tj", "tk", "ct", "pitch"))
    p = vsrc.shape[0] // 2
    s1, s2 = n2 // tj, n1 // tk

    def jv(s):
        return jnp.minimum(s, s1 - 1)

    def jg(s):
        return jnp.maximum(s - s1 - s2, 0)

    def jt(s):
        return jnp.where(s < s1, s, jg(s))

    def plane_spec(goff, jf):
        return pl.BlockSpec((2, tj, r, ct), lambda p_, c, s: (p_, jf(s), 0, goff + c))

    in_specs = [plane_spec(v_goff, jv), plane_spec(g_goff, jg)]
    args = [vsrc, u]
    in_specs += [pl.BlockSpec((1, tk, n2, ct), lambda p_, c, s: (order, jnp.clip(s - s1, 0, s2 - 1), 0, c)),
                 _const_spec((2 * n1, 2 * r)), _const_spec((2 * r, 2 * n1)),
                 _const_spec((2 * n2, 2 * n2)), _const_spec((2 * n2, 2 * n2)),
                 pl.BlockSpec((1, 2, n1, tj), lambda p_, c, s: (jt(s), 0, 0, 0))]
    args += [kf, tabs["w1"], tabs["w1inv"], tabs["f2"], tabs["g2"], tabs["tw"]]
    return pl.pallas_call(
        functools.partial(_lconv_body, n2=n2, n1=n1, pitch=pitch, tj=tj, tk=tk, s1=s1, s2=s2),
        grid=(p, HY_CH // ct, 2 * s1 + s2),
        in_specs=in_specs,
        out_specs=pl.BlockSpec((2, tj, r, ct), lambda p_, c, s: (p_, jg(s), 0, c)),
        out_shape=jax.ShapeDtypeStruct((2 * p, n2, r, HY_CH), out_dtype),
        scratch_shapes=[pltpu.VMEM((ct // LANES, n1 * pitch, LANES), U32)],
        compiler_params=_cparams("parallel", "arbitrary", "arbitrary"),
        name=name,
    )(*args)


def _filt_pos_half(shape, n2, n2_total, d):
    row = lax.broadcasted_iota(jnp.int32, shape, 0)
    if d == 0:
        return (row * n2_total + n2).astype(F32)
    pos_b = (shape[0] - row) * n2_total - n2
    return jnp.where(jnp.logical_and(row == 0, n2 == 0), 0, pos_b).astype(F32)


def _filt_hidden_body(bands_ref, w1t_ref, w1c_ref, w1s_ref, b1_ref, w2_ref, b2_ref, fr_ref, h_ref,
                      *, seq_len, n2_total):
    tjh, n1 = h_ref.shape[0], h_ref.shape[1]
    lanes = tjh * n1
    idx = lax.broadcasted_iota(jnp.int32, (1, lanes), 1)
    row = idx & (n1 - 1)
    n2 = pl.program_id(0) * tjh + (idx >> int(math.log2(n1)))
    pos_b = jnp.where(jnp.logical_and(row == n1 // 2, n2 == 0), 0, (n1 - row) * n2_total - n2)
    pos = jnp.where(row < n1 // 2, row * n2_total + n2, pos_b).astype(F32)
    t = pos / float(max(seq_len - 1, 1))
    ang = bands_ref[...] * ((2.0 * math.pi / seq_len) * pos)
    pre = (w1t_ref[...] * t + jnp.dot(w1c_ref[...], jnp.cos(ang), precision=HI, preferred_element_type=F32)
           + jnp.dot(w1s_ref[...], -jnp.sin(ang), precision=HI, preferred_element_type=F32) + b1_ref[...])
    h = jnp.sin(fr_ref[...] * pre)
    h = jnp.sin(fr_ref[...] * (jnp.dot(w2_ref[...], h, precision=HI, preferred_element_type=F32) + b2_ref[...]))
    h = h.T
    hi = h.astype(BF16)
    lo = (h - hi.astype(F32)).astype(BF16)
    h_ref[...] = jnp.concatenate([hi, hi, lo], axis=1).reshape(h_ref.shape)


def _filt_hidden(w1, b1, w2, b2, freq, cfg, seq_len):
    n2, n1 = cfg["N2"], cfg["N1"]
    tjh = FILT_LANES // n1
    bands = np.linspace(1e-4, POS_BANDS - 1, POS_BANDS, dtype=np.float32).reshape(POS_BANDS, 1)
    col = lambda v: v.reshape(FILT_HIDDEN, 1)
    args = (jnp.asarray(bands), col(w1[0]), w1[1:1 + POS_BANDS].T, w1[1 + POS_BANDS:].T, col(b1), w2.T, col(b2),
            col(freq))
    return pl.pallas_call(
        functools.partial(_filt_hidden_body, seq_len=seq_len, n2_total=n2),
        grid=(n2 // tjh,),
        in_specs=[_const_spec(a.shape) for a in args],
        out_specs=pl.BlockSpec((tjh, n1, 3 * FILT_HIDDEN), lambda j: (j, 0, 0)),
        out_shape=jax.ShapeDtypeStruct((n2, n1, 3 * FILT_HIDDEN), BF16),
        compiler_params=_cparams("parallel"),
        name="filt_hidden",
    )(*args)


def _filt_spec_body(h_ref, w3_ref, dec_ref, bias_ref, wf_ref, f_ref, tw_ref, o_ref, scr, nrm_ref,
                    *, seq_len, n2, n1, pitch, tj, tk, s1):
    s = pl.program_id(2)
    r = n1 // 2
    ct = o_ref.shape[-1]

    @pl.when(s == 0)
    def _():
        nrm_ref[...] = jnp.zeros_like(nrm_ref)

    @pl.when(s < s1)
    def _():
        row = lax.broadcasted_iota(jnp.int32, (r, ct), 0)
        for jj in range(tj):
            n2i = s * tj + jj
            halves = []
            for d in range(2):
                val = jnp.dot(h_ref[jj, d * r:(d + 1) * r, :], w3_ref[0, d], preferred_element_type=F32)
                t = _filt_pos_half((r, ct), n2i, n2, d) / float(max(seq_len - 1, 1))
                val = val * jnp.exp(-t * jnp.abs(dec_ref[0, d:d + 1, :]))
                nrm_ref[...] += jnp.sum(jnp.abs(val), axis=0, keepdims=True)
                if d == 1:
                    val = jnp.where(jnp.logical_and(row == 0, n2i == 0), 0.0, val)
                halves.append(val)
            _stage1_to_scratch(jnp.concatenate(halves, axis=0).astype(BF16), wf_ref, tw_ref, jj, scr,
                               n2i, n1, pitch)

    @pl.when(s >= s1)
    def _():
        inv = 1.0 / nrm_ref[...]
        bias = bias_ref[0]
        for kk in range(tk):
            base = pl.multiple_of(((s - s1) * tk + kk) * pitch, SUBLANES)
            sr, si = _stage2_fwd(scr, base, n2, f_ref)
            o_ref[0, kk] = _pack(sr * inv + bias, si * inv)


def _filt_spec(hid, w3, dec, bias, tabs, cfg, seq_len):
    n2, n1, tj, tk, ct, pitch = (cfg[k] for k in ("N2", "N1", "tj", "tk", "ct", "pitch"))
    s1, s2 = n2 // tj, n1 // tk
    return pl.pallas_call(
        functools.partial(_filt_spec_body, seq_len=seq_len, n2=n2, n1=n1, pitch=pitch, tj=tj, tk=tk, s1=s1),
        grid=(2, HY_CH // ct, s1 + s2),
        in_specs=[pl.BlockSpec((tj, n1, 3 * FILT_HIDDEN), lambda o, c, s: (jnp.minimum(s, s1 - 1), 0, 0)),
                  pl.BlockSpec((1, 2, 3 * FILT_HIDDEN, ct), lambda o, c, s: (o, 0, 0, c)),
                  pl.BlockSpec((1, 2, ct), lambda o, c, s: (o, 0, c)),
                  pl.BlockSpec((1, 1, ct), lambda o, c, s: (o, 0, c)),
                  _const_spec((2 * n1, n1)), _const_spec((2 * n2, 2 * n2)),
                  pl.BlockSpec((1, 2, n1, tj), lambda o, c, s: (jnp.minimum(s, s1 - 1), 0, 0, 0))],
        out_specs=pl.BlockSpec((1, tk, n2, ct), lambda o, c, s: (o, jnp.maximum(s - s1, 0), 0, c)),
        out_shape=jax.ShapeDtypeStruct((2, n1, n2, HY_CH), U32),
        scratch_shapes=[pltpu.VMEM((ct // LANES, n1 * pitch, LANES), U32), pltpu.VMEM((1, ct), F32)],
        compiler_params=_cparams("parallel", "arbitrary", "arbitrary"),
        name="filt_spec",
    )(hid, w3, dec, bias, tabs["wf"], tabs["f2"], tabs["tw"])


def _pool_body(*refs, seq_len, n2_total):
    ng = len(POOL_WINDOWS)
    m_ref, p_refs, n_refs = refs[0], refs[1:1 + ng], refs[1 + ng:2 * ng]
    pw_ref, ps_ref, gp_ref, o_ref = refs[2 * ng:]
    tj, r = m_ref.shape[1], m_ref.shape[2]
    j = pl.program_id(1)
    first, last = j == 0, j == pl.num_programs(1) - 1
    j0 = j * tj
    plane = lax.broadcasted_iota(jnp.int32, (tj, r, LANES), 0)
    row = lax.broadcasted_iota(jnp.int32, (tj, r, LANES), 1)
    t = row * n2_total + j0 + plane
    outs = []
    for g, win in enumerate(POOL_WINDOWS):
        sl = slice(g * POOL_GROUP, (g + 1) * POOL_GROUP)
        half = win // 2
        main = m_ref[0, :, :, sl]
        parts = [_wrap_prev(p_refs[g][0], first), main]
        if half > 1:
            parts.append(_wrap_next(n_refs[g - 1][0, 0:half - 1], last))
        acc = jnp.concatenate(parts, axis=0)
        width = 1
        while width < win:
            n = acc.shape[0] - width
            acc = acc[:n] + acc[width:width + n]
            width *= 2
        cnt = (jnp.minimum(t + half - 1, seq_len - 1) - jnp.maximum(t - half, 0) + 1).astype(F32)
        dgrp = acc / jnp.concatenate([cnt] * (POOL_GROUP // LANES), axis=-1) - main
        outs.append(jnp.dot(dgrp.reshape(tj * r, POOL_GROUP).astype(BF16), pw_ref[g],
                            preferred_element_type=F32))
    y = jnp.concatenate(outs, axis=-1) * ps_ref[...]
    o_ref[0] = _rms(y, gp_ref[...]).astype(BF16).reshape(tj, r, POOL_CH)


def _pool(u, pool_w, pool_scale, g_pool, cfg, seq_len):
    n2, r, tj = cfg["N2"], cfg["R"], cfg["tj_pool"]
    b = u.shape[0]
    goff = 3 * HY_CH // POOL_CH

    def halo_spec(g, nxt):
        h = POOL_WINDOWS[g] // 2
        nh = n2 // h
        cb = goff * (POOL_CH // POOL_GROUP) + g
        if nxt:
            return pl.BlockSpec((1, h, r, POOL_GROUP), lambda i, j: (i, ((j + 1) * (tj // h)) % nh, 0, cb))
        return pl.BlockSpec((1, h, r, POOL_GROUP), lambda i, j: (i, (j * (tj // h) + nh - 1) % nh, 0, cb))

    ng = len(POOL_WINDOWS)
    return pl.pallas_call(
        functools.partial(_pool_body, seq_len=seq_len, n2_total=n2),
        grid=(b, n2 // tj),
        in_specs=[pl.BlockSpec((1, tj, r, POOL_CH), lambda i, j: (i, j, 0, goff))]
        + [halo_spec(g, False) for g in range(ng)] + [halo_spec(g, True) for g in range(1, ng)]
        + [_const_spec((ng, POOL_GROUP, POOL_GROUP)), _const_spec((1, POOL_CH)), _const_spec((1, POOL_CH))],
        out_specs=pl.BlockSpec((1, tj, r, POOL_CH), lambda i, j: (i, j, 0, 0)),
        out_shape=jax.ShapeDtypeStruct((b, n2, r, POOL_CH), BF16),
        compiler_params=_cparams("parallel", "arbitrary"),
        name="pool",
    )(*([u] * (2 * ng)), pool_w, pool_scale, g_pool)


def _fft_cfg(seq_len):
    n2 = 128 if seq_len >= 8192 else 64
    r = seq_len // n2
    assert r % SUBLANES == 0 and r * n2 == seq_len
    tj = 8 if seq_len >= 8192 else 32
    return dict(N2=n2, R=r, N1=2 * r, tj=tj, tk=max(tj, 16), ct=256, pitch=n2 + SUBLANES, tj_pool=16)


def _dft_tables(cfg):
    n2, r, n1, tj = cfg["N2"], cfg["R"], cfg["N1"], cfg["tj"]
    n = n1 * n2

    def cis(idx, period):
        a = -2.0 * np.pi * (idx % period) / period
        return np.cos(a), np.sin(a)

    def ri_order(m):
        pos = np.arange(2 * m)
        return (pos % (2 * SUBLANES)) // SUBLANES * m + pos // (2 * SUBLANES) * SUBLANES + pos % SUBLANES

    k1 = np.arange(n1)
    mr, mi = cis(k1[:, None] * k1[None, :], n1)
    mrd, mid = mr[:, :r], mi[:, :r]
    o1, o2 = ri_order(n1), ri_order(n2)
    w1 = np.block([[mrd, -mid], [mid, mrd]])[o1]
    wf = np.concatenate([mr, mi], axis=0)[o1]
    k2 = np.arange(n2)
    fr, fi = cis(k2[:, None] * k2[None, :], n2)
    f2 = np.block([[fr, -fi], [fi, fr]])[o2][:, o2]
    g2 = np.block([[fr, fi], [-fi, fr]])[o2][:, o2]
    twr, twi = cis(k1[:, None] * k2[None, :], n)
    tw = np.stack([twr, twi]).reshape(2, n1, n2 // tj, tj).transpose(2, 0, 1, 3)
    return dict(w1=jnp.asarray(w1, BF16), w1inv=jnp.asarray(w1.T / n, BF16), wf=jnp.asarray(wf, BF16),
                f2=jnp.asarray(f2, BF16), g2=jnp.asarray(g2, BF16), tw=jnp.asarray(tw, F32))


def _to_plane_order(x, cfg):
    b, l, d = x.shape
    xs = jnp.swapaxes(x.reshape(b, cfg["R"], cfg["N2"], d), 1, 2).reshape(b * l, d)
    return lax.optimization_barrier(xs)


def _from_plane_order(x, b, cfg):
    d = x.shape[-1]
    return jnp.swapaxes(x.reshape(b, cfg["N2"], cfg["R"], d), 1, 2).reshape(b, cfg["N2"] * cfg["R"], d)


def _trunk(x, kv, kv_row0, lw, g_final, depth):
    b, seq_len, d = x.shape
    cfg = _fft_cfg(seq_len)
    tabs = _dft_tables(cfg)
    n2, r, ct = cfg["N2"], cfg["R"], cfg["ct"]
    xs = _to_plane_order(x, cfg)
    cpb = HY_CH // ct
    for i in range(depth):
        w = {k: v[i] for k, v in lw.items() if k not in _STACKED_WEIGHTS}
        hid = _filt_hidden(w["filt_w1"], w["filt_b1"], w["filt_w2"], w["filt_b2"], w["filt_freq"], cfg, seq_len)
        w3 = jnp.transpose(w["filt_w3"].reshape(FILT_HIDDEN, 2, 2, HY_CH), (1, 2, 0, 3))
        w3hi = w3.astype(BF16)
        w3lo = (w3 - w3hi.astype(F32)).astype(BF16)
        kf = _filt_spec(hid, jnp.concatenate([w3hi, w3lo, w3hi], axis=2),
                        w["hy_decay"], w["hy_bias"].reshape(2, 1, HY_CH), tabs, cfg, seq_len)
        u = _in_proj(xs.reshape(b, n2, r, d), w["g_mix"], lw["w_in"], i, w["conv_w"], w["conv_b"].reshape(1, -1), cfg)
        z1 = _lconv(u, 2 * cpb, u, 0, kf, 0, tabs, cfg, BF16, "lconv1")
        yh = _lconv(z1, 0, u, cpb, kf, 1, tabs, cfg, F32, "lconv2")
        yp = _pool(u, w["pool_w"], w["pool_scale"].reshape(1, -1), w["g_pool"].reshape(1, -1), cfg, seq_len)
        xs = _outproj(xs, yh.reshape(b * seq_len, HY_CH), yp.reshape(b * seq_len, POOL_CH), w["g_hy"],
                      lw["w_out"], i)
        xs = _attn(xs.reshape(b, seq_len, d), kv, kv_row0, w["g_xa"], lw["w_q"], lw["w_o"], i).reshape(b * seq_len, d)
        xs = _mlp(xs, w["g_mlp"], lw["w_up"], lw["w_down"], i, g_final, i == depth - 1)
    return _from_plane_order(xs, b, cfg)


_STACKED_WEIGHTS = ("w_in", "w_out", "w_q", "w_kv", "w_o", "w_up", "w_down")
_MATMUL_WEIGHTS = _STACKED_WEIGHTS + ("pool_w",)


def kernel(x_prompt, x_sample, mem_prompt, mem_sample, g_mix, w_in, conv_w, conv_b, filt_w1, filt_b1, filt_w2, filt_b2, filt_freq, filt_w3, hy_decay, hy_bias, pool_w, pool_scale, g_hy, g_pool, w_out, g_xa, g_mem, w_q, w_kv, w_o, g_mlp, w_up, w_down, g_final):
    lw = dict(g_mix=g_mix, w_in=w_in, conv_w=conv_w, conv_b=conv_b, filt_w1=filt_w1, filt_b1=filt_b1,
              filt_w2=filt_w2, filt_b2=filt_b2, filt_freq=filt_freq, filt_w3=filt_w3, hy_decay=hy_decay,
              hy_bias=hy_bias, pool_w=pool_w, pool_scale=pool_scale, g_hy=g_hy, g_pool=g_pool, w_out=w_out,
              g_xa=g_xa, g_mem=g_mem, w_q=w_q, w_kv=w_kv, w_o=w_o, g_mlp=g_mlp, w_up=w_up, w_down=w_down)
    lw = {k: (v.astype(BF16) if k in _MATMUL_WEIGHTS else v) for k, v in lw.items()}
    depth = g_mix.shape[0]
    mem = jnp.concatenate([mem_prompt, mem_sample], axis=0)
    kv = _kv_proj(mem.reshape(-1, D_MODEL), lw["g_mem"], lw["w_kv"]).reshape(depth, -1, N_MEM, 2 * D_MODEL)
    y_prompt = _trunk(x_prompt, kv, 0, lw, g_final, depth)
    y_sample = _trunk(x_sample, kv, mem_prompt.shape[0], lw, g_final, depth)
    return (y_prompt, y_sample)
```

```python
import functools
import math

import numpy as np
import jax
import jax.numpy as jnp
from jax import lax
from jax.experimental import pallas as pl
from jax.experimental.pallas import tpu as pltpu

F32 = jnp.float32
BF16 = jnp.bfloat16
U32 = jnp.uint32

D_MODEL = 2048
HY_CH = 1024
POOL_CH = 1024
D_IN = 3 * HY_CH + POOL_CH
POOL_WINDOWS = (2, 4, 8, 16)
POOL_GROUP = POOL_CH // len(POOL_WINDOWS)
POS_BANDS = 16
FILT_HIDDEN = 64
N_MEM = 256
XA_HEADS = 4
XA_HEAD_DIM = D_MODEL // XA_HEADS
D_FF = 4 * D_MODEL
EPS = 1e-6

LANES = 128
SUBLANES = 8
VMEM_LIMIT_BYTES = 58 * 1024 * 1024
NORM_CHUNK_ROWS = 256
FILT_LANES = 1024
IN_PROJ_TOKENS = 1024
HY_CONV_BLOCKS = 3
HI = lax.Precision.HIGHEST


def _cparams(*sem):
    return pltpu.CompilerParams(dimension_semantics=sem, vmem_limit_bytes=VMEM_LIMIT_BYTES)


def _rms(x, g):
    return x * lax.rsqrt(jnp.mean(x * x, axis=-1, keepdims=True) + EPS) * g


def _const_spec(shape):
    nd = len(shape)
    return pl.BlockSpec(shape, lambda *_: (0,) * nd)


def _kv_proj_body(x_ref, g_ref, w_ref, o_ref, h_ref):
    @pl.when(pl.program_id(2) == 0)
    def _():
        h_ref[...] = _rms(x_ref[...], g_ref[...]).astype(BF16)

    o_ref[...] = jnp.dot(h_ref[...], w_ref[...], preferred_element_type=F32).astype(o_ref.dtype)


def _kv_proj(mem, g_mem, w_kv, tm=512, tn=1024):
    t, d = mem.shape
    depth, _, n = w_kv.shape
    return pl.pallas_call(
        _kv_proj_body,
        grid=(depth, t // tm, n // tn),
        in_specs=[pl.BlockSpec((tm, d), lambda l, i, j: (i, 0)),
                  pl.BlockSpec((None, 1, d), lambda l, i, j: (l, 0, 0)),
                  pl.BlockSpec((None, d, tn), lambda l, i, j: (l, 0, j))],
        out_specs=pl.BlockSpec((None, tm, tn), lambda l, i, j: (l, i, j)),
        out_shape=jax.ShapeDtypeStruct((depth, t, n), BF16),
        scratch_shapes=[pltpu.VMEM((tm, d), BF16)],
        compiler_params=_cparams("parallel", "parallel", "arbitrary"),
        name="kv_proj",
    )(mem, g_mem.reshape(depth, 1, d), w_kv)


def _in_proj_body(x_ref, xp_ref, xn_ref, g_ref, w_ref, cw_ref, cb_ref, o_ref, h_ref, hh_ref):
    gi, j = pl.program_id(1), pl.program_id(2)
    n2, rt, d = x_ref.shape[1:]
    tn = o_ref.shape[3]

    def emit(blk):
        first, conv = blk == 0, blk < HY_CONV_BLOCKS
        w_blk = w_ref.at[:, blk * tn:(blk + 1) * tn]
        rows = NORM_CHUNK_ROWS
        ppc = rows // rt
        if conv:
            w0, w1, w2, b = cw_ref[0:1, :], cw_ref[1:2, :], cw_ref[2:3, :], cb_ref[...]
            tap = lambda lo_, mid_, hi_: lo_ * w0 + mid_ * w1 + hi_ * w2 + b
        if first:
            hh_ref[...] = _rms(jnp.concatenate([xp_ref[0, 0], xn_ref[0, 0]], axis=0), g_ref[...]).astype(BF16)
        for c in range(n2 // ppc):
            if first:
                h = _rms(x_ref[0, c * ppc:(c + 1) * ppc].reshape(rows, d), g_ref[...]).astype(BF16)
                h_ref[c * rows:(c + 1) * rows, :] = h
            else:
                h = h_ref[c * rows:(c + 1) * rows, :]
            uc = jnp.dot(h, w_blk[...], preferred_element_type=F32).reshape(ppc, rt, tn)
            p0 = c * ppc
            if not conv:
                o_ref[0, p0:p0 + ppc] = uc
                continue
            if c == 0:
                u0, u1 = uc[0], uc[1]
            else:
                o_ref[0, p0 - 1] = tap(t2, t1, uc[0])
                o_ref[0, p0] = tap(t1, uc[0], uc[1])
            o_ref[0, p0 + 1:p0 + ppc - 1] = tap(uc[0:ppc - 2], uc[1:ppc - 1], uc[2:ppc])
            t2, t1 = uc[ppc - 2], uc[ppc - 1]
        if conv:
            uh = jnp.dot(hh_ref[...], w_blk[...], preferred_element_type=F32)
            row = lax.broadcasted_iota(jnp.int32, (rt, tn), 0)
            prev_tok = jnp.where(gi == 0, 0.0, uh[SUBLANES - 1:SUBLANES])
            next_tok = jnp.where(gi == pl.num_programs(1) - 1, 0.0, uh[SUBLANES:SUBLANES + 1])
            wrap_prev = jnp.where(row == 0, prev_tok, pltpu.roll(t1, 1, axis=0))
            wrap_next = jnp.where(row == rt - 1, next_tok, pltpu.roll(u0, rt - 1, axis=0))
            o_ref[0, 0] = tap(wrap_prev, u0, u1)
            o_ref[0, n2 - 1] = tap(t2, t1, wrap_next)

    for blk in range(w_ref.shape[1] // tn):
        pl.when(j == blk)(functools.partial(emit, blk))


def _in_proj(x4, g, w, layer, conv_w, conv_b, cfg):
    b, n2, r, d = x4.shape
    rt = IN_PROJ_TOKENS // n2
    tn = HY_CH
    rg = r // SUBLANES
    return pl.pallas_call(
        _in_proj_body,
        grid=(b, r // rt, D_IN // tn),
        in_specs=[pl.BlockSpec((1, n2, rt, d), lambda i, g_, j: (i, 0, g_, 0)),
                  pl.BlockSpec((1, 1, SUBLANES, d),
                               lambda i, g_, j: (i, n2 - 1, jnp.maximum(g_ * (rt // SUBLANES) - 1, 0), 0)),
                  pl.BlockSpec((1, 1, SUBLANES, d),
                               lambda i, g_, j: (i, 0, jnp.minimum((g_ + 1) * (rt // SUBLANES), rg - 1), 0)),
                  pl.BlockSpec((1, d), lambda i, g_, j: (0, 0)),
                  pl.BlockSpec((None, d, D_IN), lambda i, g_, j: (layer, 0, 0),
                               pipeline_mode=pl.Buffered(1)),
                  pl.BlockSpec((3, tn), lambda i, g_, j: (0, jnp.minimum(j, HY_CONV_BLOCKS - 1))),
                  pl.BlockSpec((1, tn), lambda i, g_, j: (0, jnp.minimum(j, HY_CONV_BLOCKS - 1)))],
        out_specs=pl.BlockSpec((1, n2, rt, tn), lambda i, g_, j: (i, 0, g_, j)),
        out_shape=jax.ShapeDtypeStruct((b, n2, r, D_IN), F32),
        scratch_shapes=[pltpu.VMEM((n2 * rt, d), BF16), pltpu.VMEM((2 * SUBLANES, d), BF16)],
        compiler_params=_cparams("parallel", "parallel", "arbitrary"),
        name="in_proj",
    )(x4, x4, x4, g.reshape(1, d), w, conv_w, conv_b)


def _mlp_body(x_ref, g_ref, wu_ref, wd_ref, gf_ref, o_ref, h_ref, *, final_norm):
    f = pl.program_id(1)
    rows = NORM_CHUNK_ROWS

    def up_down(h):
        hm = jnp.dot(h, wu_ref[...], preferred_element_type=F32)
        a = jnp.square(jnp.maximum(hm, 0.0)).astype(BF16)
        return jnp.dot(a, wd_ref[...], preferred_element_type=F32)

    @pl.when(f == 0)
    def _():
        for c in range(x_ref.shape[0] // rows):
            sl = slice(c * rows, (c + 1) * rows)
            x = x_ref[sl, :]
            h = _rms(x, g_ref[...]).astype(BF16)
            h_ref[sl, :] = h
            o_ref[sl, :] = x + up_down(h)

    @pl.when(f != 0)
    def _():
        o_ref[...] += up_down(h_ref[...])

    if final_norm:
        @pl.when(f == pl.num_programs(1) - 1)
        def _():
            o_ref[...] = _rms(o_ref[...], gf_ref[...])


def _mlp(x, g, wu, wd, layer, g_final, final_norm, tm=512, tf=1024):
    t, d = x.shape
    ff = wu.shape[2]
    return pl.pallas_call(
        functools.partial(_mlp_body, final_norm=final_norm),
        grid=(t // tm, ff // tf),
        in_specs=[pl.BlockSpec((tm, d), lambda i, f: (i, 0)),
                  pl.BlockSpec((1, d), lambda i, f: (0, 0)),
                  pl.BlockSpec((None, d, tf), lambda i, f: (layer, 0, f)),
                  pl.BlockSpec((None, tf, d), lambda i, f: (layer, f, 0)),
                  pl.BlockSpec((1, d), lambda i, f: (0, 0))],
        out_specs=pl.BlockSpec((tm, d), lambda i, f: (i, 0)),
        out_shape=jax.ShapeDtypeStruct((t, d), F32),
        scratch_shapes=[pltpu.VMEM((tm, d), BF16)],
        compiler_params=_cparams("parallel", "arbitrary"),
        name="mlp",
    )(x, g.reshape(1, d), wu, wd, g_final.reshape(1, d))


def _attn_body(x_ref, k_ref, v_ref, g_ref, wq_ref, wo_ref, o_ref):
    scale = XA_HEAD_DIM ** -0.5
    x = x_ref[0]
    h = _rms(x, g_ref[...]).astype(BF16)
    q = jnp.dot(h, wq_ref[...], preferred_element_type=F32)
    heads = []
    for hd in range(XA_HEADS):
        sl = slice(hd * XA_HEAD_DIM, (hd + 1) * XA_HEAD_DIM)
        s = lax.dot_general(q[:, sl].astype(BF16), k_ref[0, :, sl], (((1,), (1,)), ((), ())),
                            preferred_element_type=F32) * scale
        p = jnp.exp(s - jnp.max(s, axis=-1, keepdims=True))
        a = (p / jnp.sum(p, axis=-1, keepdims=True)).astype(BF16)
        heads.append(jnp.dot(a, v_ref[0, :, sl], preferred_element_type=F32).astype(BF16))
    o = jnp.concatenate(heads, axis=-1)
    o_ref[0] = x + jnp.dot(o, wo_ref[...], preferred_element_type=F32)


def _attn(x, kv, kv_row0, g, wq, wo, layer, tq=512):
    b, l, d = x.shape
    return pl.pallas_call(
        _attn_body,
        grid=(b, l // tq),
        in_specs=[pl.BlockSpec((1, tq, d), lambda i, j: (i, j, 0)),
                  pl.BlockSpec((None, 1, N_MEM, d), lambda i, j: (layer, kv_row0 + i, 0, 0)),
                  pl.BlockSpec((None, 1, N_MEM, d), lambda i, j: (layer, kv_row0 + i, 0, 1)),
                  pl.BlockSpec((1, d), lambda i, j: (0, 0)),
                  pl.BlockSpec((None, d, d), lambda i, j: (layer, 0, 0)),
                  pl.BlockSpec((None, d, d), lambda i, j: (layer, 0, 0))],
        out_specs=pl.BlockSpec((1, tq, d), lambda i, j: (i, j, 0)),
        out_shape=jax.ShapeDtypeStruct((b, l, d), F32),
        compiler_params=_cparams("parallel", "arbitrary"),
        name="attn",
    )(x, kv, kv, g.reshape(1, d), wq, wo)


def _outproj_body(x_ref, yh_ref, yp_ref, gh_ref, wt_ref, wb_ref, o_ref):
    yh = _rms(yh_ref[...], gh_ref[...]).astype(BF16)
    acc = jnp.dot(yh, wt_ref[...], preferred_element_type=F32)
    acc += jnp.dot(yp_ref[...], wb_ref[...], preferred_element_type=F32)
    o_ref[...] = x_ref[...] + acc


def _outproj(x, yh, yp, g_hy, w_out, layer, tm=512):
    t, d = x.shape
    return pl.pallas_call(
        _outproj_body,
        grid=(t // tm,),
        in_specs=[pl.BlockSpec((tm, d), lambda i: (i, 0)),
                  pl.BlockSpec((tm, HY_CH), lambda i: (i, 0)),
                  pl.BlockSpec((tm, POOL_CH), lambda i: (i, 0)),
                  pl.BlockSpec((1, HY_CH), lambda i: (0, 0)),
                  pl.BlockSpec((None, HY_CH, d), lambda i: (layer, 0, 0)),
                  pl.BlockSpec((None, POOL_CH, d), lambda i: (layer, 1, 0))],
        out_specs=pl.BlockSpec((tm, d), lambda i: (i, 0)),
        out_shape=jax.ShapeDtypeStruct((t, d), F32),
        compiler_params=_cparams("parallel"),
        name="outproj",
    )(x, yh, yp, g_hy.reshape(1, HY_CH), w_out, w_out)


def _shift_rows(x, down):
    r = x.shape[-2]
    rows = lax.broadcasted_iota(jnp.int32, x.shape, x.ndim - 2)
    if down:
        return jnp.where(rows == 0, 0.0, pltpu.roll(x, 1, axis=x.ndim - 2))
    return jnp.where(rows == r - 1, 0.0, pltpu.roll(x, r - 1, axis=x.ndim - 2))


def _wrap_prev(x, is_first):
    return jnp.where(is_first, _shift_rows(x, True), x)


def _wrap_next(x, is_last):
    return jnp.where(is_last, _shift_rows(x, False), x)


def _pack(re, im):
    half = jnp.uint32(0x8000)
    rb = (lax.bitcast_convert_type(re, U32) + half) & jnp.uint32(0xFFFF0000)
    return rb | ((lax.bitcast_convert_type(im, U32) + half) >> 16)


def _unpack(p):
    return (lax.bitcast_convert_type(p & jnp.uint32(0xFFFF0000), F32),
            lax.bitcast_convert_type(p << 16, F32))


def _split_ri(x):
    m, c = x.shape
    x4 = x.reshape(m // (2 * SUBLANES), 2, SUBLANES, c)
    return x4[:, 0].reshape(m // 2, c), x4[:, 1].reshape(m // 2, c)


def _join_ri(re, im):
    n, c = re.shape
    g = n // SUBLANES
    return jnp.stack([re.reshape(g, SUBLANES, c), im.reshape(g, SUBLANES, c)], axis=1).reshape(2 * n, c)


def _store_cols(scr, n2i, n1, pitch, val):
    for l in range(scr.shape[0]):
        scr[l, pl.ds(n2i, n1, stride=pitch), :] = val[:, l * LANES:(l + 1) * LANES]


def _load_cols(scr, n2i, n1, pitch):
    return jnp.concatenate([scr[l, pl.ds(n2i, n1, stride=pitch), :] for l in range(scr.shape[0])], axis=1)


def _store_rows(scr, base, n2, val):
    for l in range(scr.shape[0]):
        scr[l, pl.ds(base, n2), :] = val[:, l * LANES:(l + 1) * LANES]


def _load_rows(scr, base, n2):
    return jnp.concatenate([scr[l, pl.ds(base, n2), :] for l in range(scr.shape[0])], axis=1)


def _stage1_to_scratch(x_bf16, w_ref, tw_ref, jj, scr, n2i, n1, pitch):
    res = jnp.dot(w_ref[...], x_bf16, preferred_element_type=F32)
    ar, ai = _split_ri(res)
    twr, twi = tw_ref[0, 0, :, jj:jj + 1], tw_ref[0, 1, :, jj:jj + 1]
    _store_cols(scr, n2i, n1, pitch, _pack(ar * twr - ai * twi, ar * twi + ai * twr))


def _stage2_fwd(scr, base, n2, f_ref):
    xr, xi = _unpack(_load_rows(scr, base, n2))
    s = jnp.dot(f_ref[...], _join_ri(xr, xi).astype(BF16), preferred_element_type=F32)
    return _split_ri(s)


def _lconv_body(vm_ref, gm_ref, kf_ref, w_ref, wi_ref, f_ref, g_ref, tw_ref, o_ref, scr,
                *, n2, n1, pitch, tj, tk, s1, s2):
    s = pl.program_id(2)

    @pl.when(s < s1)
    def _():
        for jj in range(tj):
            v = vm_ref[:, jj]
            nb, r, ct = v.shape
            _stage1_to_scratch(v.reshape(nb * r, ct).astype(BF16), w_ref, tw_ref, jj, scr,
                               s * tj + jj, n1, pitch)

    @pl.when(jnp.logical_and(s >= s1, s < s1 + s2))
    def _():
        for kk in range(tk):
            base = pl.multiple_of(((s - s1) * tk + kk) * pitch, SUBLANES)
            sr, si = _stage2_fwd(scr, base, n2, f_ref)
            kr, ki = _unpack(kf_ref[0, kk])
            y = _join_ri(sr * kr - si * ki, sr * ki + si * kr).astype(BF16)
            a = jnp.dot(g_ref[...], y, preferred_element_type=F32)
            _store_rows(scr, base, n2, _pack(*_split_ri(a)))

    @pl.when(s >= s1 + s2)
    def _():
        jp = s - s1 - s2
        for jj in range(tj):
            gate = gm_ref[:, jj]
            ar, ai = _unpack(_load_cols(scr, jp * tj + jj, n1, pitch))
            twr, twi = tw_ref[0, 0, :, jj:jj + 1], tw_ref[0, 1, :, jj:jj + 1]
            x = _join_ri(ar * twr + ai * twi, ai * twr - ar * twi).astype(BF16)
            y = jnp.dot(wi_ref[...], x, preferred_element_type=F32)
            o_ref[:, jj] = (gate * y.reshape(gate.shape)).astype(o_ref.dtype)


def _lconv(vsrc, v_goff, u, g_goff, kf, order, tabs, cfg, out_dtype, name):
    n2, r, n1, tj, tk, ct, pitch = (cfg[k] for k in ("N2", "R", "N1", "tj", "tk", "ct", "pitch"))
    p = vsrc.shape[0] // 2
    s1, s2 = n2 // tj, n1 // tk

    def jv(s):
        return jnp.minimum(s, s1 - 1)

    def jg(s):
        return jnp.maximum(s - s1 - s2, 0)

    def jt(s):
        return jnp.where(s < s1, s, jg(s))

    def plane_spec(goff, jf):
        return pl.BlockSpec((2, tj, r, ct), lambda p_, c, s: (p_, jf(s), 0, goff + c))

    in_specs = [plane_spec(v_goff, jv), plane_spec(g_goff, jg)]
    args = [vsrc, u]
    in_specs += [pl.BlockSpec((1, tk, n2, ct), lambda p_, c, s: (order, jnp.clip(s - s1, 0, s2 - 1), 0, c)),
                 _const_spec((2 * n1, 2 * r)), _const_spec((2 * r, 2 * n1)),
                 _const_spec((2 * n2, 2 * n2)), _const_spec((2 * n2, 2 * n2)),
                 pl.BlockSpec((1, 2, n1, tj), lambda p_, c, s: (jt(s), 0, 0, 0))]
    args += [kf, tabs["w1"], tabs["w1inv"], tabs["f2"], tabs["g2"], tabs["tw"]]
    return pl.pallas_call(
        functools.partial(_lconv_body, n2=n2, n1=n1, pitch=pitch, tj=tj, tk=tk, s1=s1, s2=s2),
        grid=(p, HY_CH // ct, 2 * s1 + s2),
        in_specs=in_specs,
        out_specs=pl.BlockSpec((2, tj, r, ct), lambda p_, c, s: (p_, jg(s), 0, c)),
        out_shape=jax.ShapeDtypeStruct((2 * p, n2, r, HY_CH), out_dtype),
        scratch_shapes=[pltpu.VMEM((ct // LANES, n1 * pitch, LANES), U32)],
        compiler_params=_cparams("parallel", "arbitrary", "arbitrary"),
        name=name,
    )(*args)


def _filt_hidden_body(bands_ref, w1t_ref, w1c_ref, w1s_ref, b1_ref, w2_ref, b2_ref, fr_ref, h_ref,
                      *, seq_len, n2_total):
    tjh, n1 = h_ref.shape[0], h_ref.shape[1]
    lanes = tjh * n1
    idx = lax.broadcasted_iota(jnp.int32, (1, lanes), 1)
    row = idx & (n1 - 1)
    n2 = pl.program_id(0) * tjh + (idx >> int(math.log2(n1)))
    pos_b = jnp.where(jnp.logical_and(row == n1 // 2, n2 == 0), 0, (n1 - row) * n2_total - n2)
    pos = jnp.where(row < n1 // 2, row * n2_total + n2, pos_b).astype(F32)
    t = pos / float(max(seq_len - 1, 1))
    ang = bands_ref[...] * ((2.0 * math.pi / seq_len) * pos)
    pre = (w1t_ref[...] * t + jnp.dot(w1c_ref[...], jnp.cos(ang), precision=HI, preferred_element_type=F32)
           + jnp.dot(w1s_ref[...], -jnp.sin(ang), precision=HI, preferred_element_type=F32) + b1_ref[...])
    h = jnp.sin(fr_ref[...] * pre)
    h = jnp.sin(fr_ref[...] * (jnp.dot(w2_ref[...], h, precision=HI, preferred_element_type=F32) + b2_ref[...]))
    h = h.T
    hi = h.astype(BF16)
    lo = (h - hi.astype(F32)).astype(BF16)
    h_ref[...] = jnp.concatenate([hi, hi, lo], axis=1).reshape(h_ref.shape)


def _filt_hidden(w1, b1, w2, b2, freq, cfg, seq_len):
    n2, n1 = cfg["N2"], cfg["N1"]
    tjh = FILT_LANES // n1
    bands = np.linspace(1e-4, POS_BANDS - 1, POS_BANDS, dtype=np.float32).reshape(POS_BANDS, 1)
    col = lambda v: v.reshape(FILT_HIDDEN, 1)
    args = (jnp.asarray(bands), col(w1[0]), w1[1:1 + POS_BANDS].T, w1[1 + POS_BANDS:].T, col(b1), w2.T, col(b2),
            col(freq))
    return pl.pallas_call(
        functools.partial(_filt_hidden_body, seq_len=seq_len, n2_total=n2),
        grid=(n2 // tjh,),
        in_specs=[_const_spec(a.shape) for a in args],
        out_specs=pl.BlockSpec((tjh, n1, 3 * FILT_HIDDEN), lambda j: (j, 0, 0)),
        out_shape=jax.ShapeDtypeStruct((n2, n1, 3 * FILT_HIDDEN), BF16),
        compiler_params=_cparams("parallel"),
        name="filt_hidden",
    )(*args)


def _filt_spec_body(h_ref, w3_ref, dec_ref, bias_ref, wf_ref, f_ref, tw_ref, o_ref, scr, nrm_ref, win_ref,
                    *, seq_len, n2, n1, pitch, tj, tk, s1):
    s = pl.program_id(2)
    r = n1 // 2
    ct = o_ref.shape[-1]
    rate = jnp.abs(dec_ref[0]) * (1.0 / float(max(seq_len - 1, 1)))

    @pl.when(s == 0)
    def _():
        nrm_ref[...] = jnp.zeros_like(nrm_ref)
        rowf = lax.broadcasted_iota(jnp.int32, (r, ct), 0).astype(F32) * float(n2)
        win_ref[0] = jnp.exp(-rowf * rate[0:1])
        win_ref[1] = jnp.exp(-(float(r * n2) - rowf) * rate[1:2])

    @pl.when(s < s1)
    def _():
        row = lax.broadcasted_iota(jnp.int32, (r, ct), 0)
        for jj in range(tj):
            n2i = s * tj + jj
            n2f = n2i.astype(F32)
            tap0 = jnp.logical_and(row == 0, n2i == 0)
            halves = []
            for d in range(2):
                raw = jnp.dot(h_ref[jj, d * r:(d + 1) * r, :], w3_ref[0, d], preferred_element_type=F32)
                plane = jnp.exp((n2f if d == 1 else -n2f) * rate[d:d + 1])
                val = raw * (win_ref[d] * plane)
                if d == 1:
                    nrm_ref[...] += jnp.sum(jnp.abs(jnp.where(tap0, raw, val)), axis=0, keepdims=True)
                    val = jnp.where(tap0, 0.0, val)
                else:
                    nrm_ref[...] += jnp.sum(jnp.abs(val), axis=0, keepdims=True)
                halves.append(val)
            _stage1_to_scratch(jnp.concatenate(halves, axis=0).astype(BF16), wf_ref, tw_ref, jj, scr,
                               n2i, n1, pitch)

    @pl.when(s >= s1)
    def _():
        inv = 1.0 / nrm_ref[...]
        bias = bias_ref[0]
        for kk in range(tk):
            base = pl.multiple_of(((s - s1) * tk + kk) * pitch, SUBLANES)
            sr, si = _stage2_fwd(scr, base, n2, f_ref)
            o_ref[0, kk] = _pack(sr * inv + bias, si * inv)


def _filt_spec(hid, w3, dec, bias, tabs, cfg, seq_len):
    n2, n1, tj, tk, ct, pitch = (cfg[k] for k in ("N2", "N1", "tj", "tk", "ct", "pitch"))
    s1, s2 = n2 // tj, n1 // tk
    return pl.pallas_call(
        functools.partial(_filt_spec_body, seq_len=seq_len, n2=n2, n1=n1, pitch=pitch, tj=tj, tk=tk, s1=s1),
        grid=(2, HY_CH // ct, s1 + s2),
        in_specs=[pl.BlockSpec((tj, n1, 3 * FILT_HIDDEN), lambda o, c, s: (jnp.minimum(s, s1 - 1), 0, 0)),
                  pl.BlockSpec((1, 2, 3 * FILT_HIDDEN, ct), lambda o, c, s: (o, 0, 0, c)),
                  pl.BlockSpec((1, 2, ct), lambda o, c, s: (o, 0, c)),
                  pl.BlockSpec((1, 1, ct), lambda o, c, s: (o, 0, c)),
                  _const_spec((2 * n1, n1)), _const_spec((2 * n2, 2 * n2)),
                  pl.BlockSpec((1, 2, n1, tj), lambda o, c, s: (jnp.minimum(s, s1 - 1), 0, 0, 0))],
        out_specs=pl.BlockSpec((1, tk, n2, ct), lambda o, c, s: (o, jnp.maximum(s - s1, 0), 0, c)),
        out_shape=jax.ShapeDtypeStruct((2, n1, n2, HY_CH), U32),
        scratch_shapes=[pltpu.VMEM((ct // LANES, n1 * pitch, LANES), U32), pltpu.VMEM((1, ct), F32),
                        pltpu.VMEM((2, n1 // 2, ct), F32)],
        compiler_params=_cparams("parallel", "arbitrary", "arbitrary"),
        name="filt_spec",
    )(hid, w3, dec, bias, tabs["wf"], tabs["f2"], tabs["tw"])


def _pool_body(*refs, seq_len, n2_total):
    ng = len(POOL_WINDOWS)
    m_ref, p_refs, n_refs = refs[0], refs[1:1 + ng], refs[1 + ng:2 * ng]
    pw_ref, ps_ref, gp_ref, o_ref = refs[2 * ng:]
    tj, r = m_ref.shape[1], m_ref.shape[2]
    j = pl.program_id(1)
    first, last = j == 0, j == pl.num_programs(1) - 1
    j0 = j * tj
    plane = lax.broadcasted_iota(jnp.int32, (tj, r, LANES), 0)
    row = lax.broadcasted_iota(jnp.int32, (tj, r, LANES), 1)
    t = row * n2_total + j0 + plane
    outs = []
    for g, win in enumerate(POOL_WINDOWS):
        sl = slice(g * POOL_GROUP, (g + 1) * POOL_GROUP)
        half = win // 2
        main = m_ref[0, :, :, sl]
        parts = [_wrap_prev(p_refs[g][0], first), main]
        if half > 1:
            parts.append(_wrap_next(n_refs[g - 1][0, 0:half - 1], last))
        acc = jnp.concatenate(parts, axis=0)
        width = 1
        while width < win:
            n = acc.shape[0] - width
            acc = acc[:n] + acc[width:width + n]
            width *= 2
        cnt = (jnp.minimum(t + half - 1, seq_len - 1) - jnp.maximum(t - half, 0) + 1).astype(F32)
        dgrp = acc / jnp.concatenate([cnt] * (POOL_GROUP // LANES), axis=-1) - main
        outs.append(jnp.dot(dgrp.reshape(tj * r, POOL_GROUP).astype(BF16), pw_ref[g],
                            preferred_element_type=F32))
    y = jnp.concatenate(outs, axis=-1) * ps_ref[...]
    o_ref[0] = _rms(y, gp_ref[...]).astype(BF16).reshape(tj, r, POOL_CH)


def _pool(u, pool_w, pool_scale, g_pool, cfg, seq_len):
    n2, r, tj = cfg["N2"], cfg["R"], cfg["tj_pool"]
    b = u.shape[0]
    goff = 3 * HY_CH // POOL_CH

    def halo_spec(g, nxt):
        h = POOL_WINDOWS[g] // 2
        nh = n2 // h
        cb = goff * (POOL_CH // POOL_GROUP) + g
        if nxt:
            return pl.BlockSpec((1, h, r, POOL_GROUP), lambda i, j: (i, ((j + 1) * (tj // h)) % nh, 0, cb))
        return pl.BlockSpec((1, h, r, POOL_GROUP), lambda i, j: (i, (j * (tj // h) + nh - 1) % nh, 0, cb))

    ng = len(POOL_WINDOWS)
    return pl.pallas_call(
        functools.partial(_pool_body, seq_len=seq_len, n2_total=n2),
        grid=(b, n2 // tj),
        in_specs=[pl.BlockSpec((1, tj, r, POOL_CH), lambda i, j: (i, j, 0, goff))]
        + [halo_spec(g, False) for g in range(ng)] + [halo_spec(g, True) for g in range(1, ng)]
        + [_const_spec((ng, POOL_GROUP, POOL_GROUP)), _const_spec((1, POOL_CH)), _const_spec((1, POOL_CH))],
        out_specs=pl.BlockSpec((1, tj, r, POOL_CH), lambda i, j: (i, j, 0, 0)),
        out_shape=jax.ShapeDtypeStruct((b, n2, r, POOL_CH), BF16),
        compiler_params=_cparams("parallel", "arbitrary"),
        name="pool",
    )(*([u] * (2 * ng)), pool_w, pool_scale, g_pool)


def _fft_cfg(seq_len):
    n2 = 128 if seq_len >= 8192 else 64
    r = seq_len // n2
    assert r % SUBLANES == 0 and r * n2 == seq_len
    tj = 8 if seq_len >= 8192 else 32
    return dict(N2=n2, R=r, N1=2 * r, tj=tj, tk=max(tj, 16), ct=256, pitch=n2 + SUBLANES, tj_pool=16)


def _dft_tables(cfg):
    n2, r, n1, tj = cfg["N2"], cfg["R"], cfg["N1"], cfg["tj"]
    n = n1 * n2

    def cis(idx, period):
        a = -2.0 * np.pi * (idx % period) / period
        return np.cos(a), np.sin(a)

    def ri_order(m):
        pos = np.arange(2 * m)
        return (pos % (2 * SUBLANES)) // SUBLANES * m + pos // (2 * SUBLANES) * SUBLANES + pos % SUBLANES

    k1 = np.arange(n1)
    mr, mi = cis(k1[:, None] * k1[None, :], n1)
    mrd, mid = mr[:, :r], mi[:, :r]
    o1, o2 = ri_order(n1), ri_order(n2)
    w1 = np.block([[mrd, -mid], [mid, mrd]])[o1]
    wf = np.concatenate([mr, mi], axis=0)[o1]
    k2 = np.arange(n2)
    fr, fi = cis(k2[:, None] * k2[None, :], n2)
    f2 = np.block([[fr, -fi], [fi, fr]])[o2][:, o2]
    g2 = np.block([[fr, fi], [-fi, fr]])[o2][:, o2]
    twr, twi = cis(k1[:, None] * k2[None, :], n)
    tw = np.stack([twr, twi]).reshape(2, n1, n2 // tj, tj).transpose(2, 0, 1, 3)
    return dict(w1=jnp.asarray(w1, BF16), w1inv=jnp.asarray(w1.T / n, BF16), wf=jnp.asarray(wf, BF16),
                f2=jnp.asarray(f2, BF16), g2=jnp.asarray(g2, BF16), tw=jnp.asarray(tw, F32))


def _to_plane_order(x, cfg):
    b, l, d = x.shape
    xs = jnp.swapaxes(x.reshape(b, cfg["R"], cfg["N2"], d), 1, 2).reshape(b * l, d)
    return lax.optimization_barrier(xs)


def _from_plane_order(x, b, cfg):
    d = x.shape[-1]
    return jnp.swapaxes(x.reshape(b, cfg["N2"], cfg["R"], d), 1, 2).reshape(b, cfg["N2"] * cfg["R"], d)


def _trunk(x, kv, kv_row0, lw, g_final, depth):
    b, seq_len, d = x.shape
    cfg = _fft_cfg(seq_len)
    tabs = _dft_tables(cfg)
    n2, r, ct = cfg["N2"], cfg["R"], cfg["ct"]
    xs = _to_plane_order(x, cfg)
    cpb = HY_CH // ct
    for i in range(depth):
        w = {k: v[i] for k, v in lw.items() if k not in _STACKED_WEIGHTS}
        hid = _filt_hidden(w["filt_w1"], w["filt_b1"], w["filt_w2"], w["filt_b2"], w["filt_freq"], cfg, seq_len)
        w3 = jnp.transpose(w["filt_w3"].reshape(FILT_HIDDEN, 2, 2, HY_CH), (1, 2, 0, 3))
        w3hi = w3.astype(BF16)
        w3lo = (w3 - w3hi.astype(F32)).astype(BF16)
        kf = _filt_spec(hid, jnp.concatenate([w3hi, w3lo, w3hi], axis=2),
                        w["hy_decay"], w["hy_bias"].reshape(2, 1, HY_CH), tabs, cfg, seq_len)
        u = _in_proj(xs.reshape(b, n2, r, d), w["g_mix"], lw["w_in"], i, w["conv_w"], w["conv_b"].reshape(1, -1), cfg)
        z1 = _lconv(u, 2 * cpb, u, 0, kf, 0, tabs, cfg, BF16, "lconv1")
        yh = _lconv(z1, 0, u, cpb, kf, 1, tabs, cfg, F32, "lconv2")
        yp = _pool(u, w["pool_w"], w["pool_scale"].reshape(1, -1), w["g_pool"].reshape(1, -1), cfg, seq_len)
        xs = _outproj(xs, yh.reshape(b * seq_len, HY_CH), yp.reshape(b * seq_len, POOL_CH), w["g_hy"],
                      lw["w_out"], i)
        xs = _attn(xs.reshape(b, seq_len, d), kv, kv_row0, w["g_xa"], lw["w_q"], lw["w_o"], i).reshape(b * seq_len, d)
        xs = _mlp(xs, w["g_mlp"], lw["w_up"], lw["w_down"], i, g_final, i == depth - 1)
    return _from_plane_order(xs, b, cfg)


_STACKED_WEIGHTS = ("w_in", "w_out", "w_q", "w_kv", "w_o", "w_up", "w_down")
_MATMUL_WEIGHTS = _STACKED_WEIGHTS + ("pool_w",)


def kernel(x_prompt, x_sample, mem_prompt, mem_sample, g_mix, w_in, conv_w, conv_b, filt_w1, filt_b1, filt_w2, filt_b2, filt_freq, filt_w3, hy_decay, hy_bias, pool_w, pool_scale, g_hy, g_pool, w_out, g_xa, g_mem, w_q, w_kv, w_o, g_mlp, w_up, w_down, g_final):
    lw = dict(g_mix=g_mix, w_in=w_in, conv_w=conv_w, conv_b=conv_b, filt_w1=filt_w1, filt_b1=filt_b1,
              filt_w2=filt_w2, filt_b2=filt_b2, filt_freq=filt_freq, filt_w3=filt_w3, hy_decay=hy_decay,
              hy_bias=hy_bias, pool_w=pool_w, pool_scale=pool_scale, g_hy=g_hy, g_pool=g_pool, w_out=w_out,
              g_xa=g_xa, g_mem=g_mem, w_q=w_q, w_kv=w_kv, w_o=w_o, g_mlp=g_mlp, w_up=w_up, w_down=w_down)
    lw = {k: (v.astype(BF16) if k in _MATMUL_WEIGHTS else v) for k, v in lw.items()}
    depth = g_mix.shape[0]
    mem = jnp.concatenate([mem_prompt, mem_sample], axis=0)
    kv = _kv_proj(mem.reshape(-1, D_MODEL), lw["g_mem"], lw["w_kv"]).reshape(depth, -1, N_MEM, 2 * D_MODEL)
    y_prompt = _trunk(x_prompt, kv, 0, lw, g_final, depth)
    y_sample = _trunk(x_sample, kv, mem_prompt.shape[0], lw, g_final, depth)
    return (y_prompt, y_sample)
```

```python
import functools
import math

import numpy as np
import jax
import jax.numpy as jnp
from jax import lax
from jax.experimental import pallas as pl
from jax.experimental.pallas import tpu as pltpu

F32 = jnp.float32
BF16 = jnp.bfloat16
U32 = jnp.uint32

D_MODEL = 2048
HY_CH = 1024
POOL_CH = 1024
D_IN = 3 * HY_CH + POOL_CH
POOL_WINDOWS = (2, 4, 8, 16)
POOL_GROUP = POOL_CH // len(POOL_WINDOWS)
POS_BANDS = 16
FILT_HIDDEN = 64
N_MEM = 256
XA_HEADS = 4
XA_HEAD_DIM = D_MODEL // XA_HEADS
EPS = 1e-6

LANES = 128
SUBLANES = 8
VMEM_LIMIT_BYTES = 58 * 1024 * 1024
NORM_CHUNK_ROWS = 256
FILT_LANES = 1024
IN_PROJ_TOKENS = 1024
HY_CONV_BLOCKS = 3
HI = lax.Precision.HIGHEST


def _cparams(*sem):
    return pltpu.CompilerParams(dimension_semantics=sem, vmem_limit_bytes=VMEM_LIMIT_BYTES)


def _rms(x, g):
    return x * lax.rsqrt(jnp.mean(x * x, axis=-1, keepdims=True) + EPS) * g


def _const_spec(shape):
    nd = len(shape)
    return pl.BlockSpec(shape, lambda *_: (0,) * nd)


def _kv_proj_body(x_ref, g_ref, w_ref, o_ref, h_ref):
    @pl.when(pl.program_id(2) == 0)
    def _():
        h_ref[...] = _rms(x_ref[...], g_ref[...]).astype(BF16)

    o_ref[...] = jnp.dot(h_ref[...], w_ref[...], preferred_element_type=F32).astype(o_ref.dtype)


def _kv_proj(mem, g_mem, w_kv, tm=512, tn=1024):
    t, d = mem.shape
    depth, _, n = w_kv.shape
    return pl.pallas_call(
        _kv_proj_body,
        grid=(depth, t // tm, n // tn),
        in_specs=[pl.BlockSpec((tm, d), lambda l, i, j: (i, 0)),
                  pl.BlockSpec((None, 1, d), lambda l, i, j: (l, 0, 0)),
                  pl.BlockSpec((None, d, tn), lambda l, i, j: (l, 0, j))],
        out_specs=pl.BlockSpec((None, tm, tn), lambda l, i, j: (l, i, j)),
        out_shape=jax.ShapeDtypeStruct((depth, t, n), BF16),
        scratch_shapes=[pltpu.VMEM((tm, d), BF16)],
        compiler_params=_cparams("parallel", "parallel", "arbitrary"),
        name="kv_proj",
    )(mem, g_mem.reshape(depth, 1, d), w_kv)


def _in_proj_body(x_ref, xp_ref, xn_ref, g_ref, w_ref, cw_ref, cb_ref, o_ref, h_ref, hh_ref):
    gi, j = pl.program_id(1), pl.program_id(2)
    n2, rt, d = x_ref.shape[1:]
    tn = o_ref.shape[3]

    def emit(blk):
        first, conv = blk == 0, blk < HY_CONV_BLOCKS
        w_blk = w_ref.at[:, blk * tn:(blk + 1) * tn]
        rows = NORM_CHUNK_ROWS
        ppc = rows // rt
        if conv:
            w0, w1, w2, b = cw_ref[0:1, :], cw_ref[1:2, :], cw_ref[2:3, :], cb_ref[...]
            tap = lambda lo_, mid_, hi_: lo_ * w0 + mid_ * w1 + hi_ * w2 + b
        if first:
            hh_ref[...] = _rms(jnp.concatenate([xp_ref[0, 0], xn_ref[0, 0]], axis=0), g_ref[...]).astype(BF16)
        for c in range(n2 // ppc):
            if first:
                h = _rms(x_ref[0, c * ppc:(c + 1) * ppc].reshape(rows, d), g_ref[...]).astype(BF16)
                h_ref[c * rows:(c + 1) * rows, :] = h
            else:
                h = h_ref[c * rows:(c + 1) * rows, :]
            uc = jnp.dot(h, w_blk[...], preferred_element_type=F32).reshape(ppc, rt, tn)
            p0 = c * ppc
            if not conv:
                o_ref[0, p0:p0 + ppc] = uc
                continue
            if c == 0:
                u0, u1 = uc[0], uc[1]
            else:
                o_ref[0, p0 - 1] = tap(t2, t1, uc[0])
                o_ref[0, p0] = tap(t1, uc[0], uc[1])
            o_ref[0, p0 + 1:p0 + ppc - 1] = tap(uc[0:ppc - 2], uc[1:ppc - 1], uc[2:ppc])
            t2, t1 = uc[ppc - 2], uc[ppc - 1]
        if conv:
            uh = jnp.dot(hh_ref[...], w_blk[...], preferred_element_type=F32)
            row = lax.broadcasted_iota(jnp.int32, (rt, tn), 0)
            prev_tok = jnp.where(gi == 0, 0.0, uh[SUBLANES - 1:SUBLANES])
            next_tok = jnp.where(gi == pl.num_programs(1) - 1, 0.0, uh[SUBLANES:SUBLANES + 1])
            wrap_prev = jnp.where(row == 0, prev_tok, pltpu.roll(t1, 1, axis=0))
            wrap_next = jnp.where(row == rt - 1, next_tok, pltpu.roll(u0, rt - 1, axis=0))
            o_ref[0, 0] = tap(wrap_prev, u0, u1)
            o_ref[0, n2 - 1] = tap(t2, t1, wrap_next)

    for blk in range(w_ref.shape[1] // tn):
        pl.when(j == blk)(functools.partial(emit, blk))


def _in_proj(x4, g, w, layer, conv_w, conv_b, cfg):
    b, n2, r, d = x4.shape
    rt = IN_PROJ_TOKENS // n2
    tn = HY_CH
    rg = r // SUBLANES
    return pl.pallas_call(
        _in_proj_body,
        grid=(b, r // rt, D_IN // tn),
        in_specs=[pl.BlockSpec((1, n2, rt, d), lambda i, g_, j: (i, 0, g_, 0)),
                  pl.BlockSpec((1, 1, SUBLANES, d),
                               lambda i, g_, j: (i, n2 - 1, jnp.maximum(g_ * (rt // SUBLANES) - 1, 0), 0)),
                  pl.BlockSpec((1, 1, SUBLANES, d),
                               lambda i, g_, j: (i, 0, jnp.minimum((g_ + 1) * (rt // SUBLANES), rg - 1), 0)),
                  pl.BlockSpec((1, d), lambda i, g_, j: (0, 0)),
                  pl.BlockSpec((None, d, D_IN), lambda i, g_, j: (layer, 0, 0),
                               pipeline_mode=pl.Buffered(1)),
                  pl.BlockSpec((3, tn), lambda i, g_, j: (0, jnp.minimum(j, HY_CONV_BLOCKS - 1))),
                  pl.BlockSpec((1, tn), lambda i, g_, j: (0, jnp.minimum(j, HY_CONV_BLOCKS - 1)))],
        out_specs=pl.BlockSpec((1, n2, rt, tn), lambda i, g_, j: (i, 0, g_, j)),
        out_shape=jax.ShapeDtypeStruct((b, n2, r, D_IN), F32),
        scratch_shapes=[pltpu.VMEM((n2 * rt, d), BF16), pltpu.VMEM((2 * SUBLANES, d), BF16)],
        compiler_params=_cparams("parallel", "parallel", "arbitrary"),
        name="in_proj",
    )(x4, x4, x4, g.reshape(1, d), w, conv_w, conv_b)


def _mlp_body(x_ref, g_ref, wu_ref, wd_ref, gf_ref, o_ref, h_ref, *, final_norm):
    f = pl.program_id(1)
    rows = NORM_CHUNK_ROWS

    def up_down(h):
        hm = jnp.dot(h, wu_ref[...], preferred_element_type=F32)
        a = jnp.square(jnp.maximum(hm, 0.0)).astype(BF16)
        return jnp.dot(a, wd_ref[...], preferred_element_type=F32)

    @pl.when(f == 0)
    def _():
        for c in range(x_ref.shape[0] // rows):
            sl = slice(c * rows, (c + 1) * rows)
            x = x_ref[sl, :]
            h = _rms(x, g_ref[...]).astype(BF16)
            h_ref[sl, :] = h
            o_ref[sl, :] = x + up_down(h)

    @pl.when(f != 0)
    def _():
        o_ref[...] += up_down(h_ref[...])

    if final_norm:
        @pl.when(f == pl.num_programs(1) - 1)
        def _():
            o_ref[...] = _rms(o_ref[...], gf_ref[...])


def _mlp(x, g, wu, wd, layer, g_final, final_norm, tm=512, tf=1024):
    t, d = x.shape
    ff = wu.shape[2]
    return pl.pallas_call(
        functools.partial(_mlp_body, final_norm=final_norm),
        grid=(t // tm, ff // tf),
        in_specs=[pl.BlockSpec((tm, d), lambda i, f: (i, 0)),
                  pl.BlockSpec((1, d), lambda i, f: (0, 0)),
                  pl.BlockSpec((None, d, tf), lambda i, f: (layer, 0, f)),
                  pl.BlockSpec((None, tf, d), lambda i, f: (layer, f, 0)),
                  pl.BlockSpec((1, d), lambda i, f: (0, 0))],
        out_specs=pl.BlockSpec((tm, d), lambda i, f: (i, 0)),
        out_shape=jax.ShapeDtypeStruct((t, d), F32),
        scratch_shapes=[pltpu.VMEM((tm, d), BF16)],
        compiler_params=_cparams("parallel", "arbitrary"),
        name="mlp",
    )(x, g.reshape(1, d), wu, wd, g_final.reshape(1, d))


def _attn_body(x_ref, k_ref, v_ref, g_ref, wq_ref, wo_ref, o_ref):
    scale = XA_HEAD_DIM ** -0.5
    x = x_ref[0]
    h = _rms(x, g_ref[...]).astype(BF16)
    q = jnp.dot(h, wq_ref[...], preferred_element_type=F32)
    heads = []
    for hd in range(XA_HEADS):
        sl = slice(hd * XA_HEAD_DIM, (hd + 1) * XA_HEAD_DIM)
        s = lax.dot_general(q[:, sl].astype(BF16), k_ref[0, :, sl], (((1,), (1,)), ((), ())),
                            preferred_element_type=F32) * scale
        p = jnp.exp(s - jnp.max(s, axis=-1, keepdims=True))
        a = (p / jnp.sum(p, axis=-1, keepdims=True)).astype(BF16)
        heads.append(jnp.dot(a, v_ref[0, :, sl], preferred_element_type=F32).astype(BF16))
    o = jnp.concatenate(heads, axis=-1)
    o_ref[0] = x + jnp.dot(o, wo_ref[...], preferred_element_type=F32)


def _attn(x, kv, kv_row0, g, wq, wo, layer, tq=512):
    b, l, d = x.shape
    return pl.pallas_call(
        _attn_body,
        grid=(b, l // tq),
        in_specs=[pl.BlockSpec((1, tq, d), lambda i, j: (i, j, 0)),
                  pl.BlockSpec((None, 1, N_MEM, d), lambda i, j: (layer, kv_row0 + i, 0, 0)),
                  pl.BlockSpec((None, 1, N_MEM, d), lambda i, j: (layer, kv_row0 + i, 0, 1)),
                  pl.BlockSpec((1, d), lambda i, j: (0, 0)),
                  pl.BlockSpec((None, d, d), lambda i, j: (layer, 0, 0)),
                  pl.BlockSpec((None, d, d), lambda i, j: (layer, 0, 0))],
        out_specs=pl.BlockSpec((1, tq, d), lambda i, j: (i, j, 0)),
        out_shape=jax.ShapeDtypeStruct((b, l, d), F32),
        compiler_params=_cparams("parallel", "arbitrary"),
        name="attn",
    )(x, kv, kv, g.reshape(1, d), wq, wo)


def _outproj_body(x_ref, yh_ref, yp_ref, gh_ref, wt_ref, wb_ref, o_ref):
    yh = _rms(yh_ref[...], gh_ref[...]).astype(BF16)
    acc = jnp.dot(yh, wt_ref[...], preferred_element_type=F32)
    acc += jnp.dot(yp_ref[...], wb_ref[...], preferred_element_type=F32)
    o_ref[...] = x_ref[...] + acc


def _outproj(x, yh, yp, g_hy, w_out, layer, tm=512):
    t, d = x.shape
    return pl.pallas_call(
        _outproj_body,
        grid=(t // tm,),
        in_specs=[pl.BlockSpec((tm, d), lambda i: (i, 0)),
                  pl.BlockSpec((tm, HY_CH), lambda i: (i, 0)),
                  pl.BlockSpec((tm, POOL_CH), lambda i: (i, 0)),
                  pl.BlockSpec((1, HY_CH), lambda i: (0, 0)),
                  pl.BlockSpec((None, HY_CH, d), lambda i: (layer, 0, 0)),
                  pl.BlockSpec((None, POOL_CH, d), lambda i: (layer, 1, 0))],
        out_specs=pl.BlockSpec((tm, d), lambda i: (i, 0)),
        out_shape=jax.ShapeDtypeStruct((t, d), F32),
        compiler_params=_cparams("parallel"),
        name="outproj",
    )(x, yh, yp, g_hy.reshape(1, HY_CH), w_out, w_out)


def _shift_rows(x, down):
    r = x.shape[-2]
    rows = lax.broadcasted_iota(jnp.int32, x.shape, x.ndim - 2)
    if down:
        return jnp.where(rows == 0, 0.0, pltpu.roll(x, 1, axis=x.ndim - 2))
    return jnp.where(rows == r - 1, 0.0, pltpu.roll(x, r - 1, axis=x.ndim - 2))


def _wrap_prev(x, is_first):
    return jnp.where(is_first, _shift_rows(x, True), x)


def _wrap_next(x, is_last):
    return jnp.where(is_last, _shift_rows(x, False), x)


def _pack(re, im):
    half = jnp.uint32(0x8000)
    rb = (lax.bitcast_convert_type(re, U32) + half) & jnp.uint32(0xFFFF0000)
    return rb | ((lax.bitcast_convert_type(im, U32) + half) >> 16)


def _unpack(p):
    return (lax.bitcast_convert_type(p & jnp.uint32(0xFFFF0000), F32),
            lax.bitcast_convert_type(p << 16, F32))


def _split_ri(x):
    m, c = x.shape
    x4 = x.reshape(m // (2 * SUBLANES), 2, SUBLANES, c)
    return x4[:, 0].reshape(m // 2, c), x4[:, 1].reshape(m // 2, c)


def _join_ri(re, im):
    n, c = re.shape
    g = n // SUBLANES
    return jnp.stack([re.reshape(g, SUBLANES, c), im.reshape(g, SUBLANES, c)], axis=1).reshape(2 * n, c)


def _store_cols(scr, n2i, n1, pitch, val):
    for l in range(scr.shape[0]):
        scr[l, pl.ds(n2i, n1, stride=pitch), :] = val[:, l * LANES:(l + 1) * LANES]


def _load_cols(scr, n2i, n1, pitch):
    return jnp.concatenate([scr[l, pl.ds(n2i, n1, stride=pitch), :] for l in range(scr.shape[0])], axis=1)


def _store_rows(scr, base, n2, val):
    for l in range(scr.shape[0]):
        scr[l, pl.ds(base, n2), :] = val[:, l * LANES:(l + 1) * LANES]


def _load_rows(scr, base, n2):
    return jnp.concatenate([scr[l, pl.ds(base, n2), :] for l in range(scr.shape[0])], axis=1)


def _stage1_to_scratch(x_bf16, w_ref, tw_ref, jj, scr, n2i, n1, pitch):
    res = jnp.dot(w_ref[...], x_bf16, preferred_element_type=F32)
    ar, ai = _split_ri(res)
    twr, twi = tw_ref[0, 0, :, jj:jj + 1], tw_ref[0, 1, :, jj:jj + 1]
    _store_cols(scr, n2i, n1, pitch, _pack(ar * twr - ai * twi, ar * twi + ai * twr))


def _stage2_fwd(scr, base, n2, f_ref):
    xr, xi = _unpack(_load_rows(scr, base, n2))
    s = jnp.dot(f_ref[...], _join_ri(xr, xi).astype(BF16), preferred_element_type=F32)
    return _split_ri(s)


def _lconv_body(vm_ref, gm_ref, kf_ref, w_ref, wi_ref, f_ref, g_ref, tw_ref, o_ref, scr,
                *, n2, n1, pitch, tj, tk, s1, s2):
    s = pl.program_id(2)

    @pl.when(s < s1)
    def _():
        for jj in range(tj):
            v = vm_ref[:, jj]
            nb, r, ct = v.shape
            _stage1_to_scratch(v.reshape(nb * r, ct).astype(BF16), w_ref, tw_ref, jj, scr,
                               s * tj + jj, n1, pitch)

    @pl.when(jnp.logical_and(s >= s1, s < s1 + s2))
    def _():
        for kk in range(tk):
            base = pl.multiple_of(((s - s1) * tk + kk) * pitch, SUBLANES)
            sr, si = _stage2_fwd(scr, base, n2, f_ref)
            kr, ki = _unpack(kf_ref[0, kk])
            y = _join_ri(sr * kr - si * ki, sr * ki + si * kr).astype(BF16)
            a = jnp.dot(g_ref[...], y, preferred_element_type=F32)
            _store_rows(scr, base, n2, _pack(*_split_ri(a)))

    @pl.when(s >= s1 + s2)
    def _():
        jp = s - s1 - s2
        for jj in range(tj):
            gate = gm_ref[:, jj]
            ar, ai = _unpack(_load_cols(scr, jp * tj + jj, n1, pitch))
            twr, twi = tw_ref[0, 0, :, jj:jj + 1], tw_ref[0, 1, :, jj:jj + 1]
            x = _join_ri(ar * twr + ai * twi, ai * twr - ar * twi).astype(BF16)
            y = jnp.dot(wi_ref[...], x, preferred_element_type=F32)
            o_ref[:, jj] = (gate * y.reshape(gate.shape)).astype(o_ref.dtype)


def _lconv(vsrc, v_goff, u, g_goff, kf, order, tabs, cfg, out_dtype, name):
    n2, r, n1, tj, tk, ct, pitch = (cfg[k] for k in ("N2", "R", "N1", "tj", "tk", "ct", "pitch"))
    p = vsrc.shape[0] // 2
    s1, s2 = n2 // tj, n1 // tk

    def jv(s):
        return jnp.minimum(s, s1 - 1)

    def jg(s):
        return jnp.maximum(s - s1 - s2, 0)

    def jt(s):
        return jnp.where(s < s1, s, jg(s))

    def plane_spec(goff, jf):
        return pl.BlockSpec((2, tj, r, ct), lambda p_, c, s: (p_, jf(s), 0, goff + c))

    in_specs = [plane_spec(v_goff, jv), plane_spec(g_goff, jg)]
    args = [vsrc, u]
    in_specs += [pl.BlockSpec((1, tk, n2, ct), lambda p_, c, s: (order, jnp.clip(s - s1, 0, s2 - 1), 0, c)),
                 _const_spec((2 * n1, 2 * r)), _const_spec((2 * r, 2 * n1)),
                 _const_spec((2 * n2, 2 * n2)), _const_spec((2 * n2, 2 * n2)),
                 pl.BlockSpec((1, 2, n1, tj), lambda p_, c, s: (jt(s), 0, 0, 0))]
    args += [kf, tabs["w1"], tabs["w1inv"], tabs["f2"], tabs["g2"], tabs["tw"]]
    return pl.pallas_call(
        functools.partial(_lconv_body, n2=n2, n1=n1, pitch=pitch, tj=tj, tk=tk, s1=s1, s2=s2),
        grid=(p, HY_CH // ct, 2 * s1 + s2),
        in_specs=in_specs,
        out_specs=pl.BlockSpec((2, tj, r, ct), lambda p_, c, s: (p_, jg(s), 0, c)),
        out_shape=jax.ShapeDtypeStruct((2 * p, n2, r, HY_CH), out_dtype),
        scratch_shapes=[pltpu.VMEM((ct // LANES, n1 * pitch, LANES), U32)],
        compiler_params=_cparams("parallel", "arbitrary", "arbitrary"),
        name=name,
    )(*args)


def _filt_hidden_body(bands_ref, w1t_ref, w1c_ref, w1s_ref, b1_ref, w2_ref, b2_ref, fr_ref, h_ref,
                      *, seq_len, n2_total):
    tjh, n1 = h_ref.shape[0], h_ref.shape[1]
    lanes = tjh * n1
    idx = lax.broadcasted_iota(jnp.int32, (1, lanes), 1)
    row = idx & (n1 - 1)
    n2 = pl.program_id(0) * tjh + (idx >> int(math.log2(n1)))
    pos_b = jnp.where(jnp.logical_and(row == n1 // 2, n2 == 0), 0, (n1 - row) * n2_total - n2)
    pos = jnp.where(row < n1 // 2, row * n2_total + n2, pos_b).astype(F32)
    t = pos / float(max(seq_len - 1, 1))
    ang = bands_ref[...] * ((2.0 * math.pi / seq_len) * pos)
    pre = (w1t_ref[...] * t + jnp.dot(w1c_ref[...], jnp.cos(ang), precision=HI, preferred_element_type=F32)
           + jnp.dot(w1s_ref[...], -jnp.sin(ang), precision=HI, preferred_element_type=F32) + b1_ref[...])
    h = jnp.sin(fr_ref[...] * pre)
    h = jnp.sin(fr_ref[...] * (jnp.dot(w2_ref[...], h, precision=HI, preferred_element_type=F32) + b2_ref[...]))
    h = h.T
    hi = h.astype(BF16)
    lo = (h - hi.astype(F32)).astype(BF16)
    h_ref[...] = jnp.concatenate([hi, hi, lo], axis=1).reshape(h_ref.shape)


def _filt_hidden(w1, b1, w2, b2, freq, cfg, seq_len):
    n2, n1 = cfg["N2"], cfg["N1"]
    tjh = FILT_LANES // n1
    bands = np.linspace(1e-4, POS_BANDS - 1, POS_BANDS, dtype=np.float32).reshape(POS_BANDS, 1)
    col = lambda v: v.reshape(FILT_HIDDEN, 1)
    args = (jnp.asarray(bands), col(w1[0]), w1[1:1 + POS_BANDS].T, w1[1 + POS_BANDS:].T, col(b1), w2.T, col(b2),
            col(freq))
    return pl.pallas_call(
        functools.partial(_filt_hidden_body, seq_len=seq_len, n2_total=n2),
        grid=(n2 // tjh,),
        in_specs=[_const_spec(a.shape) for a in args],
        out_specs=pl.BlockSpec((tjh, n1, 3 * FILT_HIDDEN), lambda j: (j, 0, 0)),
        out_shape=jax.ShapeDtypeStruct((n2, n1, 3 * FILT_HIDDEN), BF16),
        compiler_params=_cparams("parallel"),
        name="filt_hidden",
    )(*args)


def _filt_spec_body(h_ref, w3_ref, dec_ref, bias_ref, wf_ref, f_ref, tw_ref, o_ref, scr, nrm_ref, win_ref,
                    *, seq_len, n2, n1, pitch, tj, tk, s1):
    s = pl.program_id(2)
    r = n1 // 2
    ct = o_ref.shape[-1]
    rate = jnp.abs(dec_ref[0]) * (1.0 / float(max(seq_len - 1, 1)))

    @pl.when(s == 0)
    def _():
        nrm_ref[...] = jnp.zeros_like(nrm_ref)
        rowf = lax.broadcasted_iota(jnp.int32, (r, ct), 0).astype(F32) * float(n2)
        win_ref[0] = jnp.exp(-rowf * rate[0:1])
        win_ref[1] = jnp.exp(-(float((r - 1) * n2) - rowf) * rate[1:2])

    @pl.when(s < s1)
    def _():
        row = lax.broadcasted_iota(jnp.int32, (r, ct), 0)
        for jj in range(tj):
            n2i = s * tj + jj
            n2f = n2i.astype(F32)
            tap0 = jnp.logical_and(row == 0, n2i == 0)
            halves = []
            for d in range(2):
                raw = jnp.dot(h_ref[jj, d * r:(d + 1) * r, :], w3_ref[0, d], preferred_element_type=F32)
                plane = jnp.exp(-(float(n2) - n2f if d == 1 else n2f) * rate[d:d + 1])
                val = raw * (win_ref[d] * plane)
                if d == 1:
                    nrm_ref[...] += jnp.sum(jnp.abs(jnp.where(tap0, raw, val)), axis=0, keepdims=True)
                    val = jnp.where(tap0, 0.0, val)
                else:
                    nrm_ref[...] += jnp.sum(jnp.abs(val), axis=0, keepdims=True)
                halves.append(val)
            _stage1_to_scratch(jnp.concatenate(halves, axis=0).astype(BF16), wf_ref, tw_ref, jj, scr,
                               n2i, n1, pitch)

    @pl.when(s >= s1)
    def _():
        inv = 1.0 / nrm_ref[...]
        bias = bias_ref[0]
        for kk in range(tk):
            base = pl.multiple_of(((s - s1) * tk + kk) * pitch, SUBLANES)
            sr, si = _stage2_fwd(scr, base, n2, f_ref)
            o_ref[0, kk] = _pack(sr * inv + bias, si * inv)


def _filt_spec(hid, w3, dec, bias, tabs, cfg, seq_len):
    n2, n1, tj, tk, ct, pitch = (cfg[k] for k in ("N2", "N1", "tj", "tk", "ct", "pitch"))
    s1, s2 = n2 // tj, n1 // tk
    return pl.pallas_call(
        functools.partial(_filt_spec_body, seq_len=seq_len, n2=n2, n1=n1, pitch=pitch, tj=tj, tk=tk, s1=s1),
        grid=(2, HY_CH // ct, s1 + s2),
        in_specs=[pl.BlockSpec((tj, n1, 3 * FILT_HIDDEN), lambda o, c, s: (jnp.minimum(s, s1 - 1), 0, 0)),
                  pl.BlockSpec((1, 2, 3 * FILT_HIDDEN, ct), lambda o, c, s: (o, 0, 0, c)),
                  pl.BlockSpec((1, 2, ct), lambda o, c, s: (o, 0, c)),
                  pl.BlockSpec((1, 1, ct), lambda o, c, s: (o, 0, c)),
                  _const_spec((2 * n1, n1)), _const_spec((2 * n2, 2 * n2)),
                  pl.BlockSpec((1, 2, n1, tj), lambda o, c, s: (jnp.minimum(s, s1 - 1), 0, 0, 0))],
        out_specs=pl.BlockSpec((1, tk, n2, ct), lambda o, c, s: (o, jnp.maximum(s - s1, 0), 0, c)),
        out_shape=jax.ShapeDtypeStruct((2, n1, n2, HY_CH), U32),
        scratch_shapes=[pltpu.VMEM((ct // LANES, n1 * pitch, LANES), U32), pltpu.VMEM((1, ct), F32),
                        pltpu.VMEM((2, n1 // 2, ct), F32)],
        compiler_params=_cparams("parallel", "arbitrary", "arbitrary"),
        name="filt_spec",
    )(hid, w3, dec, bias, tabs["wf"], tabs["f2"], tabs["tw"])


def _pool_body(*refs, seq_len, n2_total):
    ng = len(POOL_WINDOWS)
    m_ref, p_refs, n_refs = refs[0], refs[1:1 + ng], refs[1 + ng:2 * ng]
    pw_ref, ps_ref, gp_ref, o_ref = refs[2 * ng:]
    tj, r = m_ref.shape[1], m_ref.shape[2]
    j = pl.program_id(1)
    first, last = j == 0, j == pl.num_programs(1) - 1
    j0 = j * tj
    plane = lax.broadcasted_iota(jnp.int32, (tj, r, LANES), 0)
    row = lax.broadcasted_iota(jnp.int32, (tj, r, LANES), 1)
    t = row * n2_total + j0 + plane
    outs = []
    for g, win in enumerate(POOL_WINDOWS):
        sl = slice(g * POOL_GROUP, (g + 1) * POOL_GROUP)
        half = win // 2
        main = m_ref[0, :, :, sl]
        parts = [_wrap_prev(p_refs[g][0], first), main]
        if half > 1:
            parts.append(_wrap_next(n_refs[g - 1][0, 0:half - 1], last))
        acc = jnp.concatenate(parts, axis=0)
        width = 1
        while width < win:
            n = acc.shape[0] - width
            acc = acc[:n] + acc[width:width + n]
            width *= 2
        cnt = (jnp.minimum(t + half - 1, seq_len - 1) - jnp.maximum(t - half, 0) + 1).astype(F32)
        dgrp = acc / jnp.concatenate([cnt] * (POOL_GROUP // LANES), axis=-1) - main
        outs.append(jnp.dot(dgrp.reshape(tj * r, POOL_GROUP).astype(BF16), pw_ref[g],
                            preferred_element_type=F32))
    y = jnp.concatenate(outs, axis=-1) * ps_ref[...]
    o_ref[0] = _rms(y, gp_ref[...]).astype(BF16).reshape(tj, r, POOL_CH)


def _pool(u, pool_w, pool_scale, g_pool, cfg, seq_len):
    n2, r, tj = cfg["N2"], cfg["R"], cfg["tj_pool"]
    b = u.shape[0]
    goff = 3 * HY_CH // POOL_CH

    def halo_spec(g, nxt):
        h = POOL_WINDOWS[g] // 2
        nh = n2 // h
        cb = goff * (POOL_CH // POOL_GROUP) + g
        if nxt:
            return pl.BlockSpec((1, h, r, POOL_GROUP), lambda i, j: (i, ((j + 1) * (tj // h)) % nh, 0, cb))
        return pl.BlockSpec((1, h, r, POOL_GROUP), lambda i, j: (i, (j * (tj // h) + nh - 1) % nh, 0, cb))

    ng = len(POOL_WINDOWS)
    return pl.pallas_call(
        functools.partial(_pool_body, seq_len=seq_len, n2_total=n2),
        grid=(b, n2 // tj),
        in_specs=[pl.BlockSpec((1, tj, r, POOL_CH), lambda i, j: (i, j, 0, goff))]
        + [halo_spec(g, False) for g in range(ng)] + [halo_spec(g, True) for g in range(1, ng)]
        + [_const_spec((ng, POOL_GROUP, POOL_GROUP)), _const_spec((1, POOL_CH)), _const_spec((1, POOL_CH))],
        out_specs=pl.BlockSpec((1, tj, r, POOL_CH), lambda i, j: (i, j, 0, 0)),
        out_shape=jax.ShapeDtypeStruct((b, n2, r, POOL_CH), BF16),
        compiler_params=_cparams("parallel", "arbitrary"),
        name="pool",
    )(*([u] * (2 * ng)), pool_w, pool_scale, g_pool)


def _fft_cfg(seq_len):
    n2 = 128 if seq_len >= 8192 else 64
    r = seq_len // n2
    assert r % SUBLANES == 0 and r * n2 == seq_len
    tj = 8 if seq_len >= 8192 else 32
    return dict(N2=n2, R=r, N1=2 * r, tj=tj, tk=max(tj, 16), ct=256, pitch=n2 + SUBLANES, tj_pool=16)


def _dft_tables(cfg):
    n2, r, n1, tj = cfg["N2"], cfg["R"], cfg["N1"], cfg["tj"]
    n = n1 * n2

    def cis(idx, period):
        a = -2.0 * np.pi * (idx % period) / period
        return np.cos(a), np.sin(a)

    def ri_order(m):
        pos = np.arange(2 * m)
        return (pos % (2 * SUBLANES)) // SUBLANES * m + pos // (2 * SUBLANES) * SUBLANES + pos % SUBLANES

    k1 = np.arange(n1)
    mr, mi = cis(k1[:, None] * k1[None, :], n1)
    mrd, mid = mr[:, :r], mi[:, :r]
    o1, o2 = ri_order(n1), ri_order(n2)
    w1 = np.block([[mrd, -mid], [mid, mrd]])[o1]
    wf = np.concatenate([mr, mi], axis=0)[o1]
    k2 = np.arange(n2)
    fr, fi = cis(k2[:, None] * k2[None, :], n2)
    f2 = np.block([[fr, -fi], [fi, fr]])[o2][:, o2]
    g2 = np.block([[fr, fi], [-fi, fr]])[o2][:, o2]
    twr, twi = cis(k1[:, None] * k2[None, :], n)
    tw = np.stack([twr, twi]).reshape(2, n1, n2 // tj, tj).transpose(2, 0, 1, 3)
    return dict(w1=jnp.asarray(w1, BF16), w1inv=jnp.asarray(w1.T / n, BF16), wf=jnp.asarray(wf, BF16),
                f2=jnp.asarray(f2, BF16), g2=jnp.asarray(g2, BF16), tw=jnp.asarray(tw, F32))


def _to_plane_order(x, cfg):
    b, l, d = x.shape
    xs = jnp.swapaxes(x.reshape(b, cfg["R"], cfg["N2"], d), 1, 2).reshape(b * l, d)
    return lax.optimization_barrier(xs)


def _from_plane_order(x, b, cfg):
    d = x.shape[-1]
    return jnp.swapaxes(x.reshape(b, cfg["N2"], cfg["R"], d), 1, 2).reshape(b, cfg["N2"] * cfg["R"], d)


def _trunk(x, kv, kv_row0, lw, g_final, depth):
    b, seq_len, d = x.shape
    cfg = _fft_cfg(seq_len)
    tabs = _dft_tables(cfg)
    n2, r, ct = cfg["N2"], cfg["R"], cfg["ct"]
    xs = _to_plane_order(x, cfg)
    cpb = HY_CH // ct
    for i in range(depth):
        w = {k: v[i] for k, v in lw.items() if k not in _STACKED_WEIGHTS}
        hid = _filt_hidden(w["filt_w1"], w["filt_b1"], w["filt_w2"], w["filt_b2"], w["filt_freq"], cfg, seq_len)
        w3 = jnp.transpose(w["filt_w3"].reshape(FILT_HIDDEN, 2, 2, HY_CH), (1, 2, 0, 3))
        w3hi = w3.astype(BF16)
        w3lo = (w3 - w3hi.astype(F32)).astype(BF16)
        kf = _filt_spec(hid, jnp.concatenate([w3hi, w3lo, w3hi], axis=2),
                        w["hy_decay"], w["hy_bias"].reshape(2, 1, HY_CH), tabs, cfg, seq_len)
        u = _in_proj(xs.reshape(b, n2, r, d), w["g_mix"], lw["w_in"], i, w["conv_w"], w["conv_b"].reshape(1, -1), cfg)
        z1 = _lconv(u, 2 * cpb, u, 0, kf, 0, tabs, cfg, BF16, "lconv1")
        yh = _lconv(z1, 0, u, cpb, kf, 1, tabs, cfg, F32, "lconv2")
        yp = _pool(u, w["pool_w"], w["pool_scale"].reshape(1, -1), w["g_pool"].reshape(1, -1), cfg, seq_len)
        xs = _outproj(xs, yh.reshape(b * seq_len, HY_CH), yp.reshape(b * seq_len, POOL_CH), w["g_hy"],
                      lw["w_out"], i)
        xs = _attn(xs.reshape(b, seq_len, d), kv, kv_row0, w["g_xa"], lw["w_q"], lw["w_o"], i).reshape(b * seq_len, d)
        xs = _mlp(xs, w["g_mlp"], lw["w_up"], lw["w_down"], i, g_final, i == depth - 1)
    return _from_plane_order(xs, b, cfg)


_STACKED_WEIGHTS = ("w_in", "w_out", "w_q", "w_kv", "w_o", "w_up", "w_down")
_MATMUL_WEIGHTS = _STACKED_WEIGHTS + ("pool_w",)


def kernel(x_prompt, x_sample, mem_prompt, mem_sample, g_mix, w_in, conv_w, conv_b, filt_w1, filt_b1, filt_w2, filt_b2, filt_freq, filt_w3, hy_decay, hy_bias, pool_w, pool_scale, g_hy, g_pool, w_out, g_xa, g_mem, w_q, w_kv, w_o, g_mlp, w_up, w_down, g_final):
    lw = dict(g_mix=g_mix, w_in=w_in, conv_w=conv_w, conv_b=conv_b, filt_w1=filt_w1, filt_b1=filt_b1,
              filt_w2=filt_w2, filt_b2=filt_b2, filt_freq=filt_freq, filt_w3=filt_w3, hy_decay=hy_decay,
              hy_bias=hy_bias, pool_w=pool_w, pool_scale=pool_scale, g_hy=g_hy, g_pool=g_pool, w_out=w_out,
              g_xa=g_xa, g_mem=g_mem, w_q=w_q, w_kv=w_kv, w_o=w_o, g_mlp=g_mlp, w_up=w_up, w_down=w_down)
    lw = {k: (v.astype(BF16) if k in _MATMUL_WEIGHTS else v) for k, v in lw.items()}
    depth = g_mix.shape[0]
    mem = jnp.concatenate([mem_prompt, mem_sample], axis=0)
    kv = _kv_proj(mem.reshape(-1, D_MODEL), lw["g_mem"], lw["w_kv"]).reshape(depth, -1, N_MEM, 2 * D_MODEL)
    y_prompt = _trunk(x_prompt, kv, 0, lw, g_final, depth)
    y_sample = _trunk(x_sample, kv, mem_prompt.shape[0], lw, g_final, depth)
    return (y_prompt, y_sample)
```
